```python
import math
import jax, jax.numpy as jnp
from jax import lax
import numpy as np

D_MODEL = 2048
BATCH = 4
SEQ = 2048
DEPTH = 4

HEAD_DIM = 128
MEM_LEN = 256
MEM_HEADS = 4
A_HEADS = 6
A_PATTERNS = ((128, 1), (512, 4), (2048, 16))
B_HEADS = 6
MOBA_BLOCK = 256
MOBA_TOPK = 3
MOBA_QCHUNK = 32
C_HEADS = 24
C_KV_HEADS = 3
C_HEAD_DIM = 64
C_WINDOW = 128
BAND_BLOCK = 128
ROPE_THETA = 10000.0
EPS = 1e-6
NEG = -1e30

EVEN_WIDTH = (A_HEADS + B_HEADS + MEM_HEADS) * HEAD_DIM
ODD_WIDTH = C_HEADS * C_HEAD_DIM + MEM_HEADS * HEAD_DIM
EVEN_SPLITS = [A_HEADS * HEAD_DIM] * 3 + [B_HEADS * HEAD_DIM] * 3 + [MEM_HEADS * HEAD_DIM, EVEN_WIDTH]
ODD_SPLITS = [C_HEADS * C_HEAD_DIM, C_KV_HEADS * C_HEAD_DIM, C_KV_HEADS * C_HEAD_DIM, MEM_HEADS * HEAD_DIM, ODD_WIDTH]
EVEN_IN = sum(EVEN_SPLITS)
ODD_IN = sum(ODD_SPLITS)
EVEN_OFFSETS = [sum(EVEN_SPLITS[:i + 1]) for i in range(len(EVEN_SPLITS) - 1)]
ODD_OFFSETS = [sum(ODD_SPLITS[:i + 1]) for i in range(len(ODD_SPLITS) - 1)]

kernel_name = "hybrid_dilated_moba_swa_sink_decoder"


def rmsnorm(x, g):
    xf = x.astype(jnp.float32)
    y = xf * lax.rsqrt(jnp.mean(xf * xf, axis=-1, keepdims=True) + EPS)
    return (y * g.astype(jnp.float32)).astype(x.dtype)


def rope(x, pos):
    d = x.shape[-1]
    half = d // 2
    inv_freq = jnp.exp(jnp.arange(half, dtype=jnp.float32) * (-2.0 * math.log(ROPE_THETA) / d))
    ang = pos.astype(jnp.float32)[:, None, :, None] * inv_freq
    cos, sin = jnp.cos(ang), jnp.sin(ang)
    xf = x.astype(jnp.float32)
    x1, x2 = xf[..., :half], xf[..., half:]
    return jnp.concatenate([x1 * cos - x2 * sin, x2 * cos + x1 * sin], axis=-1).astype(x.dtype)


def split_heads(t, n):
    b, s, _ = t.shape
    return t.reshape(b, s, n, -1).transpose(0, 2, 1, 3)


def merge_heads(t):
    b, h, s, d = t.shape
    return t.transpose(0, 2, 1, 3).reshape(b, s, h * d)


def banded_attention(q, k, v, max_dist, sink=None):
    n, r, L, d = q.shape
    blk = BAND_BLOCK
    nb = -(-L // blk)
    pad = nb * blk - L
    qp = jnp.pad(q, ((0, 0), (0, 0), (0, pad), (0, 0))).reshape(n, r, nb, blk, d)
    kc = jnp.pad(k, ((0, 0), (0, pad), (0, 0))).reshape(n, nb, blk, d)
    vc = jnp.pad(v, ((0, 0), (0, pad), (0, 0))).reshape(n, nb, blk, d)
    kb = jnp.concatenate([jnp.pad(kc[:, :-1], ((0, 0), (1, 0), (0, 0), (0, 0))), kc], axis=2)
    vb = jnp.concatenate([jnp.pad(vc[:, :-1], ((0, 0), (1, 0), (0, 0), (0, 0))), vc], axis=2)
    s = jnp.einsum('nrbqd,nbkd->nrbqk', qp, kb, preferred_element_type=jnp.float32) * (d ** -0.5)
    qi = jnp.arange(blk)[:, None]
    ki = jnp.arange(2 * blk)[None, :]
    dist = qi + blk - ki
    k_abs = jnp.arange(nb)[:, None, None] * blk - blk + ki[None]
    mask = ((dist >= 0) & (dist <= max_dist))[None] & (k_abs >= 0)
    s = jnp.where(mask, s, NEG)
    lse = jax.nn.logsumexp(s, axis=-1)
    if sink is not None:
        lse = jnp.logaddexp(lse, sink[:, :, None, None])
    p = jnp.exp(s - lse[..., None])
    out = jnp.einsum('nrbqk,nbkd->nrbqd', p.astype(v.dtype), vb)
    return out.reshape(n, r, nb * blk, d)[:, :, :L], lse.reshape(n, r, nb * blk)[:, :, :L]


def dilated_mixture_attention(q, k, v):
    b, h, s, d = q.shape
    outs, lses = [], []
    for window, dil in A_PATTERNS:
        L = s // dil

        def fold(t):
            return t.reshape(b, h, L, dil, d).transpose(0, 1, 3, 2, 4).reshape(b * h * dil, L, d)

        o, l = banded_attention(fold(q)[:, None], fold(k), fold(v), window // dil)
        outs.append(o[:, 0].reshape(b, h, dil, L, d).transpose(0, 1, 3, 2, 4).reshape(b, h, s, d))
        lses.append(l[:, 0].reshape(b, h, dil, L).transpose(0, 1, 3, 2).reshape(b, h, s))
    w = jax.nn.softmax(jnp.stack(lses), axis=0)
    out = jnp.einsum('gbhs,gbhsd->bhsd', w, jnp.stack(outs).astype(jnp.float32))
    return out.astype(q.dtype)


def moba_attention(q, k, v):
    b, h, s, d = q.shape
    scale = d ** -0.5
    nblk = -(-s // MOBA_BLOCK)
    sp = nblk * MOBA_BLOCK
    kp = jnp.pad(k, ((0, 0), (0, 0), (0, sp - s), (0, 0)))
    vp = jnp.pad(v, ((0, 0), (0, 0), (0, sp - s), (0, 0)))
    kblk = kp.reshape(b, h, nblk, MOBA_BLOCK, d)
    vblk = vp.reshape(b, h, nblk, MOBA_BLOCK, d)
    kmean = jnp.mean(kblk.astype(jnp.float32), axis=3)
    gate = jnp.einsum('bhsd,bhnd->bhsn', q.astype(jnp.float32), kmean)
    own = jnp.arange(s) // MOBA_BLOCK
    past = jnp.arange(nblk)[None, :] < own[:, None]
    gate = jnp.where(past, gate, NEG)
    topk = min(MOBA_TOPK, nblk)
    _, sel = lax.top_k(gate, topk)
    sel_valid = jnp.arange(topk)[None, :] < own[:, None]
    qc = MOBA_QCHUNK
    nq = s // qc
    nsel = topk * MOBA_BLOCK
    bi = jnp.arange(b)[:, None, None]
    hi = jnp.arange(h)[None, :, None]

    def chunk(args):
        c, q_c, sel_c, valid_c = args
        flat = sel_c.reshape(b, h, qc * topk)
        k_sel = kblk[bi, hi, flat].reshape(b, h, qc, nsel, d)
        v_sel = vblk[bi, hi, flat].reshape(b, h, qc, nsel, d)
        s_sel = jnp.einsum('bhqd,bhqkd->bhqk', q_c, k_sel, preferred_element_type=jnp.float32) * scale
        s_sel = jnp.where(jnp.repeat(valid_c, MOBA_BLOCK, axis=1), s_sel, NEG)
        start = (c * qc) // MOBA_BLOCK * MOBA_BLOCK
        k_own = lax.dynamic_slice_in_dim(kp, start, MOBA_BLOCK, axis=2)
        v_own = lax.dynamic_slice_in_dim(vp, start, MOBA_BLOCK, axis=2)
        s_own = jnp.einsum('bhqd,bhkd->bhqk', q_c, k_own, preferred_element_type=jnp.float32) * scale
        qpos = c * qc + jnp.arange(qc)
        kpos = start + jnp.arange(MOBA_BLOCK)
        s_own = jnp.where(kpos[None, :] <= qpos[:, None], s_own, NEG)
        p = jax.nn.softmax(jnp.concatenate([s_sel, s_own], axis=-1), axis=-1).astype(v.dtype)
        return (jnp.einsum('bhqk,bhqkd->bhqd', p[..., :nsel], v_sel)
                + jnp.einsum('bhqk,bhkd->bhqd', p[..., nsel:], v_own))

    xs = (jnp.arange(nq),
          q.reshape(b, h, nq, qc, d).transpose(2, 0, 1, 3, 4),
          sel.reshape(b, h, nq, qc, topk).transpose(2, 0, 1, 3, 4),
          sel_valid.reshape(nq, qc, topk))
    out = lax.map(chunk, xs)
    return out.transpose(1, 2, 0, 3, 4).reshape(b, h, s, d)


def swa_sink_attention(q, k, v, sinks):
    b, hq, s, d = q.shape
    g = k.shape[1]
    r = hq // g
    sink = jnp.broadcast_to(sinks.astype(jnp.float32).reshape(1, g, r), (b, g, r)).reshape(b * g, r)
    o, _ = banded_attention(q.reshape(b * g, r, s, d), k.reshape(b * g, s, d), v.reshape(b * g, s, d),
                            C_WINDOW - 1, sink)
    return o.reshape(b, hq, s, d)


def memory_attention(q, mk, mv):
    s = jnp.einsum('bhsd,bhmd->bhsm', q, mk, preferred_element_type=jnp.float32) * (q.shape[-1] ** -0.5)
    p = jax.nn.softmax(s, axis=-1).astype(mv.dtype)
    return jnp.einsum('bhsm,bhmd->bhsd', p, mv)


def memory_kv(mem_n, w_mem_kv):
    mk, mv = jnp.split(mem_n @ w_mem_kv, 2, axis=-1)
    return split_heads(mk, MEM_HEADS), split_heads(mv, MEM_HEADS)


def even_mixer(h, pos, mem_n, w_in, w_mem_kv, w_out):
    qa, ka, va, qb, kb, vb, qm, gate = jnp.split(h @ w_in, EVEN_OFFSETS, axis=-1)
    mk, mv = memory_kv(mem_n, w_mem_kv)
    ya = dilated_mixture_attention(rope(split_heads(qa, A_HEADS), pos), rope(split_heads(ka, A_HEADS), pos),
                                   split_heads(va, A_HEADS))
    yb = moba_attention(rope(split_heads(qb, B_HEADS), pos), rope(split_heads(kb, B_HEADS), pos),
                        split_heads(vb, B_HEADS))
    ym = memory_attention(split_heads(qm, MEM_HEADS), mk, mv)
    y = jnp.concatenate([merge_heads(ya), merge_heads(yb), merge_heads(ym)], axis=-1) * jax.nn.silu(gate)
    return y @ w_out


def odd_mixer(h, pos, mem_n, w_in, w_mem_kv, w_out, sinks):
    qc, kc, vc, qm, gate = jnp.split(h @ w_in, ODD_OFFSETS, axis=-1)
    mk, mv = memory_kv(mem_n, w_mem_kv)
    yc = swa_sink_attention(rope(split_heads(qc, C_HEADS), pos), rope(split_heads(kc, C_KV_HEADS), pos),
                            split_heads(vc, C_KV_HEADS), sinks)
    ym = memory_attention(split_heads(qm, MEM_HEADS), mk, mv)
    y = jnp.concatenate([merge_heads(yc), merge_heads(ym)], axis=-1) * jax.nn.silu(gate)
    return y @ w_out


def setup_inputs(seed: int = 0) -> dict:
    key = jax.random.key(seed)
    ks = jax.random.split(key, 16)
    n_even = (DEPTH + 1) // 2
    n_odd = DEPTH // 2
    f32 = jnp.float32
    x = jax.random.normal(ks[0], (BATCH, SEQ, D_MODEL), f32)
    mem = jax.random.normal(ks[1], (BATCH, MEM_LEN, D_MODEL), f32)
    offset = jax.random.randint(ks[2], (BATCH, 1), 0, 4096, dtype=jnp.int32)
    positions = (jnp.arange(SEQ, dtype=jnp.int32)[None, :] + offset).astype(jnp.int32)
    din = D_MODEL ** -0.5
    return {
        'x': x,
        'mem': mem,
        'positions': positions,
        'even_norm': 1.0 + 0.02 * jax.random.normal(ks[3], (n_even, D_MODEL), f32),
        'even_w_in': jax.random.normal(ks[4], (n_even, D_MODEL, EVEN_IN), f32) * din,
        'even_w_mem_kv': jax.random.normal(ks[5], (n_even, D_MODEL, 2 * MEM_HEADS * HEAD_DIM), f32) * din,
        'even_w_out': jax.random.normal(ks[6], (n_even, EVEN_WIDTH, D_MODEL), f32) * EVEN_WIDTH ** -0.5,
        'odd_norm': 1.0 + 0.02 * jax.random.normal(ks[7], (n_odd, D_MODEL), f32),
        'odd_w_in': jax.random.normal(ks[8], (n_odd, D_MODEL, ODD_IN), f32) * din,
        'odd_w_mem_kv': jax.random.normal(ks[9], (n_odd, D_MODEL, 2 * MEM_HEADS * HEAD_DIM), f32) * din,
        'odd_w_out': jax.random.normal(ks[10], (n_odd, ODD_WIDTH, D_MODEL), f32) * ODD_WIDTH ** -0.5,
        'odd_sinks': 0.5 * jax.random.normal(ks[11], (n_odd, C_HEADS), f32),
        'mem_norm': 1.0 + 0.02 * jax.random.normal(ks[12], (D_MODEL,), f32),
        'final_norm': 1.0 + 0.02 * jax.random.normal(ks[13], (D_MODEL,), f32),
    }


def reference(x, mem, positions, even_norm, even_w_in, even_w_mem_kv, even_w_out,
              odd_norm, odd_w_in, odd_w_mem_kv, odd_w_out, odd_sinks, mem_norm, final_norm):
    mem_n = rmsnorm(mem, mem_norm)
    for layer in range(DEPTH):
        i = layer // 2
        if layer % 2 == 0:
            x = x + even_mixer(rmsnorm(x, even_norm[i]), positions, mem_n,
                               even_w_in[i], even_w_mem_kv[i], even_w_out[i])
        else:
            x = x + odd_mixer(rmsnorm(x, odd_norm[i]), positions, mem_n,
                              odd_w_in[i], odd_w_mem_kv[i], odd_w_out[i], odd_sinks[i])
    return rmsnorm(x, final_norm)
```

```python
import functools
import math

import jax
import jax.numpy as jnp
from jax import lax
from jax.experimental import pallas as pl
from jax.experimental.pallas import tpu as pltpu

F32 = jnp.float32
BF16 = jnp.bfloat16

LANES = 128
HEAD_DIM = 128
MEM_LEN = 256
MEM_HEADS = 4
A_HEADS = 6
B_HEADS = 6
MOBA_BLOCK = 256
MOBA_TOPK = 3
C_HEADS = 24
C_KV_HEADS = 3
C_HEAD_DIM = 64
C_WINDOW = 128
ROPE_THETA = 10000.0
EPS = 1e-6
NEG = -1e30

EVEN_WIDTH = (A_HEADS + B_HEADS + MEM_HEADS) * HEAD_DIM
ODD_WIDTH = C_HEADS * C_HEAD_DIM + MEM_HEADS * HEAD_DIM
EVEN_IN = 3 * A_HEADS * HEAD_DIM + 3 * B_HEADS * HEAD_DIM + MEM_HEADS * HEAD_DIM + EVEN_WIDTH
ODD_IN = C_HEADS * C_HEAD_DIM + 2 * C_KV_HEADS * C_HEAD_DIM + MEM_HEADS * HEAD_DIM + ODD_WIDTH

E_QA, E_KA, E_VA = 0, A_HEADS, 2 * A_HEADS
E_QB, E_KB, E_VB = 3 * A_HEADS, 3 * A_HEADS + B_HEADS, 3 * A_HEADS + 2 * B_HEADS
E_QM = 3 * A_HEADS + 3 * B_HEADS
E_GATE = E_QM + MEM_HEADS
O_Q = 0
O_KV = (C_HEADS * C_HEAD_DIM) // LANES
O_QM = O_KV + (2 * C_KV_HEADS * C_HEAD_DIM) // LANES
O_GATE = O_QM + MEM_HEADS

VMEM_LIMIT = 56 * 1024 * 1024


def _params(sem):
    return pltpu.CompilerParams(dimension_semantics=sem, vmem_limit_bytes=VMEM_LIMIT)


def _silu(g):
    return g / (1.0 + jnp.exp(-g))


def _rope_table_kernel(pos_ref, invf_ref, sign_ref, cos_ref, sin_ref):
    ang = pos_ref[...].astype(F32) * invf_ref[...]
    cos_ref[...] = jnp.cos(ang)
    sin_ref[...] = jnp.sin(ang) * sign_ref[...]


def _rope_tables(pos_col, head_dim):
    t = pos_col.shape[0]
    half = head_dim // 2
    inv_freq = jnp.exp(jnp.arange(half, dtype=F32) * (-2.0 * math.log(ROPE_THETA) / head_dim))
    lane = jnp.arange(LANES)
    invf = inv_freq[lane % half][None, :]
    sign = jnp.where((lane % head_dim) < half, -1.0, 1.0).astype(F32)[None, :]
    tm = min(t, 1024)
    return pl.pallas_call(
        _rope_table_kernel,
        grid=(t // tm,),
        in_specs=[pl.BlockSpec((tm, 1), lambda i: (i, 0)),
                  pl.BlockSpec((1, LANES), lambda i: (0, 0)),
                  pl.BlockSpec((1, LANES), lambda i: (0, 0))],
        out_specs=[pl.BlockSpec((tm, LANES), lambda i: (i, 0)),
                   pl.BlockSpec((tm, LANES), lambda i: (i, 0))],
        out_shape=[jax.ShapeDtypeStruct((t, LANES), F32), jax.ShapeDtypeStruct((t, LANES), F32)],
        compiler_params=_params(("arbitrary",)),
        name=f"rope_tables_{head_dim}",
    )(pos_col, invf, sign)


def _rmsnorm_kernel(x_ref, g_ref, o_ref):
    x = x_ref[...]
    ms = jnp.mean(x * x, axis=-1, keepdims=True)
    o_ref[...] = ((x * lax.rsqrt(ms + EPS)) * g_ref[...]).astype(o_ref.dtype)


def _rmsnorm(x2d, g):
    t, d = x2d.shape
    tm = min(t, 512)
    return pl.pallas_call(
        _rmsnorm_kernel,
        grid=(t // tm,),
        in_specs=[pl.BlockSpec((tm, d), lambda i: (i, 0)),
                  pl.BlockSpec((1, d), lambda i: (0, 0))],
        out_specs=pl.BlockSpec((tm, d), lambda i: (i, 0)),
        out_shape=jax.ShapeDtypeStruct((t, d), BF16),
        compiler_params=_params(("arbitrary",)),
        name="rmsnorm",
    )(x2d, g.reshape(1, d))


ROPE_NONE, ROPE_FULL, ROPE_FIRST_HALF = 0, 1, 2


def _rope_partner(x, head_dim):
    if head_dim == LANES:
        return pltpu.roll(x, LANES // 2, 1)
    lane = lax.broadcasted_iota(jnp.int32, x.shape, 1)
    half = head_dim // 2
    return jnp.where((lane % head_dim) < half, pltpu.roll(x, LANES - half, 1), pltpu.roll(x, half, 1))


def _proj_kernel(types_ref, h_ref, w_ref, cos_ref, sin_ref, o_ref, wbf_ref, *, tn, head_dim):
    j = pl.program_id(0)
    i = pl.program_id(1)

    @pl.when(i == 0)
    def _():
        wbf_ref[...] = w_ref[...].astype(BF16)

    acc = jnp.dot(h_ref[...], wbf_ref[...], preferred_element_type=F32)
    nchunk = tn // LANES
    for c in range(nchunk):
        kind = types_ref[j * nchunk + c]
        x = acc[:, c * LANES:(c + 1) * LANES]
        cols = slice(c * LANES, (c + 1) * LANES)

        @pl.when(kind == ROPE_NONE)
        def _():
            o_ref[:, cols] = x.astype(o_ref.dtype)

        @pl.when(kind != ROPE_NONE)
        def _():
            roped = x * cos_ref[...] + _rope_partner(x, head_dim) * sin_ref[...]
            lane = lax.broadcasted_iota(jnp.int32, x.shape, 1)
            plain = jnp.logical_and(kind == ROPE_FIRST_HALF, lane >= LANES // 2)
            o_ref[:, cols] = jnp.where(plain, x, roped).astype(o_ref.dtype)


def _proj_plain_kernel(h_ref, w_ref, o_ref, wbf_ref):
    @pl.when(pl.program_id(1) == 0)
    def _():
        wbf_ref[...] = w_ref[...].astype(BF16)

    o_ref[...] = jnp.dot(h_ref[...], wbf_ref[...], preferred_element_type=F32).astype(o_ref.dtype)


def _proj(h, w, tn, rope=None):
    t, d = h.shape
    n = w.shape[1]
    assert n % tn == 0 and tn % LANES == 0
    tm = min(t, 1024)
    grid = (n // tn, t // tm)
    out_shape = jax.ShapeDtypeStruct((t, n), BF16)
    scratch = [pltpu.VMEM((d, tn), BF16)]
    if rope is None:
        return pl.pallas_call(
            _proj_plain_kernel,
            grid=grid,
            in_specs=[pl.BlockSpec((tm, d), lambda j, i: (i, 0)),
                      pl.BlockSpec((d, tn), lambda j, i: (0, j))],
            out_specs=pl.BlockSpec((tm, tn), lambda j, i: (i, j)),
            out_shape=out_shape,
            scratch_shapes=scratch,
            compiler_params=_params(("arbitrary", "arbitrary")),
            name="proj_plain",
        )(h, w)
    kinds, cos, sin, head_dim = rope
    return pl.pallas_call(
        functools.partial(_proj_kernel, tn=tn, head_dim=head_dim),
        grid_spec=pltpu.PrefetchScalarGridSpec(
            num_scalar_prefetch=1,
            grid=grid,
            in_specs=[pl.BlockSpec((tm, d), lambda j, i, k: (i, 0)),
                      pl.BlockSpec((d, tn), lambda j, i, k: (0, j)),
                      pl.BlockSpec((tm, LANES), lambda j, i, k: (i, 0)),
                      pl.BlockSpec((tm, LANES), lambda j, i, k: (i, 0))],
            out_specs=pl.BlockSpec((tm, tn), lambda j, i, k: (i, j)),
            scratch_shapes=scratch),
        out_shape=out_shape,
        compiler_params=_params(("arbitrary", "arbitrary")),
        name=f"proj_rope{head_dim}",
    )(kinds, h, w, cos, sin)


def _even_chunk_kinds():
    kinds = [ROPE_NONE] * (EVEN_IN // LANES)
    for start in (E_QA, E_KA, E_QB, E_KB):
        for c in range(start, start + A_HEADS):
            kinds[c] = ROPE_FULL
    return jnp.asarray(kinds, jnp.int32)


def _odd_chunk_kinds():
    kinds = [ROPE_NONE] * (ODD_IN // LANES)
    k_end = C_HEADS * C_HEAD_DIM + C_KV_HEADS * C_HEAD_DIM
    for c in range(len(kinds)):
        if (c + 1) * LANES <= k_end:
            kinds[c] = ROPE_FULL
        elif c * LANES < k_end:
            assert k_end - c * LANES == LANES // 2
            kinds[c] = ROPE_FIRST_HALF
    return jnp.asarray(kinds, jnp.int32)


def _scores(q, k, scale):
    return lax.dot_general(q, k, (((1,), (1,)), ((), ())), preferred_element_type=F32) * scale


def _online_update(m, l, acc, s_masked, weight, v):
    m_new = jnp.maximum(m, jnp.max(s_masked, axis=1, keepdims=True))
    p = jnp.exp(s_masked - m_new)
    if weight is not None:
        p = p * weight
    alpha = jnp.exp(m - m_new)
    l_new = alpha * l + jnp.sum(p, axis=1, keepdims=True)
    acc_new = alpha * acc + jnp.dot(p.astype(BF16), v, preferred_element_type=F32)
    return m_new, l_new, acc_new


def _dilated_kernel(q_ref, k_ref, v_ref, g_ref, o_ref, *, blk, scale):
    i = pl.program_id(2)
    q = q_ref[...]
    row = lax.broadcasted_iota(jnp.int32, (blk, blk), 0)
    col = lax.broadcasted_iota(jnp.int32, (blk, blk), 1)
    rc = row - col

    def body(n, carry):
        m, l, acc = carry
        start = pl.multiple_of(n * blk, blk)
        k = k_ref[pl.ds(start, blk), :]
        v = v_ref[pl.ds(start, blk), :]
        s = _scores(q, k, scale)
        dist = rc + (i - n) * blk
        cnt = (jnp.where(dist <= 128, 1.0, 0.0)
               + jnp.where(jnp.logical_and((dist & 3) == 0, dist <= 512), 1.0, 0.0)
               + jnp.where((dist & 15) == 0, 1.0, 0.0))
        cnt = jnp.where(dist >= 0, cnt, 0.0)
        s = jnp.where(cnt > 0.0, s, NEG)
        return _online_update(m, l, acc, s, cnt, v)

    m0 = jnp.full((blk, 1), NEG, F32)
    l0 = jnp.zeros((blk, 1), F32)
    a0 = jnp.zeros((blk, HEAD_DIM), F32)
    m, l, acc = lax.fori_loop(0, i + 1, body, (m0, l0, a0))
    g = g_ref[...].astype(F32)
    o_ref[...] = ((acc / l) * _silu(g)).astype(o_ref.dtype)


def _dilated_attention(p3):
    b, s, _ = p3.shape
    blk = 256
    return pl.pallas_call(
        functools.partial(_dilated_kernel, blk=blk, scale=HEAD_DIM ** -0.5),
        grid=(b, A_HEADS, s // blk),
        in_specs=[pl.BlockSpec((None, blk, LANES), lambda bi, h, i: (bi, i, E_QA + h)),
                  pl.BlockSpec((None, s, LANES), lambda bi, h, i: (bi, 0, E_KA + h)),
                  pl.BlockSpec((None, s, LANES), lambda bi, h, i: (bi, 0, E_VA + h)),
                  pl.BlockSpec((None, blk, LANES), lambda bi, h, i: (bi, i, E_GATE + h))],
        out_specs=pl.BlockSpec((None, blk, LANES), lambda bi, h, i: (bi, i, h)),
        out_shape=jax.ShapeDtypeStruct((b, s, A_HEADS * HEAD_DIM), BF16),
        compiler_params=_params(("arbitrary", "arbitrary", "arbitrary")),
        name="dilated_attention",
    )(p3, p3, p3, p3)


def _moba_kernel(q_ref, k_ref, v_ref, g_ref, o_ref, kmean_ref, m_ref, l_ref, acc_ref, *, nblk, blk, scale):
    i = pl.program_id(2)

    @pl.when(i == 0)
    def _():
        for n in range(nblk):
            kb = k_ref[n * blk:(n + 1) * blk, :].astype(F32)
            kmean_ref[n:n + 1, :] = jnp.sum(kb, axis=0, keepdims=True) * (1.0 / blk)

    q = q_ref[...]
    km = kmean_ref[...]
    hi = km.astype(BF16).astype(F32)
    lo = km - hi
    ksplit = jnp.concatenate([hi, lo], axis=0).astype(BF16)
    gt2 = lax.dot_general(ksplit, q, (((1,), (1,)), ((), ())), preferred_element_type=F32)
    gate = gt2[:nblk] + gt2[nblk:]

    nid = lax.broadcasted_iota(jnp.int32, (nblk, blk), 0)
    rank = jnp.zeros((nblk, blk), jnp.int32)
    for mm in range(nblk):
        gm = gate[mm:mm + 1, :]
        beats = jnp.logical_or(gm > gate, jnp.logical_and(gm == gate, mm < nid))
        rank = rank + jnp.where(jnp.logical_and(beats, mm < i), 1, 0)
    sel_t = jnp.where(jnp.logical_and(rank < MOBA_TOPK, nid < i), 1.0, 0.0)
    sel = jnp.concatenate([sel_t, jnp.zeros((LANES - nblk, blk), F32)], axis=0).T

    m_ref[...] = jnp.full(m_ref.shape, NEG, F32)
    l_ref[...] = jnp.zeros(l_ref.shape, F32)
    acc_ref[...] = jnp.zeros(acc_ref.shape, F32)
    row = lax.broadcasted_iota(jnp.int32, (blk, blk), 0)
    col = lax.broadcasted_iota(jnp.int32, (blk, blk), 1)

    for n in range(nblk):
        @pl.when(n <= i)
        def _():
            k = k_ref[n * blk:(n + 1) * blk, :]
            v = v_ref[n * blk:(n + 1) * blk, :]
            s = _scores(q, k, scale)
            okf = jnp.where(n == i, jnp.where(row >= col, 1.0, 0.0), sel[:, n:n + 1])
            s = jnp.where(okf > 0.5, s, NEG)
            m, l, acc = _online_update(m_ref[...], l_ref[...], acc_ref[...], s, okf, v)
            m_ref[...] = m
            l_ref[...] = l
            acc_ref[...] = acc

    g = g_ref[...].astype(F32)
    o_ref[...] = ((acc_ref[...] / l_ref[...]) * _silu(g)).astype(o_ref.dtype)


def _moba_attention(p3):
    b, s, _ = p3.shape
    blk = MOBA_BLOCK
    nblk = s // blk
    assert nblk >= MOBA_TOPK and nblk <= 8
    return pl.pallas_call(
        functools.partial(_moba_kernel, nblk=nblk, blk=blk, scale=HEAD_DIM ** -0.5),
        grid=(b, B_HEADS, nblk),
        in_specs=[pl.BlockSpec((None, blk, LANES), lambda bi, h, i: (bi, i, E_QB + h)),
                  pl.BlockSpec((None, s, LANES), lambda bi, h, i: (bi, 0, E_KB + h)),
                  pl.BlockSpec((None, s, LANES), lambda bi, h, i: (bi, 0, E_VB + h)),
                  pl.BlockSpec((None, blk, LANES), lambda bi, h, i: (bi, i, E_GATE + A_HEADS + h))],
        out_specs=pl.BlockSpec((None, blk, LANES), lambda bi, h, i: (bi, i, h)),
        out_shape=jax.ShapeDtypeStruct((b, s, B_HEADS * HEAD_DIM), BF16),
        scratch_shapes=[pltpu.VMEM((nblk, HEAD_DIM), F32),
                        pltpu.VMEM((blk, 1), F32),
                        pltpu.VMEM((blk, 1), F32),
                        pltpu.VMEM((blk, HEAD_DIM), F32)],
        compiler_params=_params(("arbitrary", "arbitrary", "arbitrary")),
        name="moba_attention",
    )(p3, p3, p3, p3)


def _mem_kernel(q_ref, mk_ref, mv_ref, g_ref, o_ref, *, scale):
    s = _scores(q_ref[...], mk_ref[...], scale)
    m = jnp.max(s, axis=1, keepdims=True)
    p = jnp.exp(s - m)
    l = jnp.sum(p, axis=1, keepdims=True)
    o = jnp.dot(p.astype(BF16), mv_ref[...], preferred_element_type=F32) / l
    o_ref[...] = (o * _silu(g_ref[...].astype(F32))).astype(o_ref.dtype)


def _mem_attention(p3, memkv, q_block, gate_block):
    b, s, _ = p3.shape
    tq = min(s, 1024)
    return pl.pallas_call(
        functools.partial(_mem_kernel, scale=HEAD_DIM ** -0.5),
        grid=(b, MEM_HEADS, s // tq),
        in_specs=[pl.BlockSpec((None, tq, LANES), lambda bi, h, i: (bi, i, q_block + h)),
                  pl.BlockSpec((None, MEM_LEN, LANES), lambda bi, h, i: (bi, 0, h)),
                  pl.BlockSpec((None, MEM_LEN, LANES), lambda bi, h, i: (bi, 0, MEM_HEADS + h)),
                  pl.BlockSpec((None, tq, LANES), lambda bi, h, i: (bi, i, gate_block + h))],
        out_specs=pl.BlockSpec((None, tq, LANES), lambda bi, h, i: (bi, i, h)),
        out_shape=jax.ShapeDtypeStruct((b, s, MEM_HEADS * HEAD_DIM), BF16),
        compiler_params=_params(("arbitrary", "arbitrary", "arbitrary")),
        name="memory_attention",
    )(p3, memkv, memkv, p3)


SWA_QBLK = 256
SWA_KSPAN = SWA_QBLK + C_WINDOW


def _swa_kernel(sinks_ref, q_ref, kv_ref, g0_ref, g1_ref, g2_ref, g3_ref, o_ref, *, tq, scale):
    i = pl.program_id(1)
    grp = pl.program_id(2)
    heads_per_group = C_HEADS // C_KV_HEADS
    gate = jnp.concatenate([r[...] for r in (g0_ref, g1_ref, g2_ref, g3_ref)], axis=1).astype(F32)
    qi = lax.broadcasted_iota(jnp.int32, (SWA_QBLK, SWA_KSPAN), 0)
    ki = lax.broadcasted_iota(jnp.int32, (SWA_QBLK, SWA_KSPAN), 1)

    for gs in range(C_KV_HEADS):
        @pl.when(grp == gs)
        def _():
            for sub in range(tq // SWA_QBLK):
                r0 = i * tq + sub * SWA_QBLK
                kstart = pl.multiple_of(jnp.maximum(r0 - C_WINDOW, 0), C_WINDOW)
                kv = kv_ref[pl.ds(kstart, SWA_KSPAN), :]
                k = kv[:, gs * C_HEAD_DIM:(gs + 1) * C_HEAD_DIM]
                v = kv[:, (C_KV_HEADS + gs) * C_HEAD_DIM:(C_KV_HEADS + gs + 1) * C_HEAD_DIM]
                dist = (r0 + qi) - (kstart + ki)
                ok = jnp.logical_and(dist >= 0, dist <= C_WINDOW - 1)
                outs = []
                for h in range(heads_per_group):
                    qh = q_ref[sub * SWA_QBLK:(sub + 1) * SWA_QBLK, h * C_HEAD_DIM:(h + 1) * C_HEAD_DIM]
                    s = jnp.where(ok, _scores(qh, k, scale), NEG)
                    sink = sinks_ref[gs * heads_per_group + h]
                    m = jnp.maximum(jnp.max(s, axis=1, keepdims=True), sink)
                    p = jnp.exp(s - m)
                    l = jnp.sum(p, axis=1, keepdims=True) + jnp.exp(sink - m)
                    outs.append(jnp.dot(p.astype(BF16), v, preferred_element_type=F32) / l)
                rows = slice(sub * SWA_QBLK, (sub + 1) * SWA_QBLK)
                o = jnp.concatenate(outs, axis=1) * _silu(gate[rows, :])
                o_ref[rows, :] = o.astype(o_ref.dtype)


def _swa_attention(p3, sinks):
    b, s, _ = p3.shape
    tq = min(s, 512)
    gw = (C_HEADS // C_KV_HEADS) * C_HEAD_DIM
    gblocks = gw // LANES
    kvw = 2 * C_KV_HEADS * C_HEAD_DIM
    assert (O_KV * LANES) % kvw == 0
    gate_specs = [pl.BlockSpec((None, tq, LANES),
                               functools.partial(lambda bi, i, g, sk, u: (bi, i, O_GATE + gblocks * g + u), u=u))
                  for u in range(gblocks)]
    return pl.pallas_call(
        functools.partial(_swa_kernel, tq=tq, scale=C_HEAD_DIM ** -0.5),
        grid_spec=pltpu.PrefetchScalarGridSpec(
            num_scalar_prefetch=1,
            grid=(b, s // tq, C_KV_HEADS),
            in_specs=[pl.BlockSpec((None, tq, gw), lambda bi, i, g, sk: (bi, i, g)),
                      pl.BlockSpec((None, s, kvw), lambda bi, i, g, sk: (bi, 0, (O_KV * LANES) // kvw))]
                     + gate_specs,
            out_specs=pl.BlockSpec((None, tq, gw), lambda bi, i, g, sk: (bi, i, g))),
        out_shape=jax.ShapeDtypeStruct((b, s, C_HEADS * C_HEAD_DIM), BF16),
        compiler_params=_params(("arbitrary", "arbitrary", "arbitrary")),
        name="swa_sink_attention",
    )(sinks, p3, p3, p3, p3, p3, p3)


def _outproj_kernel(*refs, widths, tn, nj, emit_x):
    ny = len(widths)
    y_refs = refs[:ny]
    w_ref, x_ref, g_ref = refs[ny:ny + 3]
    outs = refs[ny + 3:]
    if emit_x:
        xo_ref, ho_ref, wbf_ref, xrow_ref = outs
    else:
        ho_ref, wbf_ref, xrow_ref = outs
    i = pl.program_id(0)
    j = pl.program_id(1)

    @pl.when(i == 0)
    def _():
        wbf_ref[j] = w_ref[...].astype(BF16)

    acc = x_ref[...]
    off = 0
    for y_ref, wd in zip(y_refs, widths):
        acc = acc + jnp.dot(y_ref[...], wbf_ref[j, off:off + wd, :], preferred_element_type=F32)
        off += wd
    if emit_x:
        xo_ref[...] = acc
    xrow_ref[j] = acc

    @pl.when(j == nj - 1)
    def _():
        ss = jnp.zeros((acc.shape[0], 1), F32)
        for c in range(nj):
            xc = xrow_ref[c]
            ss = ss + jnp.sum(xc * xc, axis=1, keepdims=True)
        r = lax.rsqrt(ss * (1.0 / (nj * tn)) + EPS)
        for c in range(nj):
            ho_ref[:, c * tn:(c + 1) * tn] = ((xrow_ref[c] * r) * g_ref[:, c * tn:(c + 1) * tn]).astype(ho_ref.dtype)


def _outproj(ys, w, x2d, g_next, final):
    t, d = x2d.shape
    widths = tuple(y.shape[1] for y in ys)
    assert sum(widths) == w.shape[0] and w.shape[1] == d
    tm = min(t, 512)
    tn = min(d, 512)
    nj = d // tn
    kdim = w.shape[0]
    in_specs = [pl.BlockSpec((tm, wd), lambda i, j: (i, 0)) for wd in widths]
    in_specs += [pl.BlockSpec((kdim, tn), lambda i, j: (0, jnp.where(i == 0, j, nj - 1))),
                 pl.BlockSpec((tm, tn), lambda i, j: (i, j)),
                 pl.BlockSpec((1, d), lambda i, j: (0, 0))]
    h_spec = pl.BlockSpec((tm, d), lambda i, j: (i, 0))
    if final:
        out_specs = [h_spec]
        out_shape = [jax.ShapeDtypeStruct((t, d), F32)]
    else:
        out_specs = [pl.BlockSpec((tm, tn), lambda i, j: (i, j)), h_spec]
        out_shape = [jax.ShapeDtypeStruct((t, d), F32), jax.ShapeDtypeStruct((t, d), BF16)]
    res = pl.pallas_call(
        functools.partial(_outproj_kernel, widths=widths, tn=tn, nj=nj, emit_x=not final),
        grid=(t // tm, nj),
        in_specs=in_specs,
        out_specs=out_specs,
        out_shape=out_shape,
        scratch_shapes=[pltpu.VMEM((nj, kdim, tn), BF16), pltpu.VMEM((nj, tm, tn), F32)],
        compiler_params=_params(("arbitrary", "arbitrary")),
        name="outproj_final" if final else "outproj",
    )(*ys, w, x2d, g_next.reshape(1, d))
    return res[0] if final else (res[0], res[1])


def kernel(x, mem, positions, even_norm, even_w_in, even_w_mem_kv, even_w_out, odd_norm, odd_w_in,
           odd_w_mem_kv, odd_w_out, odd_sinks, mem_norm, final_norm):
    b, s, d = x.shape
    t = b * s
    depth = even_norm.shape[0] + odd_norm.shape[0]
    pos_col = positions.reshape(t, 1)
    cos_e, sin_e = _rope_tables(pos_col, HEAD_DIM)
    cos_o, sin_o = _rope_tables(pos_col, C_HEAD_DIM)
    kinds_e = _even_chunk_kinds()
    kinds_o = _odd_chunk_kinds()
    mem_n = _rmsnorm(mem.reshape(b * MEM_LEN, d), mem_norm)

    x2d = x.reshape(t, d)
    h = _rmsnorm(x2d, even_norm[0])
    out = None
    for layer in range(depth):
        idx = layer // 2
        last = layer == depth - 1
        if last:
            g_next = final_norm
        elif layer % 2 == 0:
            g_next = odd_norm[idx]
        else:
            g_next = even_norm[idx + 1]
        if layer % 2 == 0:
            memkv = _proj(mem_n, even_w_mem_kv[idx], 512).reshape(b, MEM_LEN, 2 * MEM_HEADS * HEAD_DIM)
            p3 = _proj(h, even_w_in[idx], 512, (kinds_e, cos_e, sin_e, HEAD_DIM)).reshape(b, s, EVEN_IN)
            ys = [_dilated_attention(p3), _moba_attention(p3), _mem_attention(p3, memkv, E_QM, E_GATE + 2 * A_HEADS)]
            w_out = even_w_out[idx]
        else:
            memkv = _proj(mem_n, odd_w_mem_kv[idx], 512).reshape(b, MEM_LEN, 2 * MEM_HEADS * HEAD_DIM)
            p3 = _proj(h, odd_w_in[idx], 896, (kinds_o, cos_o, sin_o, C_HEAD_DIM)).reshape(b, s, ODD_IN)
            ys = [_swa_attention(p3, odd_sinks[idx]),
                  _mem_attention(p3, memkv, O_QM, O_GATE + (C_HEADS * C_HEAD_DIM) // LANES)]
            w_out = odd_w_out[idx]
        ys = [y.reshape(t, y.shape[-1]) for y in ys]
        if last:
            out = _outproj(ys, w_out, x2d, g_next, final=True)
        else:
            x2d, h = _outproj(ys, w_out, x2d, g_next, final=False)
    return out.reshape(b, s, d)
```

```python
import functools
import math

import jax
import jax.numpy as jnp
from jax import lax
from jax.experimental import pallas as pl
from jax.experimental.pallas import tpu as pltpu

F32 = jnp.float32
BF16 = jnp.bfloat16

LANES = 128
HEAD_DIM = 128
MEM_LEN = 256
MEM_HEADS = 4
A_HEADS = 6
B_HEADS = 6
MOBA_BLOCK = 256
MOBA_TOPK = 3
C_HEADS = 24
C_KV_HEADS = 3
C_HEAD_DIM = 64
C_WINDOW = 128
ROPE_THETA = 10000.0
EPS = 1e-6
NEG = -1e30
LOG2E = 1.4426950408889634

EVEN_WIDTH = (A_HEADS + B_HEADS + MEM_HEADS) * HEAD_DIM
ODD_WIDTH = C_HEADS * C_HEAD_DIM + MEM_HEADS * HEAD_DIM
EVEN_IN = 3 * A_HEADS * HEAD_DIM + 3 * B_HEADS * HEAD_DIM + MEM_HEADS * HEAD_DIM + EVEN_WIDTH
ODD_IN = C_HEADS * C_HEAD_DIM + 2 * C_KV_HEADS * C_HEAD_DIM + MEM_HEADS * HEAD_DIM + ODD_WIDTH

E_QA, E_KA, E_VA = 0, A_HEADS, 2 * A_HEADS
E_QB, E_KB, E_VB = 3 * A_HEADS, 3 * A_HEADS + B_HEADS, 3 * A_HEADS + 2 * B_HEADS
E_QM = 3 * A_HEADS + 3 * B_HEADS
E_GATE = E_QM + MEM_HEADS
O_Q = 0
O_KV = (C_HEADS * C_HEAD_DIM) // LANES
O_QM = O_KV + (2 * C_KV_HEADS * C_HEAD_DIM) // LANES
O_GATE = O_QM + MEM_HEADS

VMEM_LIMIT = 56 * 1024 * 1024


def _params(sem):
    return pltpu.CompilerParams(dimension_semantics=sem, vmem_limit_bytes=VMEM_LIMIT)


def _silu(g):
    return g / (1.0 + jnp.exp(-g))


def _rope_table_kernel(pos_ref, invf_ref, sign_ref, cos_ref, sin_ref):
    ang = pos_ref[...].astype(F32) * invf_ref[...]
    cos_ref[...] = jnp.cos(ang)
    sin_ref[...] = jnp.sin(ang) * sign_ref[...]


def _rope_tables(pos_col, head_dim):
    t = pos_col.shape[0]
    half = head_dim // 2
    inv_freq = jnp.exp(jnp.arange(half, dtype=F32) * (-2.0 * math.log(ROPE_THETA) / head_dim))
    lane = jnp.arange(LANES)
    invf = inv_freq[lane % half][None, :]
    sign = jnp.where((lane % head_dim) < half, -1.0, 1.0).astype(F32)[None, :]
    tm = min(t, 1024)
    return pl.pallas_call(
        _rope_table_kernel,
        grid=(t // tm,),
        in_specs=[pl.BlockSpec((tm, 1), lambda i: (i, 0)),
                  pl.BlockSpec((1, LANES), lambda i: (0, 0)),
                  pl.BlockSpec((1, LANES), lambda i: (0, 0))],
        out_specs=[pl.BlockSpec((tm, LANES), lambda i: (i, 0)),
                   pl.BlockSpec((tm, LANES), lambda i: (i, 0))],
        out_shape=[jax.ShapeDtypeStruct((t, LANES), F32), jax.ShapeDtypeStruct((t, LANES), F32)],
        compiler_params=_params(("arbitrary",)),
        name=f"rope_tables_{head_dim}",
    )(pos_col, invf, sign)


def _rmsnorm_kernel(x_ref, g_ref, o_ref):
    x = x_ref[...]
    ms = jnp.mean(x * x, axis=-1, keepdims=True)
    o_ref[...] = ((x * lax.rsqrt(ms + EPS)) * g_ref[...]).astype(o_ref.dtype)


def _rmsnorm(x2d, g):
    t, d = x2d.shape
    tm = min(t, 512)
    return pl.pallas_call(
        _rmsnorm_kernel,
        grid=(t // tm,),
        in_specs=[pl.BlockSpec((tm, d), lambda i: (i, 0)),
                  pl.BlockSpec((1, d), lambda i: (0, 0))],
        out_specs=pl.BlockSpec((tm, d), lambda i: (i, 0)),
        out_shape=jax.ShapeDtypeStruct((t, d), BF16),
        compiler_params=_params(("arbitrary",)),
        name="rmsnorm",
    )(x2d, g.reshape(1, d))


ROPE_NONE, ROPE_FULL, ROPE_FIRST_HALF = 0, 1, 2


def _rope_partner(x, head_dim):
    if head_dim == LANES:
        return pltpu.roll(x, LANES // 2, 1)
    lane = lax.broadcasted_iota(jnp.int32, x.shape, 1)
    half = head_dim // 2
    return jnp.where((lane % head_dim) < half, pltpu.roll(x, LANES - half, 1), pltpu.roll(x, half, 1))


def _proj_kernel(types_ref, h_ref, w_ref, cos_ref, sin_ref, o_ref, wbf_ref, *, tn, head_dim):
    j = pl.program_id(0)
    i = pl.program_id(1)

    @pl.when(i == 0)
    def _():
        wbf_ref[...] = w_ref[...].astype(BF16)

    acc = jnp.dot(h_ref[...], wbf_ref[...], preferred_element_type=F32)
    nchunk = tn // LANES
    for c in range(nchunk):
        kind = types_ref[j * nchunk + c]
        x = acc[:, c * LANES:(c + 1) * LANES]
        cols = slice(c * LANES, (c + 1) * LANES)

        @pl.when(kind == ROPE_NONE)
        def _():
            o_ref[:, cols] = x.astype(o_ref.dtype)

        @pl.when(kind != ROPE_NONE)
        def _():
            roped = x * cos_ref[...] + _rope_partner(x, head_dim) * sin_ref[...]
            lane = lax.broadcasted_iota(jnp.int32, x.shape, 1)
            plain = jnp.logical_and(kind == ROPE_FIRST_HALF, lane >= LANES // 2)
            o_ref[:, cols] = jnp.where(plain, x, roped).astype(o_ref.dtype)


def _proj_plain_kernel(h_ref, w_ref, o_ref, wbf_ref):
    @pl.when(pl.program_id(1) == 0)
    def _():
        wbf_ref[...] = w_ref[...].astype(BF16)

    o_ref[...] = jnp.dot(h_ref[...], wbf_ref[...], preferred_element_type=F32).astype(o_ref.dtype)


def _proj(h, w_stack, idx, tn, rope=None):
    t, d = h.shape
    n = w_stack.shape[2]
    assert n % tn == 0 and tn % LANES == 0
    tm = min(t, 1024)
    grid = (n // tn, t // tm)
    out_shape = jax.ShapeDtypeStruct((t, n), BF16)
    scratch = [pltpu.VMEM((d, tn), BF16)]
    if rope is None:
        return pl.pallas_call(
            _proj_plain_kernel,
            grid=grid,
            in_specs=[pl.BlockSpec((tm, d), lambda j, i: (i, 0)),
                      pl.BlockSpec((None, d, tn), lambda j, i: (idx, 0, j))],
            out_specs=pl.BlockSpec((tm, tn), lambda j, i: (i, j)),
            out_shape=out_shape,
            scratch_shapes=scratch,
            compiler_params=_params(("arbitrary", "arbitrary")),
            name="proj_plain",
        )(h, w_stack)
    kinds, cos, sin, head_dim = rope
    return pl.pallas_call(
        functools.partial(_proj_kernel, tn=tn, head_dim=head_dim),
        grid_spec=pltpu.PrefetchScalarGridSpec(
            num_scalar_prefetch=1,
            grid=grid,
            in_specs=[pl.BlockSpec((tm, d), lambda j, i, k: (i, 0)),
                      pl.BlockSpec((None, d, tn), lambda j, i, k: (idx, 0, j)),
                      pl.BlockSpec((tm, LANES), lambda j, i, k: (i, 0)),
                      pl.BlockSpec((tm, LANES), lambda j, i, k: (i, 0))],
            out_specs=pl.BlockSpec((tm, tn), lambda j, i, k: (i, j)),
            scratch_shapes=scratch),
        out_shape=out_shape,
        compiler_params=_params(("arbitrary", "arbitrary")),
        name=f"proj_rope{head_dim}",
    )(kinds, h, w_stack, cos, sin)


def _even_chunk_kinds():
    kinds = [ROPE_NONE] * (EVEN_IN // LANES)
    for start in (E_QA, E_KA, E_QB, E_KB):
        for c in range(start, start + A_HEADS):
            kinds[c] = ROPE_FULL
    return jnp.asarray(kinds, jnp.int32)


def _odd_chunk_kinds():
    kinds = [ROPE_NONE] * (ODD_IN // LANES)
    k_end = C_HEADS * C_HEAD_DIM + C_KV_HEADS * C_HEAD_DIM
    for c in range(len(kinds)):
        if (c + 1) * LANES <= k_end:
            kinds[c] = ROPE_FULL
        elif c * LANES < k_end:
            assert k_end - c * LANES == LANES // 2
            kinds[c] = ROPE_FIRST_HALF
    return jnp.asarray(kinds, jnp.int32)


def _scores_t(k, q):
    return lax.dot_general(k, q, (((1,), (1,)), ((), ())), preferred_element_type=F32)


def _transpose_bf16(x):
    return x.astype(F32).T.astype(BF16)


def _softmax_pv_t(tiles, weights, vt, c, sink=None):
    top = tiles[0]
    for s in tiles[1:]:
        top = jnp.maximum(top, s)
    m = jnp.max(top, axis=0, keepdims=True)
    if sink is not None:
        m = jnp.maximum(m, sink)
    total = None
    ps = []
    for s, w in zip(tiles, weights):
        p = jnp.exp2((s - m) * c)
        if w is not None:
            p = p * w
        total = p if total is None else total + p
        ps.append(p.astype(BF16))
    l = jnp.sum(total, axis=0, keepdims=True)
    if sink is not None:
        l = l + jnp.exp2((sink - m) * c)
    p_all = ps[0] if len(ps) == 1 else jnp.concatenate(ps, axis=0)
    return jnp.dot(vt, p_all, preferred_element_type=F32) * (1.0 / l)


def _tile_delta(blk):
    row = lax.broadcasted_iota(jnp.int32, (blk, blk), 0)
    col = lax.broadcasted_iota(jnp.int32, (blk, blk), 1)
    return col - row


def _dilated_kernel(q_ref, k_ref, v_ref, g_ref, o_ref, *, blk, nblk, c):
    vt = _transpose_bf16(v_ref[...])
    cr = _tile_delta(blk)

    def near(delta):
        dist = cr + delta * blk
        cnt = (jnp.where(dist <= 128, 1.0, 0.0)
               + jnp.where(jnp.logical_and((dist & 3) == 0, dist <= 512), 1.0, 0.0)
               + jnp.where((dist & 15) == 0, 1.0, 0.0))
        cnt = jnp.where(dist >= 0, cnt, 0.0)
        return jnp.where(cnt > 0.0, 0.0, NEG), cnt

    n_near = -(-512 // blk) + 1
    nears = [near(d) for d in range(min(n_near, nblk))]
    far_bias = jnp.where((cr & 15) == 0, 0.0, NEG)

    for i in range(nblk):
        rows = slice(i * blk, (i + 1) * blk)
        ext = (i + 1) * blk
        st = _scores_t(k_ref[0:ext, :], q_ref[rows, :])
        tiles, weights = [], []
        for n in range(i + 1):
            s_n = st[n * blk:(n + 1) * blk]
            if i - n < len(nears):
                bias, cnt = nears[i - n]
                tiles.append(s_n + bias)
                weights.append(cnt)
            else:
                tiles.append(s_n + far_bias)
                weights.append(None)
        ot = _softmax_pv_t(tiles, weights, vt[:, 0:ext], c)
        o_ref[rows, :] = (ot.T * _silu(g_ref[rows, :].astype(F32))).astype(o_ref.dtype)


def _dilated_attention(p3):
    b, s, _ = p3.shape
    blk = 256
    assert blk * (-(-512 // blk)) >= 512 and s % blk == 0
    return pl.pallas_call(
        functools.partial(_dilated_kernel, blk=blk, nblk=s // blk, c=HEAD_DIM ** -0.5 * LOG2E),
        grid=(b, A_HEADS),
        in_specs=[pl.BlockSpec((None, s, LANES), lambda bi, h: (bi, 0, E_QA + h)),
                  pl.BlockSpec((None, s, LANES), lambda bi, h: (bi, 0, E_KA + h)),
                  pl.BlockSpec((None, s, LANES), lambda bi, h: (bi, 0, E_VA + h)),
                  pl.BlockSpec((None, s, LANES), lambda bi, h: (bi, 0, E_GATE + h))],
        out_specs=pl.BlockSpec((None, s, LANES), lambda bi, h: (bi, 0, h)),
        out_shape=jax.ShapeDtypeStruct((b, s, A_HEADS * HEAD_DIM), BF16),
        compiler_params=_params(("arbitrary", "arbitrary")),
        name="dilated_attention",
    )(p3, p3, p3, p3)


def _moba_kernel(q_ref, k_ref, v_ref, g_ref, o_ref, *, blk, nblk, c):
    s_len = nblk * blk
    q = q_ref[...]
    vt = _transpose_bf16(v_ref[...])

    km = jnp.concatenate(
        [jnp.sum(k_ref[n * blk:(n + 1) * blk, :].astype(F32), axis=0, keepdims=True) * (1.0 / blk)
         for n in range(nblk)], axis=0)
    hi = km.astype(BF16).astype(F32)
    ksplit = jnp.concatenate([hi, km - hi], axis=0).astype(BF16)
    gt2 = _scores_t(ksplit, q)
    gate = gt2[:nblk] + gt2[nblk:]

    nid = lax.broadcasted_iota(jnp.int32, (nblk, s_len), 0)
    own = jnp.right_shift(lax.broadcasted_iota(jnp.int32, (nblk, s_len), 1), blk.bit_length() - 1)
    rank = jnp.zeros((nblk, s_len), jnp.int32)
    for mm in range(nblk):
        gm = gate[mm:mm + 1, :]
        beats = jnp.logical_or(gm > gate, jnp.logical_and(gm == gate, mm < nid))
        rank = rank + jnp.where(jnp.logical_and(beats, mm < own), 1, 0)
    sel_bias = jnp.where(jnp.logical_and(rank < MOBA_TOPK, nid < own), 0.0, NEG)
    causal_bias = jnp.where(_tile_delta(blk) >= 0, 0.0, NEG)

    for i in range(nblk):
        rows = slice(i * blk, (i + 1) * blk)
        ext = (i + 1) * blk
        st = _scores_t(k_ref[0:ext, :], q[rows, :])
        tiles = [st[n * blk:(n + 1) * blk] + sel_bias[n:n + 1, rows] for n in range(i)]
        tiles.append(st[i * blk:ext] + causal_bias)
        ot = _softmax_pv_t(tiles, [None] * (i + 1), vt[:, 0:ext], c)
        o_ref[rows, :] = (ot.T * _silu(g_ref[rows, :].astype(F32))).astype(o_ref.dtype)


def _moba_attention(p3):
    b, s, _ = p3.shape
    blk = MOBA_BLOCK
    nblk = s // blk
    assert nblk >= MOBA_TOPK and blk & (blk - 1) == 0
    return pl.pallas_call(
        functools.partial(_moba_kernel, blk=blk, nblk=nblk, c=HEAD_DIM ** -0.5 * LOG2E),
        grid=(b, B_HEADS),
        in_specs=[pl.BlockSpec((None, s, LANES), lambda bi, h: (bi, 0, E_QB + h)),
                  pl.BlockSpec((None, s, LANES), lambda bi, h: (bi, 0, E_KB + h)),
                  pl.BlockSpec((None, s, LANES), lambda bi, h: (bi, 0, E_VB + h)),
                  pl.BlockSpec((None, s, LANES), lambda bi, h: (bi, 0, E_GATE + A_HEADS + h))],
        out_specs=pl.BlockSpec((None, s, LANES), lambda bi, h: (bi, 0, h)),
        out_shape=jax.ShapeDtypeStruct((b, s, B_HEADS * HEAD_DIM), BF16),
        compiler_params=_params(("arbitrary", "arbitrary")),
        name="moba_attention",
    )(p3, p3, p3, p3)


def _mem_kernel(q_ref, mk_ref, mv_ref, g_ref, o_ref, *, tq, c):
    mvt = _transpose_bf16(mv_ref[...])
    mk = mk_ref[...]
    for j in range(q_ref.shape[0] // tq):
        rows = slice(j * tq, (j + 1) * tq)
        ot = _softmax_pv_t([_scores_t(mk, q_ref[rows, :])], [None], mvt, c)
        o_ref[rows, :] = (ot.T * _silu(g_ref[rows, :].astype(F32))).astype(o_ref.dtype)


def _mem_attention(p3, memkv, q_block, gate_block):
    b, s, _ = p3.shape
    return pl.pallas_call(
        functools.partial(_mem_kernel, tq=min(s, 512), c=HEAD_DIM ** -0.5 * LOG2E),
        grid=(b, MEM_HEADS),
        in_specs=[pl.BlockSpec((None, s, LANES), lambda bi, h: (bi, 0, q_block + h)),
                  pl.BlockSpec((None, MEM_LEN, LANES), lambda bi, h: (bi, 0, h)),
                  pl.BlockSpec((None, MEM_LEN, LANES), lambda bi, h: (bi, 0, MEM_HEADS + h)),
                  pl.BlockSpec((None, s, LANES), lambda bi, h: (bi, 0, gate_block + h))],
        out_specs=pl.BlockSpec((None, s, LANES), lambda bi, h: (bi, 0, h)),
        out_shape=jax.ShapeDtypeStruct((b, s, MEM_HEADS * HEAD_DIM), BF16),
        compiler_params=_params(("arbitrary", "arbitrary")),
        name="memory_attention",
    )(p3, memkv, memkv, p3)


SWA_QBLK = 256
SWA_KSPAN = SWA_QBLK + C_WINDOW
SWA_GATE_BLOCKS = (C_HEADS * C_HEAD_DIM) // LANES
SWA_VT_WIDTH = 2 * LANES


def _swa_kernel(sinks_ref, q_ref, kv_ref, *rest, tq, c, scale):
    gate_refs = rest[:SWA_GATE_BLOCKS]
    o_ref, kvt_ref = rest[SWA_GATE_BLOCKS:]
    i = pl.program_id(1)
    heads_per_group = C_HEADS // C_KV_HEADS
    kw = C_KV_HEADS * C_HEAD_DIM

    @pl.when(i == 0)
    def _():
        for ch in range(kv_ref.shape[0] // LANES):
            t = kv_ref[ch * LANES:(ch + 1) * LANES, LANES:LANES + SWA_VT_WIDTH].astype(F32)
            kvt_ref[ch] = t.T.astype(BF16)

    row = lax.broadcasted_iota(jnp.int32, (SWA_KSPAN, SWA_QBLK), 0)
    col = lax.broadcasted_iota(jnp.int32, (SWA_KSPAN, SWA_QBLK), 1)
    cr = col - row
    for sub in range(tq // SWA_QBLK):
        rows = slice(sub * SWA_QBLK, (sub + 1) * SWA_QBLK)
        r0 = i * tq + sub * SWA_QBLK
        kstart = pl.multiple_of(jnp.maximum(r0 - C_WINDOW, 0), LANES)
        ch0 = jnp.right_shift(kstart, LANES.bit_length() - 1)
        kband = kv_ref[pl.ds(kstart, SWA_KSPAN), 0:kw]
        vtb = jnp.concatenate([kvt_ref[ch0 + u] for u in range(SWA_KSPAN // LANES)], axis=1)
        dist = cr + (r0 - kstart)
        bias = jnp.where(jnp.logical_and(dist >= 0, dist <= C_WINDOW - 1), 0.0, NEG)
        outs = []
        for g in range(C_KV_HEADS):
            kg = kband[:, g * C_HEAD_DIM:(g + 1) * C_HEAD_DIM]
            v0 = (kw + g * C_HEAD_DIM) - LANES
            vtg = vtb[v0:v0 + C_HEAD_DIM, :]
            for h in range(heads_per_group):
                head = g * heads_per_group + h
                qh = q_ref[rows, head * C_HEAD_DIM:(head + 1) * C_HEAD_DIM]
                st = _scores_t(kg, qh) + bias
                sink = sinks_ref[head] * (1.0 / scale)
                outs.append(_softmax_pv_t([st], [None], vtg, c, sink))
        o = jnp.concatenate(outs, axis=0).T
        gate = jnp.concatenate([r[rows, :] for r in gate_refs], axis=1).astype(F32)
        o_ref[rows, :] = (o * _silu(gate)).astype(o_ref.dtype)


def _swa_attention(p3, sinks):
    b, s, _ = p3.shape
    tq = min(s, 512)
    qw = C_HEADS * C_HEAD_DIM
    kvw = 2 * C_KV_HEADS * C_HEAD_DIM
    assert (O_KV * LANES) % kvw == 0 and kvw == LANES + SWA_VT_WIDTH
    gate_specs = [pl.BlockSpec((None, tq, LANES),
                               functools.partial(lambda bi, i, sk, u: (bi, i, O_GATE + u), u=u))
                  for u in range(SWA_GATE_BLOCKS)]
    scale = C_HEAD_DIM ** -0.5
    return pl.pallas_call(
        functools.partial(_swa_kernel, tq=tq, c=scale * LOG2E, scale=scale),
        grid_spec=pltpu.PrefetchScalarGridSpec(
            num_scalar_prefetch=1,
            grid=(b, s // tq),
            in_specs=[pl.BlockSpec((None, tq, qw), lambda bi, i, sk: (bi, i, 0)),
                      pl.BlockSpec((None, s, kvw), lambda bi, i, sk: (bi, 0, (O_KV * LANES) // kvw))]
                     + gate_specs,
            out_specs=pl.BlockSpec((None, tq, qw), lambda bi, i, sk: (bi, i, 0)),
            scratch_shapes=[pltpu.VMEM((s // LANES, SWA_VT_WIDTH, LANES), BF16)]),
        out_shape=jax.ShapeDtypeStruct((b, s, qw), BF16),
        compiler_params=_params(("arbitrary", "arbitrary")),
        name="swa_sink_attention",
    )(sinks, p3, p3, *([p3] * SWA_GATE_BLOCKS))


def _outproj_kernel(*refs, widths, tn, nj, emit_x):
    ny = len(widths)
    y_refs = refs[:ny]
    w_ref, x_ref, g_ref = refs[ny:ny + 3]
    outs = refs[ny + 3:]
    if emit_x:
        xo_ref, ho_ref, wbf_ref, xrow_ref = outs
    else:
        ho_ref, wbf_ref, xrow_ref = outs
    i = pl.program_id(0)
    j = pl.program_id(1)

    @pl.when(i == 0)
    def _():
        wbf_ref[j] = w_ref[...].astype(BF16)

    acc = x_ref[...]
    off = 0
    for y_ref, wd in zip(y_refs, widths):
        acc = acc + jnp.dot(y_ref[...], wbf_ref[j, off:off + wd, :], preferred_element_type=F32)
        off += wd
    if emit_x:
        xo_ref[...] = acc
    xrow_ref[j] = acc

    @pl.when(j == nj - 1)
    def _():
        ss = jnp.zeros((acc.shape[0], 1), F32)
        for cc in range(nj):
            xc = xrow_ref[cc]
            ss = ss + jnp.sum(xc * xc, axis=1, keepdims=True)
        r = lax.rsqrt(ss * (1.0 / (nj * tn)) + EPS)
        for cc in range(nj):
            ho_ref[:, cc * tn:(cc + 1) * tn] = (
                (xrow_ref[cc] * r) * g_ref[:, cc * tn:(cc + 1) * tn]).astype(ho_ref.dtype)


def _outproj(ys, w_stack, idx, x2d, g_next, final):
    t, d = x2d.shape
    widths = tuple(y.shape[1] for y in ys)
    kdim = w_stack.shape[1]
    assert sum(widths) == kdim and w_stack.shape[2] == d
    tm = min(t, 512)
    tn = min(d, 512)
    nj = d // tn
    in_specs = [pl.BlockSpec((tm, wd), lambda i, j: (i, 0)) for wd in widths]
    in_specs += [pl.BlockSpec((None, kdim, tn), lambda i, j: (idx, 0, jnp.where(i == 0, j, nj - 1))),
                 pl.BlockSpec((tm, tn), lambda i, j: (i, j)),
                 pl.BlockSpec((1, d), lambda i, j: (0, 0))]
    h_spec = pl.BlockSpec((tm, d), lambda i, j: (i, 0))
    if final:
        out_specs = [h_spec]
        out_shape = [jax.ShapeDtypeStruct((t, d), F32)]
    else:
        out_specs = [pl.BlockSpec((tm, tn), lambda i, j: (i, j)), h_spec]
        out_shape = [jax.ShapeDtypeStruct((t, d), F32), jax.ShapeDtypeStruct((t, d), BF16)]
    res = pl.pallas_call(
        functools.partial(_outproj_kernel, widths=widths, tn=tn, nj=nj, emit_x=not final),
        grid=(t // tm, nj),
        in_specs=in_specs,
        out_specs=out_specs,
        out_shape=out_shape,
        scratch_shapes=[pltpu.VMEM((nj, kdim, tn), BF16), pltpu.VMEM((nj, tm, tn), F32)],
        compiler_params=_params(("arbitrary", "arbitrary")),
        name="outproj_final" if final else "outproj",
    )(*ys, w_stack, x2d, g_next.reshape(1, d))
    return res[0] if final else (res[0], res[1])


def kernel(x, mem, positions, even_norm, even_w_in, even_w_mem_kv, even_w_out, odd_norm, odd_w_in,
           odd_w_mem_kv, odd_w_out, odd_sinks, mem_norm, final_norm):
    b, s, d = x.shape
    t = b * s
    depth = even_norm.shape[0] + odd_norm.shape[0]
    pos_col = positions.reshape(t, 1)
    cos_e, sin_e = _rope_tables(pos_col, HEAD_DIM)
    cos_o, sin_o = _rope_tables(pos_col, C_HEAD_DIM)
    kinds_e = _even_chunk_kinds()
    kinds_o = _odd_chunk_kinds()
    mem_n = _rmsnorm(mem.reshape(b * MEM_LEN, d), mem_norm)

    x2d = x.reshape(t, d)
    h = _rmsnorm(x2d, even_norm[0])
    out = None
    for layer in range(depth):
        idx = layer // 2
        last = layer == depth - 1
        if last:
            g_next = final_norm
        elif layer % 2 == 0:
            g_next = odd_norm[idx]
        else:
            g_next = even_norm[idx + 1]
        if layer % 2 == 0:
            memkv = _proj(mem_n, even_w_mem_kv, idx, 512).reshape(b, MEM_LEN, 2 * MEM_HEADS * HEAD_DIM)
            p3 = _proj(h, even_w_in, idx, 512, (kinds_e, cos_e, sin_e, HEAD_DIM)).reshape(b, s, EVEN_IN)
            ys = [_dilated_attention(p3), _moba_attention(p3), _mem_attention(p3, memkv, E_QM, E_GATE + 2 * A_HEADS)]
            w_out = even_w_out
        else:
            memkv = _proj(mem_n, odd_w_mem_kv, idx, 512).reshape(b, MEM_LEN, 2 * MEM_HEADS * HEAD_DIM)
            p3 = _proj(h, odd_w_in, idx, 896, (kinds_o, cos_o, sin_o, C_HEAD_DIM)).reshape(b, s, ODD_IN)
            ys = [_swa_attention(p3, odd_sinks[idx]), _mem_attention(p3, memkv, O_QM, O_GATE + SWA_GATE_BLOCKS)]
            w_out = odd_w_out
        ys = [y.reshape(t, y.shape[-1]) for y in ys]
        if last:
            out = _outproj(ys, w_out, idx, x2d, g_next, final=True)
        else:
            x2d, h = _outproj(ys, w_out, idx, x2d, g_next, final=False)
    return out.reshape(b, s, d)
```

```python
import functools
import math

import jax
import jax.numpy as jnp
from jax import lax
from jax.experimental import pallas as pl
from jax.experimental.pallas import tpu as pltpu

F32 = jnp.float32
BF16 = jnp.bfloat16

LANES = 128
HEAD_DIM = 128
MEM_LEN = 256
MEM_HEADS = 4
A_HEADS = 6
B_HEADS = 6
MOBA_BLOCK = 256
MOBA_TOPK = 3
C_HEADS = 24
C_KV_HEADS = 3
C_HEAD_DIM = 64
C_WINDOW = 128
ROPE_THETA = 10000.0
EPS = 1e-6
NEG = -1e30
LOG2E = 1.4426950408889634

EVEN_WIDTH = (A_HEADS + B_HEADS + MEM_HEADS) * HEAD_DIM
ODD_WIDTH = C_HEADS * C_HEAD_DIM + MEM_HEADS * HEAD_DIM
EVEN_IN = 3 * A_HEADS * HEAD_DIM + 3 * B_HEADS * HEAD_DIM + MEM_HEADS * HEAD_DIM + EVEN_WIDTH
ODD_IN = C_HEADS * C_HEAD_DIM + 2 * C_KV_HEADS * C_HEAD_DIM + MEM_HEADS * HEAD_DIM + ODD_WIDTH

E_QA, E_KA, E_VA = 0, A_HEADS, 2 * A_HEADS
E_QB, E_KB, E_VB = 3 * A_HEADS, 3 * A_HEADS + B_HEADS, 3 * A_HEADS + 2 * B_HEADS
E_QM = 3 * A_HEADS + 3 * B_HEADS
E_GATE = E_QM + MEM_HEADS
O_Q = 0
O_KV = (C_HEADS * C_HEAD_DIM) // LANES
O_QM = O_KV + (2 * C_KV_HEADS * C_HEAD_DIM) // LANES
O_GATE = O_QM + MEM_HEADS

VMEM_LIMIT = 56 * 1024 * 1024


def _params(sem):
    return pltpu.CompilerParams(dimension_semantics=sem, vmem_limit_bytes=VMEM_LIMIT)


def _silu(g):
    return g / (1.0 + jnp.exp(-g))


def _rope_table_kernel(pos_ref, invf_ref, sign_ref, cos_ref, sin_ref):
    ang = pos_ref[...].astype(F32) * invf_ref[...]
    cos_ref[...] = jnp.cos(ang)
    sin_ref[...] = jnp.sin(ang) * sign_ref[...]


def _rope_tables(pos_col, head_dim):
    t = pos_col.shape[0]
    half = head_dim // 2
    inv_freq = jnp.exp(jnp.arange(half, dtype=F32) * (-2.0 * math.log(ROPE_THETA) / head_dim))
    lane = jnp.arange(LANES)
    invf = inv_freq[lane % half][None, :]
    sign = jnp.where((lane % head_dim) < half, -1.0, 1.0).astype(F32)[None, :]
    tm = min(t, 1024)
    return pl.pallas_call(
        _rope_table_kernel,
        grid=(t // tm,),
        in_specs=[pl.BlockSpec((tm, 1), lambda i: (i, 0)),
                  pl.BlockSpec((1, LANES), lambda i: (0, 0)),
                  pl.BlockSpec((1, LANES), lambda i: (0, 0))],
        out_specs=[pl.BlockSpec((tm, LANES), lambda i: (i, 0)),
                   pl.BlockSpec((tm, LANES), lambda i: (i, 0))],
        out_shape=[jax.ShapeDtypeStruct((t, LANES), F32), jax.ShapeDtypeStruct((t, LANES), F32)],
        compiler_params=_params(("arbitrary",)),
        name=f"rope_tables_{head_dim}",
    )(pos_col, invf, sign)


def _rmsnorm_kernel(x_ref, g_ref, o_ref):
    x = x_ref[...]
    ms = jnp.mean(x * x, axis=-1, keepdims=True)
    o_ref[...] = ((x * lax.rsqrt(ms + EPS)) * g_ref[...]).astype(o_ref.dtype)


def _rmsnorm(x2d, g):
    t, d = x2d.shape
    tm = min(t, 512)
    return pl.pallas_call(
        _rmsnorm_kernel,
        grid=(t // tm,),
        in_specs=[pl.BlockSpec((tm, d), lambda i: (i, 0)),
                  pl.BlockSpec((1, d), lambda i: (0, 0))],
        out_specs=pl.BlockSpec((tm, d), lambda i: (i, 0)),
        out_shape=jax.ShapeDtypeStruct((t, d), BF16),
        compiler_params=_params(("arbitrary",)),
        name="rmsnorm",
    )(x2d, g.reshape(1, d))


ROPE_NONE, ROPE_FULL, ROPE_FIRST_HALF = 0, 1, 2


def _rope_partner(x, head_dim):
    if head_dim == LANES:
        return pltpu.roll(x, LANES // 2, 1)
    lane = lax.broadcasted_iota(jnp.int32, x.shape, 1)
    half = head_dim // 2
    return jnp.where((lane % head_dim) < half, pltpu.roll(x, LANES - half, 1), pltpu.roll(x, half, 1))


def _proj_kernel(types_ref, h_ref, w_ref, cos_ref, sin_ref, o_ref, wbf_ref, *, tn, head_dim):
    j = pl.program_id(0)
    i = pl.program_id(1)

    @pl.when(i == 0)
    def _():
        wbf_ref[...] = w_ref[...].astype(BF16)

    acc = jnp.dot(h_ref[...], wbf_ref[...], preferred_element_type=F32)
    cos = cos_ref[...]
    sin = sin_ref[...]
    upper = lax.broadcasted_iota(jnp.int32, cos.shape, 1) >= LANES // 2
    nchunk = tn // LANES
    for c in range(nchunk):
        kind = types_ref[j * nchunk + c]
        x = acc[:, c * LANES:(c + 1) * LANES]
        roped = x * cos + _rope_partner(x, head_dim) * sin
        plain = jnp.logical_or(kind == ROPE_NONE, jnp.logical_and(kind == ROPE_FIRST_HALF, upper))
        o_ref[:, c * LANES:(c + 1) * LANES] = jnp.where(plain, x, roped).astype(o_ref.dtype)


def _proj_plain_kernel(h_ref, w_ref, o_ref, wbf_ref):
    @pl.when(pl.program_id(1) == 0)
    def _():
        wbf_ref[...] = w_ref[...].astype(BF16)

    o_ref[...] = jnp.dot(h_ref[...], wbf_ref[...], preferred_element_type=F32).astype(o_ref.dtype)


def _proj(h, w_stack, idx, tn, rope=None):
    t, d = h.shape
    n = w_stack.shape[2]
    assert n % tn == 0 and tn % LANES == 0
    tm = min(t, 1024)
    grid = (n // tn, t // tm)
    out_shape = jax.ShapeDtypeStruct((t, n), BF16)
    scratch = [pltpu.VMEM((d, tn), BF16)]
    if rope is None:
        return pl.pallas_call(
            _proj_plain_kernel,
            grid=grid,
            in_specs=[pl.BlockSpec((tm, d), lambda j, i: (i, 0)),
                      pl.BlockSpec((None, d, tn), lambda j, i: (idx, 0, j))],
            out_specs=pl.BlockSpec((tm, tn), lambda j, i: (i, j)),
            out_shape=out_shape,
            scratch_shapes=scratch,
            compiler_params=_params(("arbitrary", "arbitrary")),
            name="proj_plain",
        )(h, w_stack)
    kinds, cos, sin, head_dim = rope
    return pl.pallas_call(
        functools.partial(_proj_kernel, tn=tn, head_dim=head_dim),
        grid_spec=pltpu.PrefetchScalarGridSpec(
            num_scalar_prefetch=1,
            grid=grid,
            in_specs=[pl.BlockSpec((tm, d), lambda j, i, k: (i, 0)),
                      pl.BlockSpec((None, d, tn), lambda j, i, k: (idx, 0, j)),
                      pl.BlockSpec((tm, LANES), lambda j, i, k: (i, 0)),
                      pl.BlockSpec((tm, LANES), lambda j, i, k: (i, 0))],
            out_specs=pl.BlockSpec((tm, tn), lambda j, i, k: (i, j)),
            scratch_shapes=scratch),
        out_shape=out_shape,
        compiler_params=_params(("arbitrary", "arbitrary")),
        name=f"proj_rope{head_dim}",
    )(kinds, h, w_stack, cos, sin)


def _even_chunk_kinds():
    kinds = [ROPE_NONE] * (EVEN_IN // LANES)
    for start in (E_QA, E_KA, E_QB, E_KB):
        for c in range(start, start + A_HEADS):
            kinds[c] = ROPE_FULL
    return jnp.asarray(kinds, jnp.int32)


def _odd_chunk_kinds():
    kinds = [ROPE_NONE] * (ODD_IN // LANES)
    k_end = C_HEADS * C_HEAD_DIM + C_KV_HEADS * C_HEAD_DIM
    for c in range(len(kinds)):
        if (c + 1) * LANES <= k_end:
            kinds[c] = ROPE_FULL
        elif c * LANES < k_end:
            assert k_end - c * LANES == LANES // 2
            kinds[c] = ROPE_FIRST_HALF
    return jnp.asarray(kinds, jnp.int32)


def _scores_t(k, q):
    return lax.dot_general(k, q, (((1,), (1,)), ((), ())), preferred_element_type=F32)


def _transpose_bf16(x):
    return x.astype(F32).T.astype(BF16)


def _softmax_pv_t(tiles, weights, vt, c, sink=None):
    m = jnp.max(tiles[0], axis=0, keepdims=True)
    for s in tiles[1:]:
        m = jnp.maximum(m, jnp.max(s, axis=0, keepdims=True))
    if sink is not None:
        m = jnp.maximum(m, sink)
    l = None if sink is None else jnp.exp2((sink - m) * c)
    ps = []
    for s, w in zip(tiles, weights):
        p = jnp.exp2((s - m) * c)
        if w is not None:
            p = p * w
        lt = jnp.sum(p, axis=0, keepdims=True)
        l = lt if l is None else l + lt
        ps.append(p.astype(BF16))
    p_all = ps[0] if len(ps) == 1 else jnp.concatenate(ps, axis=0)
    return jnp.dot(vt, p_all, preferred_element_type=F32) * (1.0 / l)


def _tile_delta(blk):
    row = lax.broadcasted_iota(jnp.int32, (blk, blk), 0)
    col = lax.broadcasted_iota(jnp.int32, (blk, blk), 1)
    return col - row


def _dilated_kernel(q_ref, k_ref, v_ref, g_ref, o_ref, *, blk, nblk, c):
    vt = _transpose_bf16(v_ref[...])
    cr = _tile_delta(blk)

    def near(delta):
        dist = cr + delta * blk
        cnt = (jnp.where(dist <= 128, 1.0, 0.0)
               + jnp.where(jnp.logical_and((dist & 3) == 0, dist <= 512), 1.0, 0.0)
               + jnp.where((dist & 15) == 0, 1.0, 0.0))
        cnt = jnp.where(dist >= 0, cnt, 0.0)
        return jnp.where(cnt > 0.0, 0.0, NEG), cnt

    n_near = -(-512 // blk) + 1
    nears = [near(d) for d in range(min(n_near, nblk))]
    far_bias = jnp.where((cr & 15) == 0, 0.0, NEG)

    for i in range(nblk):
        rows = slice(i * blk, (i + 1) * blk)
        ext = (i + 1) * blk
        st = _scores_t(k_ref[0:ext, :], q_ref[rows, :])
        tiles, weights = [], []
        for n in range(i + 1):
            s_n = st[n * blk:(n + 1) * blk]
            if i - n < len(nears):
                bias, cnt = nears[i - n]
                tiles.append(s_n + bias)
                weights.append(cnt)
            else:
                tiles.append(s_n + far_bias)
                weights.append(None)
        ot = _softmax_pv_t(tiles, weights, vt[:, 0:ext], c)
        o_ref[rows, :] = (ot.T * _silu(g_ref[rows, :].astype(F32))).astype(o_ref.dtype)


def _dilated_attention(p3):
    b, s, _ = p3.shape
    blk = 256
    assert blk * (-(-512 // blk)) >= 512 and s % blk == 0
    return pl.pallas_call(
        functools.partial(_dilated_kernel, blk=blk, nblk=s // blk, c=HEAD_DIM ** -0.5 * LOG2E),
        grid=(b, A_HEADS),
        in_specs=[pl.BlockSpec((None, s, LANES), lambda bi, h: (bi, 0, E_QA + h)),
                  pl.BlockSpec((None, s, LANES), lambda bi, h: (bi, 0, E_KA + h)),
                  pl.BlockSpec((None, s, LANES), lambda bi, h: (bi, 0, E_VA + h)),
                  pl.BlockSpec((None, s, LANES), lambda bi, h: (bi, 0, E_GATE + h))],
        out_specs=pl.BlockSpec((None, s, LANES), lambda bi, h: (bi, 0, h)),
        out_shape=jax.ShapeDtypeStruct((b, s, A_HEADS * HEAD_DIM), BF16),
        compiler_params=_params(("arbitrary", "arbitrary")),
        name="dilated_attention",
    )(p3, p3, p3, p3)


def _moba_kernel(q_ref, k_ref, v_ref, g_ref, o_ref, *, blk, nblk, c):
    s_len = nblk * blk
    q = q_ref[...]
    vt = _transpose_bf16(v_ref[...])

    km = jnp.concatenate(
        [jnp.sum(k_ref[n * blk:(n + 1) * blk, :].astype(F32), axis=0, keepdims=True) * (1.0 / blk)
         for n in range(nblk)], axis=0)
    hi = km.astype(BF16).astype(F32)
    ksplit = jnp.concatenate([hi, km - hi], axis=0).astype(BF16)
    gt2 = _scores_t(ksplit, q)
    gate = gt2[:nblk] + gt2[nblk:]

    nid = lax.broadcasted_iota(jnp.int32, (nblk, s_len), 0)
    own = jnp.right_shift(lax.broadcasted_iota(jnp.int32, (nblk, s_len), 1), blk.bit_length() - 1)
    rank = jnp.zeros((nblk, s_len), jnp.int32)
    for mm in range(nblk):
        gm = gate[mm:mm + 1, :]
        beats = jnp.logical_or(gm > gate, jnp.logical_and(gm == gate, mm < nid))
        rank = rank + jnp.where(jnp.logical_and(beats, mm < own), 1, 0)
    sel_bias = jnp.where(jnp.logical_and(rank < MOBA_TOPK, nid < own), 0.0, NEG)
    causal_bias = jnp.where(_tile_delta(blk) >= 0, 0.0, NEG)

    for i in range(nblk):
        rows = slice(i * blk, (i + 1) * blk)
        ext = (i + 1) * blk
        st = _scores_t(k_ref[0:ext, :], q[rows, :])
        tiles = [st[n * blk:(n + 1) * blk] + sel_bias[n:n + 1, rows] for n in range(i)]
        tiles.append(st[i * blk:ext] + causal_bias)
        ot = _softmax_pv_t(tiles, [None] * (i + 1), vt[:, 0:ext], c)
        o_ref[rows, :] = (ot.T * _silu(g_ref[rows, :].astype(F32))).astype(o_ref.dtype)


def _moba_attention(p3):
    b, s, _ = p3.shape
    blk = MOBA_BLOCK
    nblk = s // blk
    assert nblk >= MOBA_TOPK and blk & (blk - 1) == 0
    return pl.pallas_call(
        functools.partial(_moba_kernel, blk=blk, nblk=nblk, c=HEAD_DIM ** -0.5 * LOG2E),
        grid=(b, B_HEADS),
        in_specs=[pl.BlockSpec((None, s, LANES), lambda bi, h: (bi, 0, E_QB + h)),
                  pl.BlockSpec((None, s, LANES), lambda bi, h: (bi, 0, E_KB + h)),
                  pl.BlockSpec((None, s, LANES), lambda bi, h: (bi, 0, E_VB + h)),
                  pl.BlockSpec((None, s, LANES), lambda bi, h: (bi, 0, E_GATE + A_HEADS + h))],
        out_specs=pl.BlockSpec((None, s, LANES), lambda bi, h: (bi, 0, h)),
        out_shape=jax.ShapeDtypeStruct((b, s, B_HEADS * HEAD_DIM), BF16),
        compiler_params=_params(("arbitrary", "arbitrary")),
        name="moba_attention",
    )(p3, p3, p3, p3)


def _mem_kernel(q_ref, mk_ref, mv_ref, g_ref, o_ref, *, tq, c):
    mvt = _transpose_bf16(mv_ref[...])
    mk = mk_ref[...]
    for j in range(q_ref.shape[0] // tq):
        rows = slice(j * tq, (j + 1) * tq)
        ot = _softmax_pv_t([_scores_t(mk, q_ref[rows, :])], [None], mvt, c)
        o_ref[rows, :] = (ot.T * _silu(g_ref[rows, :].astype(F32))).astype(o_ref.dtype)


def _mem_attention(p3, memkv, q_block, gate_block):
    b, s, _ = p3.shape
    return pl.pallas_call(
        functools.partial(_mem_kernel, tq=min(s, 1024), c=HEAD_DIM ** -0.5 * LOG2E),
        grid=(b, MEM_HEADS),
        in_specs=[pl.BlockSpec((None, s, LANES), lambda bi, h: (bi, 0, q_block + h)),
                  pl.BlockSpec((None, MEM_LEN, LANES), lambda bi, h: (bi, 0, h)),
                  pl.BlockSpec((None, MEM_LEN, LANES), lambda bi, h: (bi, 0, MEM_HEADS + h)),
                  pl.BlockSpec((None, s, LANES), lambda bi, h: (bi, 0, gate_block + h))],
        out_specs=pl.BlockSpec((None, s, LANES), lambda bi, h: (bi, 0, h)),
        out_shape=jax.ShapeDtypeStruct((b, s, MEM_HEADS * HEAD_DIM), BF16),
        compiler_params=_params(("arbitrary", "arbitrary")),
        name="memory_attention",
    )(p3, memkv, memkv, p3)


SWA_QBLK = 256
SWA_KSPAN = SWA_QBLK + C_WINDOW
SWA_GATE_BLOCKS = (C_HEADS * C_HEAD_DIM) // LANES
SWA_VT_WIDTH = 2 * LANES


def _swa_kernel(sinks_ref, q_ref, kv_ref, *rest, tq, c, scale):
    gate_refs = rest[:SWA_GATE_BLOCKS]
    o_ref, kvt_ref = rest[SWA_GATE_BLOCKS:]
    i = pl.program_id(1)
    heads_per_group = C_HEADS // C_KV_HEADS
    kw = C_KV_HEADS * C_HEAD_DIM

    @pl.when(i == 0)
    def _():
        for ch in range(kv_ref.shape[0] // LANES):
            t = kv_ref[ch * LANES:(ch + 1) * LANES, LANES:LANES + SWA_VT_WIDTH].astype(F32)
            kvt_ref[ch] = t.T.astype(BF16)

    row = lax.broadcasted_iota(jnp.int32, (SWA_KSPAN, SWA_QBLK), 0)
    col = lax.broadcasted_iota(jnp.int32, (SWA_KSPAN, SWA_QBLK), 1)
    cr = col - row
    for sub in range(tq // SWA_QBLK):
        rows = slice(sub * SWA_QBLK, (sub + 1) * SWA_QBLK)
        r0 = i * tq + sub * SWA_QBLK
        kstart = pl.multiple_of(jnp.maximum(r0 - C_WINDOW, 0), LANES)
        ch0 = jnp.right_shift(kstart, LANES.bit_length() - 1)
        kband = kv_ref[pl.ds(kstart, SWA_KSPAN), 0:kw]
        vtb = jnp.concatenate([kvt_ref[ch0 + u] for u in range(SWA_KSPAN // LANES)], axis=1)
        dist = cr + (r0 - kstart)
        bias = jnp.where(jnp.logical_and(dist >= 0, dist <= C_WINDOW - 1), 0.0, NEG)
        for g in range(C_KV_HEADS):
            kg = kband[:, g * C_HEAD_DIM:(g + 1) * C_HEAD_DIM]
            v0 = (kw + g * C_HEAD_DIM) - LANES
            vtg = vtb[v0:v0 + C_HEAD_DIM, :]
            heads = range(g * heads_per_group, (g + 1) * heads_per_group)
            qg = jnp.concatenate([q_ref[rows, hd * C_HEAD_DIM:(hd + 1) * C_HEAD_DIM] for hd in heads], axis=0)
            st = _scores_t(kg, qg)
            ps, inv_l = [], []
            for u, hd in enumerate(heads):
                s_h = st[:, u * SWA_QBLK:(u + 1) * SWA_QBLK] + bias
                sink = sinks_ref[hd] * (1.0 / scale)
                m = jnp.maximum(jnp.max(s_h, axis=0, keepdims=True), sink)
                p = jnp.exp2((s_h - m) * c)
                inv_l.append(1.0 / (jnp.sum(p, axis=0, keepdims=True) + jnp.exp2((sink - m) * c)))
                ps.append(p.astype(BF16))
            ot = jnp.dot(vtg, jnp.concatenate(ps, axis=1), preferred_element_type=F32)
            ot = ot * jnp.concatenate(inv_l, axis=1)
            for pair in range(heads_per_group // 2):
                both = jnp.concatenate([ot[:, (2 * pair + u) * SWA_QBLK:(2 * pair + u + 1) * SWA_QBLK]
                                        for u in range(2)], axis=0)
                blk_i = (g * heads_per_group) // 2 + pair
                gate = gate_refs[blk_i][rows, :].astype(F32)
                o_ref[rows, blk_i * LANES:(blk_i + 1) * LANES] = (both.T * _silu(gate)).astype(o_ref.dtype)


def _swa_attention(p3, sinks):
    b, s, _ = p3.shape
    tq = min(s, 512)
    qw = C_HEADS * C_HEAD_DIM
    kvw = 2 * C_KV_HEADS * C_HEAD_DIM
    assert (O_KV * LANES) % kvw == 0 and kvw == LANES + SWA_VT_WIDTH
    gate_specs = [pl.BlockSpec((None, tq, LANES),
                               functools.partial(lambda bi, i, sk, u: (bi, i, O_GATE + u), u=u))
                  for u in range(SWA_GATE_BLOCKS)]
    scale = C_HEAD_DIM ** -0.5
    return pl.pallas_call(
        functools.partial(_swa_kernel, tq=tq, c=scale * LOG2E, scale=scale),
        grid_spec=pltpu.PrefetchScalarGridSpec(
            num_scalar_prefetch=1,
            grid=(b, s // tq),
            in_specs=[pl.BlockSpec((None, tq, qw), lambda bi, i, sk: (bi, i, 0)),
                      pl.BlockSpec((None, s, kvw), lambda bi, i, sk: (bi, 0, (O_KV * LANES) // kvw))]
                     + gate_specs,
            out_specs=pl.BlockSpec((None, tq, qw), lambda bi, i, sk: (bi, i, 0)),
            scratch_shapes=[pltpu.VMEM((s // LANES, SWA_VT_WIDTH, LANES), BF16)]),
        out_shape=jax.ShapeDtypeStruct((b, s, qw), BF16),
        compiler_params=_params(("arbitrary", "arbitrary")),
        name="swa_sink_attention",
    )(sinks, p3, p3, *([p3] * SWA_GATE_BLOCKS))


def _cast_kernel(w_ref, o_ref):
    o_ref[...] = w_ref[...].astype(o_ref.dtype)


def _cast_bf16(w_stack):
    nl, k, n = w_stack.shape
    tk = min(k, 512)
    return pl.pallas_call(
        _cast_kernel,
        grid=(nl, k // tk),
        in_specs=[pl.BlockSpec((None, tk, n), lambda l, i: (l, i, 0))],
        out_specs=pl.BlockSpec((None, tk, n), lambda l, i: (l, i, 0)),
        out_shape=jax.ShapeDtypeStruct(w_stack.shape, BF16),
        compiler_params=_params(("arbitrary", "arbitrary")),
        name="cast_bf16",
    )(w_stack)


def _outproj_kernel(*refs, widths, tn, emit_x):
    ny = len(widths)
    y_refs = refs[:ny]
    w_ref, x_ref, g_ref = refs[ny:ny + 3]
    ho_ref = refs[-1]
    hold_ref = refs[ny + 3] if emit_x else ho_ref
    tm, d = x_ref.shape
    ss = jnp.zeros((tm, 1), F32)
    for j in range(d // tn):
        cols = slice(j * tn, (j + 1) * tn)
        acc = x_ref[:, cols]
        off = 0
        for y_ref, wd in zip(y_refs, widths):
            acc = acc + jnp.dot(y_ref[...], w_ref[off:off + wd, cols], preferred_element_type=F32)
            off += wd
        hold_ref[:, cols] = acc
        ss = ss + jnp.sum(acc * acc, axis=1, keepdims=True)
    r = lax.rsqrt(ss * (1.0 / d) + EPS)
    for j in range(d // tn):
        cols = slice(j * tn, (j + 1) * tn)
        ho_ref[:, cols] = ((hold_ref[:, cols] * r) * g_ref[:, cols]).astype(ho_ref.dtype)


def _outproj(ys, w_bf16, idx, x2d, g_next, final):
    t, d = x2d.shape
    widths = tuple(y.shape[1] for y in ys)
    kdim = w_bf16.shape[1]
    assert sum(widths) == kdim and w_bf16.shape[2] == d
    tm = min(t, 512)
    tn = min(d, 512)
    row_spec = pl.BlockSpec((tm, d), lambda i: (i, 0))
    in_specs = [pl.BlockSpec((tm, wd), lambda i: (i, 0)) for wd in widths]
    in_specs += [pl.BlockSpec((None, kdim, d), lambda i: (idx, 0, 0)), row_spec,
                 pl.BlockSpec((1, d), lambda i: (0, 0))]
    if final:
        out_specs = [row_spec]
        out_shape = [jax.ShapeDtypeStruct((t, d), F32)]
    else:
        out_specs = [row_spec, row_spec]
        out_shape = [jax.ShapeDtypeStruct((t, d), F32), jax.ShapeDtypeStruct((t, d), BF16)]
    res = pl.pallas_call(
        functools.partial(_outproj_kernel, widths=widths, tn=tn, emit_x=not final),
        grid=(t // tm,),
        in_specs=in_specs,
        out_specs=out_specs,
        out_shape=out_shape,
        compiler_params=_params(("arbitrary",)),
        name="outproj_final" if final else "outproj",
    )(*ys, w_bf16, x2d, g_next.reshape(1, d))
    return res[0] if final else (res[0], res[1])


def kernel(x, mem, positions, even_norm, even_w_in, even_w_mem_kv, even_w_out, odd_norm, odd_w_in,
           odd_w_mem_kv, odd_w_out, odd_sinks, mem_norm, final_norm):
    b, s, d = x.shape
    t = b * s
    depth = even_norm.shape[0] + odd_norm.shape[0]
    pos_col = positions.reshape(t, 1)
    cos_e, sin_e = _rope_tables(pos_col, HEAD_DIM)
    cos_o, sin_o = _rope_tables(pos_col, C_HEAD_DIM)
    kinds_e = _even_chunk_kinds()
    kinds_o = _odd_chunk_kinds()
    mem_n = _rmsnorm(mem.reshape(b * MEM_LEN, d), mem_norm)

    w_out_even = _cast_bf16(even_w_out)
    w_out_odd = _cast_bf16(odd_w_out)

    x2d = x.reshape(t, d)
    h = _rmsnorm(x2d, even_norm[0])
    out = None
    for layer in range(depth):
        idx = layer // 2
        last = layer == depth - 1
        if last:
            g_next = final_norm
        elif layer % 2 == 0:
            g_next = odd_norm[idx]
        else:
            g_next = even_norm[idx + 1]
        if layer % 2 == 0:
            memkv = _proj(mem_n, even_w_mem_kv, idx, 512).reshape(b, MEM_LEN, 2 * MEM_HEADS * HEAD_DIM)
            p3 = _proj(h, even_w_in, idx, 1024, (kinds_e, cos_e, sin_e, HEAD_DIM)).reshape(b, s, EVEN_IN)
            ys = [_dilated_attention(p3), _moba_attention(p3), _mem_attention(p3, memkv, E_QM, E_GATE + 2 * A_HEADS)]
            w_out = w_out_even
        else:
            memkv = _proj(mem_n, odd_w_mem_kv, idx, 512).reshape(b, MEM_LEN, 2 * MEM_HEADS * HEAD_DIM)
            p3 = _proj(h, odd_w_in, idx, 896, (kinds_o, cos_o, sin_o, C_HEAD_DIM)).reshape(b, s, ODD_IN)
            ys = [_swa_attention(p3, odd_sinks[idx]), _mem_attention(p3, memkv, O_QM, O_GATE + SWA_GATE_BLOCKS)]
            w_out = w_out_odd
        ys = [y.reshape(t, y.shape[-1]) for y in ys]
        if last:
            out = _outproj(ys, w_out, idx, x2d, g_next, final=True)
        else:
            x2d, h = _outproj(ys, w_out, idx, x2d, g_next, final=False)
    return out.reshape(b, s, d)
```

```python
import functools
import math

import jax
import jax.numpy as jnp
from jax import lax
from jax.experimental import pallas as pl
from jax.experimental.pallas import tpu as pltpu

F32 = jnp.float32
BF16 = jnp.bfloat16

LANES = 128
HEAD_DIM = 128
MEM_LEN = 256
MEM_HEADS = 4
A_HEADS = 6
B_HEADS = 6
MOBA_BLOCK = 256
MOBA_TOPK = 3
C_HEADS = 24
C_KV_HEADS = 3
C_HEAD_DIM = 64
C_WINDOW = 128
ROPE_THETA = 10000.0
EPS = 1e-6
NEG = -1e30
LOG2E = 1.4426950408889634

EVEN_WIDTH = (A_HEADS + B_HEADS + MEM_HEADS) * HEAD_DIM
ODD_WIDTH = C_HEADS * C_HEAD_DIM + MEM_HEADS * HEAD_DIM
EVEN_IN = 3 * A_HEADS * HEAD_DIM + 3 * B_HEADS * HEAD_DIM + MEM_HEADS * HEAD_DIM + EVEN_WIDTH
ODD_IN = C_HEADS * C_HEAD_DIM + 2 * C_KV_HEADS * C_HEAD_DIM + MEM_HEADS * HEAD_DIM + ODD_WIDTH

E_QA, E_KA, E_VA = 0, A_HEADS, 2 * A_HEADS
E_QB, E_KB, E_VB = 3 * A_HEADS, 3 * A_HEADS + B_HEADS, 3 * A_HEADS + 2 * B_HEADS
E_QM = 3 * A_HEADS + 3 * B_HEADS
E_GATE = E_QM + MEM_HEADS
O_Q = 0
O_KV = (C_HEADS * C_HEAD_DIM) // LANES
O_QM = O_KV + (2 * C_KV_HEADS * C_HEAD_DIM) // LANES
O_GATE = O_QM + MEM_HEADS

VMEM_LIMIT = 56 * 1024 * 1024


def _params(sem):
    return pltpu.CompilerParams(dimension_semantics=sem, vmem_limit_bytes=VMEM_LIMIT)


def _silu(g):
    return g / (1.0 + jnp.exp(-g))


def _rope_table_kernel(pos_ref, invf_ref, sign_ref, cos_ref, sin_ref):
    ang = pos_ref[...].astype(F32) * invf_ref[...]
    cos_ref[...] = jnp.cos(ang)
    sin_ref[...] = jnp.sin(ang) * sign_ref[...]


def _rope_tables(pos_col, head_dim):
    t = pos_col.shape[0]
    half = head_dim // 2
    inv_freq = jnp.exp(jnp.arange(half, dtype=F32) * (-2.0 * math.log(ROPE_THETA) / head_dim))
    lane = jnp.arange(LANES)
    invf = inv_freq[lane % half][None, :]
    sign = jnp.where((lane % head_dim) < half, -1.0, 1.0).astype(F32)[None, :]
    tm = min(t, 1024)
    return pl.pallas_call(
        _rope_table_kernel,
        grid=(t // tm,),
        in_specs=[pl.BlockSpec((tm, 1), lambda i: (i, 0)),
                  pl.BlockSpec((1, LANES), lambda i: (0, 0)),
                  pl.BlockSpec((1, LANES), lambda i: (0, 0))],
        out_specs=[pl.BlockSpec((tm, LANES), lambda i: (i, 0)),
                   pl.BlockSpec((tm, LANES), lambda i: (i, 0))],
        out_shape=[jax.ShapeDtypeStruct((t, LANES), F32), jax.ShapeDtypeStruct((t, LANES), F32)],
        compiler_params=_params(("arbitrary",)),
        name=f"rope_tables_{head_dim}",
    )(pos_col, invf, sign)


def _rmsnorm_kernel(x_ref, g_ref, o_ref):
    x = x_ref[...]
    ms = jnp.mean(x * x, axis=-1, keepdims=True)
    o_ref[...] = ((x * lax.rsqrt(ms + EPS)) * g_ref[...]).astype(o_ref.dtype)


def _rmsnorm(x2d, g):
    t, d = x2d.shape
    tm = min(t, 512)
    return pl.pallas_call(
        _rmsnorm_kernel,
        grid=(t // tm,),
        in_specs=[pl.BlockSpec((tm, d), lambda i: (i, 0)),
                  pl.BlockSpec((1, d), lambda i: (0, 0))],
        out_specs=pl.BlockSpec((tm, d), lambda i: (i, 0)),
        out_shape=jax.ShapeDtypeStruct((t, d), BF16),
        compiler_params=_params(("arbitrary",)),
        name="rmsnorm",
    )(x2d, g.reshape(1, d))


ROPE_NONE, ROPE_FULL, ROPE_FIRST_HALF = 0, 1, 2


def _rope_partner(x, head_dim):
    if head_dim == LANES:
        return pltpu.roll(x, LANES // 2, 1)
    lane = lax.broadcasted_iota(jnp.int32, x.shape, 1)
    half = head_dim // 2
    return jnp.where((lane % head_dim) < half, pltpu.roll(x, LANES - half, 1), pltpu.roll(x, half, 1))


def _proj_kernel(types_ref, h_ref, w_ref, cos_ref, sin_ref, o_ref, wbf_ref, *, tn, head_dim):
    j = pl.program_id(0)
    i = pl.program_id(1)

    @pl.when(i == 0)
    def _():
        wbf_ref[...] = w_ref[...].astype(BF16)

    nchunk = tn // LANES
    any_rope = types_ref[pl.num_programs(0) * nchunk + j]

    @pl.when(any_rope == 0)
    def _():
        o_ref[...] = jnp.dot(h_ref[...], wbf_ref[...], preferred_element_type=F32).astype(o_ref.dtype)

    @pl.when(any_rope != 0)
    def _():
        acc = jnp.dot(h_ref[...], wbf_ref[...], preferred_element_type=F32)
        cos = cos_ref[...]
        sin = sin_ref[...]
        upper = lax.broadcasted_iota(jnp.int32, cos.shape, 1) >= LANES // 2
        for c in range(nchunk):
            kind = types_ref[j * nchunk + c]
            x = acc[:, c * LANES:(c + 1) * LANES]
            roped = x * cos + _rope_partner(x, head_dim) * sin
            plain = jnp.logical_or(kind == ROPE_NONE, jnp.logical_and(kind == ROPE_FIRST_HALF, upper))
            o_ref[:, c * LANES:(c + 1) * LANES] = jnp.where(plain, x, roped).astype(o_ref.dtype)


def _proj_plain_kernel(h_ref, w_ref, o_ref, wbf_ref):
    @pl.when(pl.program_id(1) == 0)
    def _():
        wbf_ref[...] = w_ref[...].astype(BF16)

    o_ref[...] = jnp.dot(h_ref[...], wbf_ref[...], preferred_element_type=F32).astype(o_ref.dtype)


def _proj(h, w_stack, idx, tn, rope=None):
    t, d = h.shape
    n = w_stack.shape[2]
    assert n % tn == 0 and tn % LANES == 0
    tm = min(t, 1024)
    grid = (n // tn, t // tm)
    out_shape = jax.ShapeDtypeStruct((t, n), BF16)
    scratch = [pltpu.VMEM((d, tn), BF16)]
    if rope is None:
        return pl.pallas_call(
            _proj_plain_kernel,
            grid=grid,
            in_specs=[pl.BlockSpec((tm, d), lambda j, i: (i, 0)),
                      pl.BlockSpec((None, d, tn), lambda j, i: (idx, 0, j))],
            out_specs=pl.BlockSpec((tm, tn), lambda j, i: (i, j)),
            out_shape=out_shape,
            scratch_shapes=scratch,
            compiler_params=_params(("arbitrary", "arbitrary")),
            name="proj_plain",
        )(h, w_stack)
    kinds, cos, sin, head_dim = rope
    per_tile = kinds.reshape(n // tn, tn // LANES)
    kinds = jnp.concatenate([kinds, (per_tile != ROPE_NONE).any(axis=1).astype(jnp.int32)])
    return pl.pallas_call(
        functools.partial(_proj_kernel, tn=tn, head_dim=head_dim),
        grid_spec=pltpu.PrefetchScalarGridSpec(
            num_scalar_prefetch=1,
            grid=grid,
            in_specs=[pl.BlockSpec((tm, d), lambda j, i, k: (i, 0)),
                      pl.BlockSpec((None, d, tn), lambda j, i, k: (idx, 0, j)),
                      pl.BlockSpec((tm, LANES), lambda j, i, k: (i, 0)),
                      pl.BlockSpec((tm, LANES), lambda j, i, k: (i, 0))],
            out_specs=pl.BlockSpec((tm, tn), lambda j, i, k: (i, j)),
            scratch_shapes=scratch),
        out_shape=out_shape,
        compiler_params=_params(("arbitrary", "arbitrary")),
        name=f"proj_rope{head_dim}",
    )(kinds, h, w_stack, cos, sin)


def _even_chunk_kinds():
    kinds = [ROPE_NONE] * (EVEN_IN // LANES)
    for start in (E_QA, E_KA, E_QB, E_KB):
        for c in range(start, start + A_HEADS):
            kinds[c] = ROPE_FULL
    return jnp.asarray(kinds, jnp.int32)


def _odd_chunk_kinds():
    kinds = [ROPE_NONE] * (ODD_IN // LANES)
    k_end = C_HEADS * C_HEAD_DIM + C_KV_HEADS * C_HEAD_DIM
    for c in range(len(kinds)):
        if (c + 1) * LANES <= k_end:
            kinds[c] = ROPE_FULL
        elif c * LANES < k_end:
            assert k_end - c * LANES == LANES // 2
            kinds[c] = ROPE_FIRST_HALF
    return jnp.asarray(kinds, jnp.int32)


def _scores_t(k, q):
    return lax.dot_general(k, q, (((1,), (1,)), ((), ())), preferred_element_type=F32)


def _transpose_bf16(x):
    return x.astype(F32).T.astype(BF16)


def _softmax_probs_t(tiles, biases, weights, c):
    m = None
    for s, b in zip(tiles, biases):
        if b is None:
            mt = jnp.max(s, axis=0, keepdims=True)
        elif b.shape[0] == 1:
            mt = jnp.max(s, axis=0, keepdims=True) + b
        else:
            mt = jnp.max(s + b, axis=0, keepdims=True)
        m = mt if m is None else jnp.maximum(m, mt)
    shifts = {}
    l = None
    ps = []
    for s, b, w in zip(tiles, biases, weights):
        if b is None:
            shift = -m
        else:
            if id(b) not in shifts:
                shifts[id(b)] = b - m
            shift = shifts[id(b)]
        p = jnp.exp2((s + shift) * c)
        if w is not None:
            p = p * w
        lt = jnp.sum(p, axis=0, keepdims=True)
        l = lt if l is None else l + lt
        ps.append(p.astype(BF16))
    p_all = ps[0] if len(ps) == 1 else jnp.concatenate(ps, axis=0)
    return p_all, 1.0 / l


SCORE_LOOKAHEAD = 2


def _tile_delta(blk):
    row = lax.broadcasted_iota(jnp.int32, (blk, blk), 0)
    col = lax.broadcasted_iota(jnp.int32, (blk, blk), 1)
    return col - row


def _dilated_kernel(q_ref, k_ref, v_ref, g_ref, o_ref, *, blk, nblk, c):
    vt = _transpose_bf16(v_ref[...])
    cr = _tile_delta(blk)

    def near(delta):
        dist = cr + delta * blk
        cnt = (jnp.where(dist <= 128, 1.0, 0.0)
               + jnp.where(jnp.logical_and((dist & 3) == 0, dist <= 512), 1.0, 0.0)
               + jnp.where((dist & 15) == 0, 1.0, 0.0))
        cnt = jnp.where(dist >= 0, cnt, 0.0)
        return jnp.where(cnt > 0.0, 0.0, NEG), cnt

    n_near = -(-512 // blk) + 1
    nears = [near(d) for d in range(min(n_near, nblk))]
    far_bias = jnp.where((cr & 15) == 0, 0.0, NEG)

    def scores(u):
        return _scores_t(k_ref[0:(u + 1) * blk, :], q_ref[u * blk:(u + 1) * blk, :])

    sts = [scores(u) for u in range(min(SCORE_LOOKAHEAD, nblk))]
    for i in range(nblk):
        rows = slice(i * blk, (i + 1) * blk)
        ext = (i + 1) * blk
        if i + SCORE_LOOKAHEAD < nblk:
            sts.append(scores(i + SCORE_LOOKAHEAD))
        st = sts[i]
        tiles, biases, weights = [], [], []
        for n in range(i + 1):
            tiles.append(st[n * blk:(n + 1) * blk])
            if i - n < len(nears):
                bias, cnt = nears[i - n]
                biases.append(bias)
                weights.append(cnt)
            else:
                biases.append(far_bias)
                weights.append(None)
        p_all, inv_l = _softmax_probs_t(tiles, biases, weights, c)
        ot = jnp.dot(vt[:, 0:ext], p_all, preferred_element_type=F32) * inv_l
        o_ref[rows, :] = (ot.T * _silu(g_ref[rows, :].astype(F32))).astype(o_ref.dtype)


def _dilated_attention(p3):
    b, s, _ = p3.shape
    blk = 256
    assert blk * (-(-512 // blk)) >= 512 and s % blk == 0
    return pl.pallas_call(
        functools.partial(_dilated_kernel, blk=blk, nblk=s // blk, c=HEAD_DIM ** -0.5 * LOG2E),
        grid=(b, A_HEADS),
        in_specs=[pl.BlockSpec((None, s, LANES), lambda bi, h: (bi, 0, E_QA + h)),
                  pl.BlockSpec((None, s, LANES), lambda bi, h: (bi, 0, E_KA + h)),
                  pl.BlockSpec((None, s, LANES), lambda bi, h: (bi, 0, E_VA + h)),
                  pl.BlockSpec((None, s, LANES), lambda bi, h: (bi, 0, E_GATE + h))],
        out_specs=pl.BlockSpec((None, s, LANES), lambda bi, h: (bi, 0, h)),
        out_shape=jax.ShapeDtypeStruct((b, s, A_HEADS * HEAD_DIM), BF16),
        compiler_params=_params(("arbitrary", "arbitrary")),
        name="dilated_attention",
    )(p3, p3, p3, p3)


def _moba_kernel(q_ref, k_ref, v_ref, g_ref, o_ref, *, blk, nblk, c):
    s_len = nblk * blk
    q = q_ref[...]
    vt = _transpose_bf16(v_ref[...])

    km = jnp.concatenate(
        [jnp.sum(k_ref[n * blk:(n + 1) * blk, :].astype(F32), axis=0, keepdims=True) * (1.0 / blk)
         for n in range(nblk)], axis=0)
    hi = km.astype(BF16).astype(F32)
    ksplit = jnp.concatenate([hi, km - hi], axis=0).astype(BF16)
    gt2 = _scores_t(ksplit, q)
    gate = gt2[:nblk] + gt2[nblk:]

    nid = lax.broadcasted_iota(jnp.int32, (nblk, s_len), 0)
    own = jnp.right_shift(lax.broadcasted_iota(jnp.int32, (nblk, s_len), 1), blk.bit_length() - 1)
    rank = jnp.zeros((nblk, s_len), jnp.int32)
    for mm in range(nblk):
        gm = gate[mm:mm + 1, :]
        beats = jnp.logical_or(gm > gate, jnp.logical_and(gm == gate, mm < nid))
        rank = rank + jnp.where(jnp.logical_and(beats, mm < own), 1, 0)
    sel_bias = jnp.where(jnp.logical_and(rank < MOBA_TOPK, nid < own), 0.0, NEG)
    causal_bias = jnp.where(_tile_delta(blk) >= 0, 0.0, NEG)

    def scores(u):
        return _scores_t(k_ref[0:(u + 1) * blk, :], q[u * blk:(u + 1) * blk, :])

    sts = [scores(u) for u in range(min(SCORE_LOOKAHEAD, nblk))]
    for i in range(nblk):
        rows = slice(i * blk, (i + 1) * blk)
        ext = (i + 1) * blk
        if i + SCORE_LOOKAHEAD < nblk:
            sts.append(scores(i + SCORE_LOOKAHEAD))
        st = sts[i]
        tiles = [st[n * blk:(n + 1) * blk] for n in range(i + 1)]
        biases = [sel_bias[n:n + 1, rows] for n in range(i)] + [causal_bias]
        p_all, inv_l = _softmax_probs_t(tiles, biases, [None] * (i + 1), c)
        ot = jnp.dot(vt[:, 0:ext], p_all, preferred_element_type=F32) * inv_l
        o_ref[rows, :] = (ot.T * _silu(g_ref[rows, :].astype(F32))).astype(o_ref.dtype)


def _moba_attention(p3):
    b, s, _ = p3.shape
    blk = MOBA_BLOCK
    nblk = s // blk
    assert nblk >= MOBA_TOPK and blk & (blk - 1) == 0
    return pl.pallas_call(
        functools.partial(_moba_kernel, blk=blk, nblk=nblk, c=HEAD_DIM ** -0.5 * LOG2E),
        grid=(b, B_HEADS),
        in_specs=[pl.BlockSpec((None, s, LANES), lambda bi, h: (bi, 0, E_QB + h)),
                  pl.BlockSpec((None, s, LANES), lambda bi, h: (bi, 0, E_KB + h)),
                  pl.BlockSpec((None, s, LANES), lambda bi, h: (bi, 0, E_VB + h)),
                  pl.BlockSpec((None, s, LANES), lambda bi, h: (bi, 0, E_GATE + A_HEADS + h))],
        out_specs=pl.BlockSpec((None, s, LANES), lambda bi, h: (bi, 0, h)),
        out_shape=jax.ShapeDtypeStruct((b, s, B_HEADS * HEAD_DIM), BF16),
        compiler_params=_params(("arbitrary", "arbitrary")),
        name="moba_attention",
    )(p3, p3, p3, p3)


def _mem_kernel(q_ref, mk_ref, mv_ref, g_ref, o_ref, *, tq, c):
    mvt = _transpose_bf16(mv_ref[...])
    mk = mk_ref[...]
    units = q_ref.shape[0] // tq

    def scores(u):
        return _scores_t(mk, q_ref[u * tq:(u + 1) * tq, :])

    sts = [scores(u) for u in range(min(SCORE_LOOKAHEAD, units))]
    for j in range(units):
        rows = slice(j * tq, (j + 1) * tq)
        if j + SCORE_LOOKAHEAD < units:
            sts.append(scores(j + SCORE_LOOKAHEAD))
        p_all, inv_l = _softmax_probs_t([sts[j]], [None], [None], c)
        ot = jnp.dot(mvt, p_all, preferred_element_type=F32) * inv_l
        o_ref[rows, :] = (ot.T * _silu(g_ref[rows, :].astype(F32))).astype(o_ref.dtype)


def _mem_attention(p3, memkv, q_block, gate_block):
    b, s, _ = p3.shape
    return pl.pallas_call(
        functools.partial(_mem_kernel, tq=min(s, 512), c=HEAD_DIM ** -0.5 * LOG2E),
        grid=(b, MEM_HEADS),
        in_specs=[pl.BlockSpec((None, s, LANES), lambda bi, h: (bi, 0, q_block + h)),
                  pl.BlockSpec((None, MEM_LEN, LANES), lambda bi, h: (bi, 0, h)),
                  pl.BlockSpec((None, MEM_LEN, LANES), lambda bi, h: (bi, 0, MEM_HEADS + h)),
                  pl.BlockSpec((None, s, LANES), lambda bi, h: (bi, 0, gate_block + h))],
        out_specs=pl.BlockSpec((None, s, LANES), lambda bi, h: (bi, 0, h)),
        out_shape=jax.ShapeDtypeStruct((b, s, MEM_HEADS * HEAD_DIM), BF16),
        compiler_params=_params(("arbitrary", "arbitrary")),
        name="memory_attention",
    )(p3, memkv, memkv, p3)


SWA_QBLK = 128
SWA_KSPAN = SWA_QBLK + C_WINDOW
SWA_GATE_BLOCKS = (C_HEADS * C_HEAD_DIM) // LANES
SWA_VT_WIDTH = 2 * LANES


def _swa_kernel(sinks_ref, q_ref, kv_ref, *rest, tq, c, scale):
    gate_refs = rest[:SWA_GATE_BLOCKS]
    o_ref, kvt_ref = rest[SWA_GATE_BLOCKS:]
    i = pl.program_id(1)
    heads_per_group = C_HEADS // C_KV_HEADS
    kw = C_KV_HEADS * C_HEAD_DIM

    @pl.when(i == 0)
    def _():
        for ch in range(kv_ref.shape[0] // LANES):
            t = kv_ref[ch * LANES:(ch + 1) * LANES, LANES:LANES + SWA_VT_WIDTH].astype(F32)
            kvt_ref[ch] = t.T.astype(BF16)

    row = lax.broadcasted_iota(jnp.int32, (SWA_KSPAN, SWA_QBLK), 0)
    col = lax.broadcasted_iota(jnp.int32, (SWA_KSPAN, SWA_QBLK), 1)
    cr = col - row
    def band(sub):
        r0 = i * tq + sub * SWA_QBLK
        kstart = pl.multiple_of(jnp.maximum(r0 - C_WINDOW, 0), LANES)
        ch0 = jnp.right_shift(kstart, LANES.bit_length() - 1)
        kband = kv_ref[pl.ds(kstart, SWA_KSPAN), 0:kw]
        vtb = jnp.concatenate([kvt_ref[ch0 + u] for u in range(SWA_KSPAN // LANES)], axis=1)
        dist = cr + (r0 - kstart)
        bias = jnp.where(jnp.logical_and(dist >= 0, dist <= C_WINDOW - 1), 0.0, NEG)
        return kband, vtb, bias

    bands = [band(sub) for sub in range(tq // SWA_QBLK)]
    units = [(sub, g) for sub in range(tq // SWA_QBLK) for g in range(C_KV_HEADS)]

    def scores(unit):
        sub, g = unit
        rows = slice(sub * SWA_QBLK, (sub + 1) * SWA_QBLK)
        kg = bands[sub][0][:, g * C_HEAD_DIM:(g + 1) * C_HEAD_DIM]
        qg = jnp.concatenate([q_ref[rows, hd * C_HEAD_DIM:(hd + 1) * C_HEAD_DIM]
                              for hd in range(g * heads_per_group, (g + 1) * heads_per_group)], axis=0)
        return _scores_t(kg, qg)

    sts = [scores(u) for u in units[:SCORE_LOOKAHEAD]]
    for ui, (sub, g) in enumerate(units):
        if ui + SCORE_LOOKAHEAD < len(units):
            sts.append(scores(units[ui + SCORE_LOOKAHEAD]))
        st = sts[ui]
        rows = slice(sub * SWA_QBLK, (sub + 1) * SWA_QBLK)
        _, vtb, bias = bands[sub]
        v0 = (kw + g * C_HEAD_DIM) - LANES
        vtg = vtb[v0:v0 + C_HEAD_DIM, :]
        ps, inv_l = [], []
        for u in range(heads_per_group):
            s_h = st[:, u * SWA_QBLK:(u + 1) * SWA_QBLK] + bias
            sink = sinks_ref[g * heads_per_group + u] * (1.0 / scale)
            m = jnp.maximum(jnp.max(s_h, axis=0, keepdims=True), sink)
            p = jnp.exp2((s_h - m) * c)
            inv_l.append(1.0 / (jnp.sum(p, axis=0, keepdims=True) + jnp.exp2((sink - m) * c)))
            ps.append(p.astype(BF16))
        ot = jnp.dot(vtg, jnp.concatenate(ps, axis=1), preferred_element_type=F32)
        ot = ot * jnp.concatenate(inv_l, axis=1)
        for pair in range(heads_per_group // 2):
            both = jnp.concatenate([ot[:, (2 * pair + u) * SWA_QBLK:(2 * pair + u + 1) * SWA_QBLK]
                                    for u in range(2)], axis=0)
            blk_i = (g * heads_per_group) // 2 + pair
            gate = gate_refs[blk_i][rows, :].astype(F32)
            o_ref[rows, blk_i * LANES:(blk_i + 1) * LANES] = (both.T * _silu(gate)).astype(o_ref.dtype)


def _swa_attention(p3, sinks):
    b, s, _ = p3.shape
    tq = min(s, 512)
    qw = C_HEADS * C_HEAD_DIM
    kvw = 2 * C_KV_HEADS * C_HEAD_DIM
    assert (O_KV * LANES) % kvw == 0 and kvw == LANES + SWA_VT_WIDTH
    gate_specs = [pl.BlockSpec((None, tq, LANES),
                               functools.partial(lambda bi, i, sk, u: (bi, i, O_GATE + u), u=u))
                  for u in range(SWA_GATE_BLOCKS)]
    scale = C_HEAD_DIM ** -0.5
    return pl.pallas_call(
        functools.partial(_swa_kernel, tq=tq, c=scale * LOG2E, scale=scale),
        grid_spec=pltpu.PrefetchScalarGridSpec(
            num_scalar_prefetch=1,
            grid=(b, s // tq),
            in_specs=[pl.BlockSpec((None, tq, qw), lambda bi, i, sk: (bi, i, 0)),
                      pl.BlockSpec((None, s, kvw), lambda bi, i, sk: (bi, 0, (O_KV * LANES) // kvw))]
                     + gate_specs,
            out_specs=pl.BlockSpec((None, tq, qw), lambda bi, i, sk: (bi, i, 0)),
            scratch_shapes=[pltpu.VMEM((s // LANES, SWA_VT_WIDTH, LANES), BF16)]),
        out_shape=jax.ShapeDtypeStruct((b, s, qw), BF16),
        compiler_params=_params(("arbitrary", "arbitrary")),
        name="swa_sink_attention",
    )(sinks, p3, p3, *([p3] * SWA_GATE_BLOCKS))


def _cast_kernel(w_ref, o_ref):
    o_ref[...] = w_ref[...].astype(o_ref.dtype)


def _cast_bf16(w_stack):
    nl, k, n = w_stack.shape
    tk = min(k, 512)
    return pl.pallas_call(
        _cast_kernel,
        grid=(nl, k // tk),
        in_specs=[pl.BlockSpec((None, tk, n), lambda l, i: (l, i, 0))],
        out_specs=pl.BlockSpec((None, tk, n), lambda l, i: (l, i, 0)),
        out_shape=jax.ShapeDtypeStruct(w_stack.shape, BF16),
        compiler_params=_params(("arbitrary", "arbitrary")),
        name="cast_bf16",
    )(w_stack)


def _outproj_kernel(*refs, widths, tn, emit_x):
    ny = len(widths)
    y_refs = refs[:ny]
    w_ref, x_ref, g_ref = refs[ny:ny + 3]
    ho_ref = refs[-1]
    hold_ref = refs[ny + 3] if emit_x else ho_ref
    tm, d = x_ref.shape
    ss = jnp.zeros((tm, 1), F32)
    for j in range(d // tn):
        cols = slice(j * tn, (j + 1) * tn)
        acc = x_ref[:, cols]
        off = 0
        for y_ref, wd in zip(y_refs, widths):
            acc = acc + jnp.dot(y_ref[...], w_ref[off:off + wd, cols], preferred_element_type=F32)
            off += wd
        hold_ref[:, cols] = acc
        ss = ss + jnp.sum(acc * acc, axis=1, keepdims=True)
    r = lax.rsqrt(ss * (1.0 / d) + EPS)
    for j in range(d // tn):
        cols = slice(j * tn, (j + 1) * tn)
        ho_ref[:, cols] = ((hold_ref[:, cols] * r) * g_ref[:, cols]).astype(ho_ref.dtype)


def _outproj(ys, w_bf16, idx, x2d, g_next, final):
    t, d = x2d.shape
    widths = tuple(y.shape[1] for y in ys)
    kdim = w_bf16.shape[1]
    assert sum(widths) == kdim and w_bf16.shape[2] == d
    tm = min(t, 512)
    tn = min(d, 512)
    row_spec = pl.BlockSpec((tm, d), lambda i: (i, 0))
    in_specs = [pl.BlockSpec((tm, wd), lambda i: (i, 0)) for wd in widths]
    in_specs += [pl.BlockSpec((None, kdim, d), lambda i: (idx, 0, 0)), row_spec,
                 pl.BlockSpec((1, d), lambda i: (0, 0))]
    if final:
        out_specs = [row_spec]
        out_shape = [jax.ShapeDtypeStruct((t, d), F32)]
    else:
        out_specs = [row_spec, row_spec]
        out_shape = [jax.ShapeDtypeStruct((t, d), F32), jax.ShapeDtypeStruct((t, d), BF16)]
    res = pl.pallas_call(
        functools.partial(_outproj_kernel, widths=widths, tn=tn, emit_x=not final),
        grid=(t // tm,),
        in_specs=in_specs,
        out_specs=out_specs,
        out_shape=out_shape,
        compiler_params=_params(("arbitrary",)),
        name="outproj_final" if final else "outproj",
    )(*ys, w_bf16, x2d, g_next.reshape(1, d))
    return res[0] if final else (res[0], res[1])


def kernel(x, mem, positions, even_norm, even_w_in, even_w_mem_kv, even_w_out, odd_norm, odd_w_in,
           odd_w_mem_kv, odd_w_out, odd_sinks, mem_norm, final_norm):
    b, s, d = x.shape
    t = b * s
    depth = even_norm.shape[0] + odd_norm.shape[0]
    pos_col = positions.reshape(t, 1)
    cos_e, sin_e = _rope_tables(pos_col, HEAD_DIM)
    cos_o, sin_o = _rope_tables(pos_col, C_HEAD_DIM)
    kinds_e = _even_chunk_kinds()
    kinds_o = _odd_chunk_kinds()
    mem_n = _rmsnorm(mem.reshape(b * MEM_LEN, d), mem_norm)

    w_out_even = _cast_bf16(even_w_out)
    w_out_odd = _cast_bf16(odd_w_out)

    x2d = x.reshape(t, d)
    h = _rmsnorm(x2d, even_norm[0])
    out = None
    for layer in range(depth):
        idx = layer // 2
        last = layer == depth - 1
        if last:
            g_next = final_norm
        elif layer % 2 == 0:
            g_next = odd_norm[idx]
        else:
            g_next = even_norm[idx + 1]
        if layer % 2 == 0:
            memkv = _proj(mem_n, even_w_mem_kv, idx, 512).reshape(b, MEM_LEN, 2 * MEM_HEADS * HEAD_DIM)
            p3 = _proj(h, even_w_in, idx, 1024, (kinds_e, cos_e, sin_e, HEAD_DIM)).reshape(b, s, EVEN_IN)
            ys = [_dilated_attention(p3), _moba_attention(p3), _mem_attention(p3, memkv, E_QM, E_GATE + 2 * A_HEADS)]
            w_out = w_out_even
        else:
            memkv = _proj(mem_n, odd_w_mem_kv, idx, 512).reshape(b, MEM_LEN, 2 * MEM_HEADS * HEAD_DIM)
            p3 = _proj(h, odd_w_in, idx, 896, (kinds_o, cos_o, sin_o, C_HEAD_DIM)).reshape(b, s, ODD_IN)
            ys = [_swa_attention(p3, odd_sinks[idx]), _mem_attention(p3, memkv, O_QM, O_GATE + SWA_GATE_BLOCKS)]
            w_out = w_out_odd
        ys = [y.reshape(t, y.shape[-1]) for y in ys]
        if last:
            out = _outproj(ys, w_out, idx, x2d, g_next, final=True)
        else:
            x2d, h = _outproj(ys, w_out, idx, x2d, g_next, final=False)
    return out.reshape(b, s, d)
```

```python
import functools
import math

import jax
import jax.numpy as jnp
from jax import lax
from jax.experimental import pallas as pl
from jax.experimental.pallas import tpu as pltpu

F32 = jnp.float32
BF16 = jnp.bfloat16

LANES = 128
HEAD_DIM = 128
MEM_LEN = 256
MEM_HEADS = 4
A_HEADS = 6
B_HEADS = 6
MOBA_BLOCK = 256
MOBA_TOPK = 3
C_HEADS = 24
C_KV_HEADS = 3
C_HEAD_DIM = 64
C_WINDOW = 128
ROPE_THETA = 10000.0
EPS = 1e-6
NEG = -1e30
LOG2E = 1.4426950408889634

EVEN_WIDTH = (A_HEADS + B_HEADS + MEM_HEADS) * HEAD_DIM
ODD_WIDTH = C_HEADS * C_HEAD_DIM + MEM_HEADS * HEAD_DIM
EVEN_IN = 3 * A_HEADS * HEAD_DIM + 3 * B_HEADS * HEAD_DIM + MEM_HEADS * HEAD_DIM + EVEN_WIDTH
ODD_IN = C_HEADS * C_HEAD_DIM + 2 * C_KV_HEADS * C_HEAD_DIM + MEM_HEADS * HEAD_DIM + ODD_WIDTH

E_QA, E_KA, E_VA = 0, A_HEADS, 2 * A_HEADS
E_QB, E_KB, E_VB = 3 * A_HEADS, 3 * A_HEADS + B_HEADS, 3 * A_HEADS + 2 * B_HEADS
E_QM = 3 * A_HEADS + 3 * B_HEADS
E_GATE = E_QM + MEM_HEADS
O_Q = 0
O_KV = (C_HEADS * C_HEAD_DIM) // LANES
O_QM = O_KV + (2 * C_KV_HEADS * C_HEAD_DIM) // LANES
O_GATE = O_QM + MEM_HEADS

VMEM_LIMIT = 56 * 1024 * 1024


def _params(sem):
    return pltpu.CompilerParams(dimension_semantics=sem, vmem_limit_bytes=VMEM_LIMIT)


def _silu(g):
    return g / (1.0 + jnp.exp(-g))


def _rope_table_kernel(pos_ref, invf_ref, c_full_ref, s_full_ref, c_half_ref, s_half_ref):
    ang = pos_ref[...].astype(F32) * invf_ref[...]
    cos = jnp.cos(ang)
    sin = jnp.sin(ang)
    lane = lax.broadcasted_iota(jnp.int32, cos.shape, 1)
    half, quarter = LANES // 2, LANES // 4

    def spread(t, sign):
        swapped = pltpu.roll(t, half, 1)
        full = jnp.where(lane < half, sign * t, swapped)
        narrow = jnp.where(lane < quarter, sign * swapped,
                           jnp.where(lane < half, pltpu.roll(t, LANES - quarter, 1),
                                     jnp.where(lane < half + quarter, sign * t, pltpu.roll(t, quarter, 1))))
        return full, narrow

    c_full_ref[...], c_half_ref[...] = spread(cos, 1.0)
    s_full_ref[...], s_half_ref[...] = spread(sin, -1.0)


def _rope_tables(pos_col):
    t = pos_col.shape[0]

    def inv_freq(head_dim):
        return jnp.exp(jnp.arange(head_dim // 2, dtype=F32) * (-2.0 * math.log(ROPE_THETA) / head_dim))

    invf = jnp.concatenate([inv_freq(HEAD_DIM), inv_freq(C_HEAD_DIM),
                            jnp.zeros((LANES - HEAD_DIM // 2 - C_HEAD_DIM // 2,), F32)])[None, :]
    tm = min(t, 1024)
    tab = pl.BlockSpec((tm, LANES), lambda i: (i, 0))
    return pl.pallas_call(
        _rope_table_kernel,
        grid=(t // tm,),
        in_specs=[pl.BlockSpec((tm, 1), lambda i: (i, 0)),
                  pl.BlockSpec((1, LANES), lambda i: (0, 0))],
        out_specs=[tab] * 4,
        out_shape=[jax.ShapeDtypeStruct((t, LANES), F32)] * 4,
        compiler_params=_params(("arbitrary",)),
        name="rope_tables",
    )(pos_col, invf)


def _rmsnorm_kernel(x_ref, g_ref, o_ref):
    x = x_ref[...]
    ms = jnp.mean(x * x, axis=-1, keepdims=True)
    o_ref[...] = ((x * lax.rsqrt(ms + EPS)) * g_ref[...]).astype(o_ref.dtype)


def _rmsnorm(x2d, g):
    t, d = x2d.shape
    tm = min(t, 512)
    return pl.pallas_call(
        _rmsnorm_kernel,
        grid=(t // tm,),
        in_specs=[pl.BlockSpec((tm, d), lambda i: (i, 0)),
                  pl.BlockSpec((1, d), lambda i: (0, 0))],
        out_specs=pl.BlockSpec((tm, d), lambda i: (i, 0)),
        out_shape=jax.ShapeDtypeStruct((t, d), BF16),
        compiler_params=_params(("arbitrary",)),
        name="rmsnorm",
    )(x2d, g.reshape(1, d))


ROPE_NONE, ROPE_FULL, ROPE_FIRST_HALF = 0, 1, 2


def _rope_partner(x, head_dim):
    if head_dim == LANES:
        return pltpu.roll(x, LANES // 2, 1)
    lane = lax.broadcasted_iota(jnp.int32, x.shape, 1)
    half = head_dim // 2
    return jnp.where((lane % head_dim) < half, pltpu.roll(x, LANES - half, 1), pltpu.roll(x, half, 1))


def _proj_kernel(types_ref, h_ref, w_ref, cos_ref, sin_ref, o_ref, wbf_ref, *, tn, head_dim):
    j = pl.program_id(0)
    i = pl.program_id(1)

    @pl.when(i == 0)
    def _():
        wbf_ref[...] = w_ref[...].astype(BF16)

    nchunk = tn // LANES
    any_rope = types_ref[pl.num_programs(0) * nchunk + j]

    @pl.when(any_rope == 0)
    def _():
        o_ref[...] = jnp.dot(h_ref[...], wbf_ref[...], preferred_element_type=F32).astype(o_ref.dtype)

    @pl.when(any_rope != 0)
    def _():
        acc = jnp.dot(h_ref[...], wbf_ref[...], preferred_element_type=F32)
        cos = cos_ref[...]
        sin = sin_ref[...]
        upper = lax.broadcasted_iota(jnp.int32, cos.shape, 1) >= LANES // 2
        for c in range(nchunk):
            kind = types_ref[j * nchunk + c]
            x = acc[:, c * LANES:(c + 1) * LANES]
            roped = x * cos + _rope_partner(x, head_dim) * sin
            plain = jnp.logical_or(kind == ROPE_NONE, jnp.logical_and(kind == ROPE_FIRST_HALF, upper))
            o_ref[:, c * LANES:(c + 1) * LANES] = jnp.where(plain, x, roped).astype(o_ref.dtype)


def _proj_plain_kernel(h_ref, w_ref, o_ref, wbf_ref):
    @pl.when(pl.program_id(1) == 0)
    def _():
        wbf_ref[...] = w_ref[...].astype(BF16)

    o_ref[...] = jnp.dot(h_ref[...], wbf_ref[...], preferred_element_type=F32).astype(o_ref.dtype)


def _proj(h, w_stack, idx, tn, rope=None):
    t, d = h.shape
    n = w_stack.shape[2]
    assert n % tn == 0 and tn % LANES == 0
    tm = min(t, 1024)
    grid = (n // tn, t // tm)
    out_shape = jax.ShapeDtypeStruct((t, n), BF16)
    scratch = [pltpu.VMEM((d, tn), BF16)]
    if rope is None:
        return pl.pallas_call(
            _proj_plain_kernel,
            grid=grid,
            in_specs=[pl.BlockSpec((tm, d), lambda j, i: (i, 0)),
                      pl.BlockSpec((None, d, tn), lambda j, i: (idx, 0, j))],
            out_specs=pl.BlockSpec((tm, tn), lambda j, i: (i, j)),
            out_shape=out_shape,
            scratch_shapes=scratch,
            compiler_params=_params(("arbitrary", "arbitrary")),
            name="proj_plain",
        )(h, w_stack)
    kinds, cos, sin, head_dim = rope
    per_tile = kinds.reshape(n // tn, tn // LANES)
    kinds = jnp.concatenate([kinds, (per_tile != ROPE_NONE).any(axis=1).astype(jnp.int32)])
    return pl.pallas_call(
        functools.partial(_proj_kernel, tn=tn, head_dim=head_dim),
        grid_spec=pltpu.PrefetchScalarGridSpec(
            num_scalar_prefetch=1,
            grid=grid,
            in_specs=[pl.BlockSpec((tm, d), lambda j, i, k: (i, 0)),
                      pl.BlockSpec((None, d, tn), lambda j, i, k: (idx, 0, j)),
                      pl.BlockSpec((tm, LANES), lambda j, i, k: (i, 0)),
                      pl.BlockSpec((tm, LANES), lambda j, i, k: (i, 0))],
            out_specs=pl.BlockSpec((tm, tn), lambda j, i, k: (i, j)),
            scratch_shapes=scratch),
        out_shape=out_shape,
        compiler_params=_params(("arbitrary", "arbitrary")),
        name=f"proj_rope{head_dim}",
    )(kinds, h, w_stack, cos, sin)


def _even_chunk_kinds():
    kinds = [ROPE_NONE] * (EVEN_IN // LANES)
    for start in (E_QA, E_KA, E_QB, E_KB):
        for c in range(start, start + A_HEADS):
            kinds[c] = ROPE_FULL
    return jnp.asarray(kinds, jnp.int32)


def _odd_chunk_kinds():
    kinds = [ROPE_NONE] * (ODD_IN // LANES)
    k_end = C_HEADS * C_HEAD_DIM + C_KV_HEADS * C_HEAD_DIM
    for c in range(len(kinds)):
        if (c + 1) * LANES <= k_end:
            kinds[c] = ROPE_FULL
        elif c * LANES < k_end:
            assert k_end - c * LANES == LANES // 2
            kinds[c] = ROPE_FIRST_HALF
    return jnp.asarray(kinds, jnp.int32)


def _scores_t(k, q):
    return lax.dot_general(k, q, (((1,), (1,)), ((), ())), preferred_element_type=F32)


def _transpose_bf16(x):
    return x.astype(F32).T.astype(BF16)


def _softmax_probs_t(tiles, biases, weights, c):
    m = None
    for s, b in zip(tiles, biases):
        if b is None:
            mt = jnp.max(s, axis=0, keepdims=True)
        elif b.shape[0] == 1:
            mt = jnp.max(s, axis=0, keepdims=True) + b
        else:
            mt = jnp.max(s + b, axis=0, keepdims=True)
        m = mt if m is None else jnp.maximum(m, mt)
    shifts = {}
    l = None
    ps = []
    for s, b, w in zip(tiles, biases, weights):
        if b is None:
            shift = -m
        else:
            if id(b) not in shifts:
                shifts[id(b)] = b - m
            shift = shifts[id(b)]
        p = jnp.exp2((s + shift) * c)
        if w is not None:
            p = p * w
        lt = jnp.sum(p, axis=0, keepdims=True)
        l = lt if l is None else l + lt
        ps.append(p.astype(BF16))
    p_all = ps[0] if len(ps) == 1 else jnp.concatenate(ps, axis=0)
    return p_all, 1.0 / l


SCORE_LOOKAHEAD = 2


def _tile_delta(blk):
    row = lax.broadcasted_iota(jnp.int32, (blk, blk), 0)
    col = lax.broadcasted_iota(jnp.int32, (blk, blk), 1)
    return col - row


def _dilated_kernel(q_ref, k_ref, v_ref, g_ref, o_ref, *, blk, nblk, c):
    vt = _transpose_bf16(v_ref[...])
    cr = _tile_delta(blk)

    def near(delta):
        dist = cr + delta * blk
        cnt = (jnp.where(dist <= 128, 1.0, 0.0)
               + jnp.where(jnp.logical_and((dist & 3) == 0, dist <= 512), 1.0, 0.0)
               + jnp.where((dist & 15) == 0, 1.0, 0.0))
        cnt = jnp.where(dist >= 0, cnt, 0.0)
        return jnp.where(cnt > 0.0, 0.0, NEG), cnt

    n_near = -(-512 // blk) + 1
    nears = [near(d) for d in range(min(n_near, nblk))]
    far_bias = jnp.where((cr & 15) == 0, 0.0, NEG)

    def scores(u):
        return _scores_t(k_ref[0:(u + 1) * blk, :], q_ref[u * blk:(u + 1) * blk, :])

    sts = [scores(u) for u in range(min(SCORE_LOOKAHEAD, nblk))]
    for i in range(nblk):
        rows = slice(i * blk, (i + 1) * blk)
        ext = (i + 1) * blk
        if i + SCORE_LOOKAHEAD < nblk:
            sts.append(scores(i + SCORE_LOOKAHEAD))
        st = sts[i]
        tiles, biases, weights = [], [], []
        for n in range(i + 1):
            tiles.append(st[n * blk:(n + 1) * blk])
            if i - n < len(nears):
                bias, cnt = nears[i - n]
                biases.append(bias)
                weights.append(cnt)
            else:
                biases.append(far_bias)
                weights.append(None)
        p_all, inv_l = _softmax_probs_t(tiles, biases, weights, c)
        ot = jnp.dot(vt[:, 0:ext], p_all, preferred_element_type=F32) * inv_l
        o_ref[rows, :] = (ot.T * _silu(g_ref[rows, :].astype(F32))).astype(o_ref.dtype)


def _dilated_attention(p3):
    b, s, _ = p3.shape
    blk = 256
    assert blk * (-(-512 // blk)) >= 512 and s % blk == 0
    return pl.pallas_call(
        functools.partial(_dilated_kernel, blk=blk, nblk=s // blk, c=HEAD_DIM ** -0.5 * LOG2E),
        grid=(b, A_HEADS),
        in_specs=[pl.BlockSpec((None, s, LANES), lambda bi, h: (bi, 0, E_QA + h)),
                  pl.BlockSpec((None, s, LANES), lambda bi, h: (bi, 0, E_KA + h)),
                  pl.BlockSpec((None, s, LANES), lambda bi, h: (bi, 0, E_VA + h)),
                  pl.BlockSpec((None, s, LANES), lambda bi, h: (bi, 0, E_GATE + h))],
        out_specs=pl.BlockSpec((None, s, LANES), lambda bi, h: (bi, 0, h)),
        out_shape=jax.ShapeDtypeStruct((b, s, A_HEADS * HEAD_DIM), BF16),
        compiler_params=_params(("arbitrary", "arbitrary")),
        name="dilated_attention",
    )(p3, p3, p3, p3)


def _moba_kernel(q_ref, k_ref, v_ref, g_ref, o_ref, *, blk, nblk, c):
    s_len = nblk * blk
    q = q_ref[...]

    def scores(u):
        return _scores_t(k_ref[0:(u + 1) * blk, :], q[u * blk:(u + 1) * blk, :])

    sts = [scores(u) for u in range(min(SCORE_LOOKAHEAD, nblk))]
    vt = _transpose_bf16(v_ref[...])

    km = jnp.concatenate(
        [jnp.sum(k_ref[n * blk:(n + 1) * blk, :].astype(F32), axis=0, keepdims=True) * (1.0 / blk)
         for n in range(nblk)], axis=0)
    hi = km.astype(BF16).astype(F32)
    ksplit = jnp.concatenate([hi, km - hi], axis=0).astype(BF16)
    gt2 = _scores_t(ksplit, q)
    gate = gt2[:nblk] + gt2[nblk:]

    nid = lax.broadcasted_iota(jnp.int32, (nblk, s_len), 0)
    own = jnp.right_shift(lax.broadcasted_iota(jnp.int32, (nblk, s_len), 1), blk.bit_length() - 1)
    rank = jnp.zeros((nblk, s_len), jnp.int32)
    for mm in range(nblk):
        gm = gate[mm:mm + 1, :]
        beats = jnp.logical_or(gm > gate, jnp.logical_and(gm == gate, mm < nid))
        rank = rank + jnp.where(jnp.logical_and(beats, mm < own), 1, 0)
    sel_bias = jnp.where(jnp.logical_and(rank < MOBA_TOPK, nid < own), 0.0, NEG)
    causal_bias = jnp.where(_tile_delta(blk) >= 0, 0.0, NEG)

    for i in range(nblk):
        rows = slice(i * blk, (i + 1) * blk)
        ext = (i + 1) * blk
        if i + SCORE_LOOKAHEAD < nblk:
            sts.append(scores(i + SCORE_LOOKAHEAD))
        st = sts[i]
        tiles = [st[n * blk:(n + 1) * blk] for n in range(i + 1)]
        biases = [sel_bias[n:n + 1, rows] for n in range(i)] + [causal_bias]
        p_all, inv_l = _softmax_probs_t(tiles, biases, [None] * (i + 1), c)
        ot = jnp.dot(vt[:, 0:ext], p_all, preferred_element_type=F32) * inv_l
        o_ref[rows, :] = (ot.T * _silu(g_ref[rows, :].astype(F32))).astype(o_ref.dtype)


def _moba_attention(p3):
    b, s, _ = p3.shape
    blk = MOBA_BLOCK
    nblk = s // blk
    assert nblk >= MOBA_TOPK and blk & (blk - 1) == 0
    return pl.pallas_call(
        functools.partial(_moba_kernel, blk=blk, nblk=nblk, c=HEAD_DIM ** -0.5 * LOG2E),
        grid=(b, B_HEADS),
        in_specs=[pl.BlockSpec((None, s, LANES), lambda bi, h: (bi, 0, E_QB + h)),
                  pl.BlockSpec((None, s, LANES), lambda bi, h: (bi, 0, E_KB + h)),
                  pl.BlockSpec((None, s, LANES), lambda bi, h: (bi, 0, E_VB + h)),
                  pl.BlockSpec((None, s, LANES), lambda bi, h: (bi, 0, E_GATE + A_HEADS + h))],
        out_specs=pl.BlockSpec((None, s, LANES), lambda bi, h: (bi, 0, h)),
        out_shape=jax.ShapeDtypeStruct((b, s, B_HEADS * HEAD_DIM), BF16),
        compiler_params=_params(("arbitrary", "arbitrary")),
        name="moba_attention",
    )(p3, p3, p3, p3)


def _mem_kernel(*refs, tq, c):
    q_refs = refs[:MEM_HEADS]
    g_refs = refs[MEM_HEADS:2 * MEM_HEADS]
    memkv_ref, o_ref = refs[2 * MEM_HEADS:]
    kvw = MEM_HEADS * HEAD_DIM
    units = [(h, j) for h in range(MEM_HEADS) for j in range(o_ref.shape[0] // tq)]

    def scores(unit):
        h, j = unit
        return _scores_t(memkv_ref[:, h * HEAD_DIM:(h + 1) * HEAD_DIM], q_refs[h][j * tq:(j + 1) * tq, :])

    sts = [scores(u) for u in units[:SCORE_LOOKAHEAD]]
    mvts = [_transpose_bf16(memkv_ref[:, kvw + h * HEAD_DIM:kvw + (h + 1) * HEAD_DIM]) for h in range(MEM_HEADS)]
    for ui, (h, j) in enumerate(units):
        rows = slice(j * tq, (j + 1) * tq)
        if ui + SCORE_LOOKAHEAD < len(units):
            sts.append(scores(units[ui + SCORE_LOOKAHEAD]))
        p_all, inv_l = _softmax_probs_t([sts[ui]], [None], [None], c)
        ot = jnp.dot(mvts[h], p_all, preferred_element_type=F32) * inv_l
        gate = g_refs[h][rows, :].astype(F32)
        o_ref[rows, h * HEAD_DIM:(h + 1) * HEAD_DIM] = (ot.T * _silu(gate)).astype(o_ref.dtype)


def _mem_attention(p3, memkv, q_block, gate_block):
    b, s, _ = p3.shape

    def head_spec(first, h):
        return pl.BlockSpec((None, s, LANES), lambda bi: (bi, 0, first + h))

    width = MEM_HEADS * HEAD_DIM
    return pl.pallas_call(
        functools.partial(_mem_kernel, tq=min(s, 512), c=HEAD_DIM ** -0.5 * LOG2E),
        grid=(b,),
        in_specs=[head_spec(q_block, h) for h in range(MEM_HEADS)]
                 + [head_spec(gate_block, h) for h in range(MEM_HEADS)]
                 + [pl.BlockSpec((None, MEM_LEN, 2 * width), lambda bi: (bi, 0, 0))],
        out_specs=pl.BlockSpec((None, s, width), lambda bi: (bi, 0, 0)),
        out_shape=jax.ShapeDtypeStruct((b, s, width), BF16),
        compiler_params=_params(("arbitrary",)),
        name="memory_attention",
    )(*([p3] * (2 * MEM_HEADS)), memkv)


SWA_QBLK = C_WINDOW
SWA_GATE_BLOCKS = (C_HEADS * C_HEAD_DIM) // LANES
SWA_VT_WIDTH = 2 * LANES


def _swa_kernel(sinks_ref, q_ref, kv_ref, *rest, tq, c, scale):
    gate_refs = rest[:SWA_GATE_BLOCKS]
    o_ref, kvt_ref = rest[SWA_GATE_BLOCKS:]
    i = pl.program_id(1)
    heads_per_group = C_HEADS // C_KV_HEADS
    kw = C_KV_HEADS * C_HEAD_DIM

    @pl.when(i == 0)
    def _():
        for ch in range(kv_ref.shape[0] // LANES):
            t = kv_ref[ch * LANES:(ch + 1) * LANES, LANES:LANES + SWA_VT_WIDTH].astype(F32)
            kvt_ref[ch] = t.T.astype(BF16)

    krow = lax.broadcasted_iota(jnp.int32, (C_WINDOW, SWA_QBLK), 0)
    qcol = lax.broadcasted_iota(jnp.int32, (C_WINDOW, SWA_QBLK), 1)
    from_prev = krow > qcol
    zero_b = jnp.zeros((C_WINDOW, SWA_QBLK), BF16)
    shift = LANES.bit_length() - 1
    units = [(sub, g) for sub in range(tq // SWA_QBLK) for g in range(C_KV_HEADS)]

    def chunks(sub):
        r0 = i * tq + sub * SWA_QBLK
        prev = pl.multiple_of(jnp.maximum(r0 - C_WINDOW, 0), LANES)
        own = pl.multiple_of(r0, LANES)
        return r0, prev, own

    def scores(unit):
        sub, g = unit
        r0, prev, own = chunks(sub)
        rows = slice(sub * SWA_QBLK, (sub + 1) * SWA_QBLK)
        kcols = slice(g * C_HEAD_DIM, (g + 1) * C_HEAD_DIM)
        qg = jnp.concatenate([q_ref[rows, hd * C_HEAD_DIM:(hd + 1) * C_HEAD_DIM]
                              for hd in range(g * heads_per_group, (g + 1) * heads_per_group)], axis=0)
        s_prev = _scores_t(kv_ref[pl.ds(prev, C_WINDOW), kcols], qg)
        if sub == 0:
            s_prev = s_prev + jnp.where(r0 > 0, 0.0, NEG)
        return s_prev, _scores_t(kv_ref[pl.ds(own, C_WINDOW), kcols], qg)

    sts = [scores(u) for u in units[:SCORE_LOOKAHEAD]]
    for ui, (sub, g) in enumerate(units):
        if ui + SCORE_LOOKAHEAD < len(units):
            sts.append(scores(units[ui + SCORE_LOOKAHEAD]))
        s_prev, s_own = sts[ui]
        _, prev, own = chunks(sub)
        rows = slice(sub * SWA_QBLK, (sub + 1) * SWA_QBLK)
        v0 = (kw + g * C_HEAD_DIM) - LANES
        vt_prev = kvt_ref[jnp.right_shift(prev, shift)][v0:v0 + C_HEAD_DIM, :]
        vt_own = kvt_ref[jnp.right_shift(own, shift)][v0:v0 + C_HEAD_DIM, :]
        p_prev, p_own, inv_l = [], [], []
        for u in range(heads_per_group):
            cols = slice(u * SWA_QBLK, (u + 1) * SWA_QBLK)
            s_h = jnp.where(from_prev, s_prev[:, cols], s_own[:, cols])
            sink = sinks_ref[g * heads_per_group + u] * (1.0 / scale)
            m = jnp.maximum(jnp.max(s_h, axis=0, keepdims=True), sink)
            p = jnp.exp2((s_h - m) * c)
            inv_l.append(1.0 / (jnp.sum(p, axis=0, keepdims=True) + jnp.exp2((sink - m) * c)))
            pb = p.astype(BF16)
            p_prev.append(jnp.where(from_prev, pb, zero_b))
            p_own.append(jnp.where(from_prev, zero_b, pb))
        ot = (jnp.dot(vt_prev, jnp.concatenate(p_prev, axis=1), preferred_element_type=F32)
              + jnp.dot(vt_own, jnp.concatenate(p_own, axis=1), preferred_element_type=F32))
        ot = ot * jnp.concatenate(inv_l, axis=1)
        for pair in range(heads_per_group // 2):
            both = jnp.concatenate([ot[:, (2 * pair + u) * SWA_QBLK:(2 * pair + u + 1) * SWA_QBLK]
                                    for u in range(2)], axis=0)
            blk_i = (g * heads_per_group) // 2 + pair
            gate = gate_refs[blk_i][rows, :].astype(F32)
            o_ref[rows, blk_i * LANES:(blk_i + 1) * LANES] = (both.T * _silu(gate)).astype(o_ref.dtype)


def _swa_attention(p3, sinks):
    b, s, _ = p3.shape
    tq = min(s, 512)
    qw = C_HEADS * C_HEAD_DIM
    kvw = 2 * C_KV_HEADS * C_HEAD_DIM
    assert (O_KV * LANES) % kvw == 0 and kvw == LANES + SWA_VT_WIDTH
    gate_specs = [pl.BlockSpec((None, tq, LANES),
                               functools.partial(lambda bi, i, sk, u: (bi, i, O_GATE + u), u=u))
                  for u in range(SWA_GATE_BLOCKS)]
    scale = C_HEAD_DIM ** -0.5
    return pl.pallas_call(
        functools.partial(_swa_kernel, tq=tq, c=scale * LOG2E, scale=scale),
        grid_spec=pltpu.PrefetchScalarGridSpec(
            num_scalar_prefetch=1,
            grid=(b, s // tq),
            in_specs=[pl.BlockSpec((None, tq, qw), lambda bi, i, sk: (bi, i, 0)),
                      pl.BlockSpec((None, s, kvw), lambda bi, i, sk: (bi, 0, (O_KV * LANES) // kvw))]
                     + gate_specs,
            out_specs=pl.BlockSpec((None, tq, qw), lambda bi, i, sk: (bi, i, 0)),
            scratch_shapes=[pltpu.VMEM((s // LANES, SWA_VT_WIDTH, LANES), BF16)]),
        out_shape=jax.ShapeDtypeStruct((b, s, qw), BF16),
        compiler_params=_params(("arbitrary", "arbitrary")),
        name="swa_sink_attention",
    )(sinks, p3, p3, *([p3] * SWA_GATE_BLOCKS))


def _cast_kernel(w_ref, o_ref):
    o_ref[...] = w_ref[...].astype(o_ref.dtype)


def _cast_bf16(w_stack):
    nl, k, n = w_stack.shape
    tk = min(k, 512)
    return pl.pallas_call(
        _cast_kernel,
        grid=(nl, k // tk),
        in_specs=[pl.BlockSpec((None, tk, n), lambda l, i: (l, i, 0))],
        out_specs=pl.BlockSpec((None, tk, n), lambda l, i: (l, i, 0)),
        out_shape=jax.ShapeDtypeStruct(w_stack.shape, BF16),
        compiler_params=_params(("arbitrary", "arbitrary")),
        name="cast_bf16",
    )(w_stack)


def _outproj_kernel(*refs, widths, tn, emit_x):
    ny = len(widths)
    y_refs = refs[:ny]
    w_ref, x_ref, g_ref = refs[ny:ny + 3]
    ho_ref = refs[-1]
    hold_ref = refs[ny + 3] if emit_x else ho_ref
    tm, d = x_ref.shape
    ss = jnp.zeros((tm, 1), F32)
    for j in range(d // tn):
        cols = slice(j * tn, (j + 1) * tn)
        acc = x_ref[:, cols]
        off = 0
        for y_ref, wd in zip(y_refs, widths):
            acc = acc + jnp.dot(y_ref[...], w_ref[off:off + wd, cols], preferred_element_type=F32)
            off += wd
        hold_ref[:, cols] = acc
        ss = ss + jnp.sum(acc * acc, axis=1, keepdims=True)
    r = lax.rsqrt(ss * (1.0 / d) + EPS)
    for j in range(d // tn):
        cols = slice(j * tn, (j + 1) * tn)
        ho_ref[:, cols] = ((hold_ref[:, cols] * r) * g_ref[:, cols]).astype(ho_ref.dtype)


def _outproj(ys, w_bf16, idx, x2d, g_next, final):
    t, d = x2d.shape
    widths = tuple(y.shape[1] for y in ys)
    kdim = w_bf16.shape[1]
    assert sum(widths) == kdim and w_bf16.shape[2] == d
    tm = min(t, 512)
    tn = min(d, 512)
    row_spec = pl.BlockSpec((tm, d), lambda i: (i, 0))
    in_specs = [pl.BlockSpec((tm, wd), lambda i: (i, 0)) for wd in widths]
    in_specs += [pl.BlockSpec((None, kdim, d), lambda i: (idx, 0, 0)), row_spec,
                 pl.BlockSpec((1, d), lambda i: (0, 0))]
    if final:
        out_specs = [row_spec]
        out_shape = [jax.ShapeDtypeStruct((t, d), F32)]
    else:
        out_specs = [row_spec, row_spec]
        out_shape = [jax.ShapeDtypeStruct((t, d), F32), jax.ShapeDtypeStruct((t, d), BF16)]
    res = pl.pallas_call(
        functools.partial(_outproj_kernel, widths=widths, tn=tn, emit_x=not final),
        grid=(t // tm,),
        in_specs=in_specs,
        out_specs=out_specs,
        out_shape=out_shape,
        compiler_params=_params(("arbitrary",)),
        name="outproj_final" if final else "outproj",
    )(*ys, w_bf16, x2d, g_next.reshape(1, d))
    return res[0] if final else (res[0], res[1])


def kernel(x, mem, positions, even_norm, even_w_in, even_w_mem_kv, even_w_out, odd_norm, odd_w_in,
           odd_w_mem_kv, odd_w_out, odd_sinks, mem_norm, final_norm):
    b, s, d = x.shape
    t = b * s
    depth = even_norm.shape[0] + odd_norm.shape[0]
    pos_col = positions.reshape(t, 1)
    cos_e, sin_e, cos_o, sin_o = _rope_tables(pos_col)
    kinds_e = _even_chunk_kinds()
    kinds_o = _odd_chunk_kinds()
    mem_n = _rmsnorm(mem.reshape(b * MEM_LEN, d), mem_norm)

    w_out_even = _cast_bf16(even_w_out)
    w_out_odd = _cast_bf16(odd_w_out)

    x2d = x.reshape(t, d)
    h = _rmsnorm(x2d, even_norm[0])
    out = None
    for layer in range(depth):
        idx = layer // 2
        last = layer == depth - 1
        if last:
            g_next = final_norm
        elif layer % 2 == 0:
            g_next = odd_norm[idx]
        else:
            g_next = even_norm[idx + 1]
        if layer % 2 == 0:
            memkv = _proj(mem_n, even_w_mem_kv, idx, 512).reshape(b, MEM_LEN, 2 * MEM_HEADS * HEAD_DIM)
            p3 = _proj(h, even_w_in, idx, 1024, (kinds_e, cos_e, sin_e, HEAD_DIM)).reshape(b, s, EVEN_IN)
            ys = [_dilated_attention(p3), _moba_attention(p3), _mem_attention(p3, memkv, E_QM, E_GATE + 2 * A_HEADS)]
            w_out = w_out_even
        else:
            memkv = _proj(mem_n, odd_w_mem_kv, idx, 512).reshape(b, MEM_LEN, 2 * MEM_HEADS * HEAD_DIM)
            p3 = _proj(h, odd_w_in, idx, 896, (kinds_o, cos_o, sin_o, C_HEAD_DIM)).reshape(b, s, ODD_IN)
            ys = [_swa_attention(p3, odd_sinks[idx]), _mem_attention(p3, memkv, O_QM, O_GATE + SWA_GATE_BLOCKS)]
            w_out = w_out_odd
        ys = [y.reshape(t, y.shape[-1]) for y in ys]
        if last:
            out = _outproj(ys, w_out, idx, x2d, g_next, final=True)
        else:
            x2d, h = _outproj(ys, w_out, idx, x2d, g_next, final=False)
    return out.reshape(b, s, d)
```

```python
import functools
import math

import jax
import jax.numpy as jnp
from jax import lax
from jax.experimental import pallas as pl
from jax.experimental.pallas import tpu as pltpu

F32 = jnp.float32
BF16 = jnp.bfloat16

LANES = 128
HEAD_DIM = 128
MEM_LEN = 256
MEM_HEADS = 4
A_HEADS = 6
B_HEADS = 6
MOBA_BLOCK = 256
MOBA_TOPK = 3
C_HEADS = 24
C_KV_HEADS = 3
C_HEAD_DIM = 64
C_WINDOW = 128
ROPE_THETA = 10000.0
EPS = 1e-6
NEG = -1e30
LOG2E = 1.4426950408889634

EVEN_WIDTH = (A_HEADS + B_HEADS + MEM_HEADS) * HEAD_DIM
ODD_WIDTH = C_HEADS * C_HEAD_DIM + MEM_HEADS * HEAD_DIM
EVEN_IN = 3 * A_HEADS * HEAD_DIM + 3 * B_HEADS * HEAD_DIM + MEM_HEADS * HEAD_DIM + EVEN_WIDTH
ODD_IN = C_HEADS * C_HEAD_DIM + 2 * C_KV_HEADS * C_HEAD_DIM + MEM_HEADS * HEAD_DIM + ODD_WIDTH

E_QA, E_KA, E_VA = 0, A_HEADS, 2 * A_HEADS
E_QB, E_KB, E_VB = 3 * A_HEADS, 3 * A_HEADS + B_HEADS, 3 * A_HEADS + 2 * B_HEADS
E_QM = 3 * A_HEADS + 3 * B_HEADS
E_GATE = E_QM + MEM_HEADS
O_Q = 0
O_KV = (C_HEADS * C_HEAD_DIM) // LANES
O_QM = O_KV + (2 * C_KV_HEADS * C_HEAD_DIM) // LANES
O_GATE = O_QM + MEM_HEADS

ODD_TN = 896
ODD_SPLIT_TILES = 4

VMEM_LIMIT = 56 * 1024 * 1024


def _params(sem):
    return pltpu.CompilerParams(dimension_semantics=sem, vmem_limit_bytes=VMEM_LIMIT)


def _silu(g):
    return g / (1.0 + jnp.exp(-g))


def _rope_table_kernel(pos_ref, invf_ref, c_full_ref, s_full_ref, c_half_ref, s_half_ref):
    ang = pos_ref[...].astype(F32) * invf_ref[...]
    cos = jnp.cos(ang)
    sin = jnp.sin(ang)
    lane = lax.broadcasted_iota(jnp.int32, cos.shape, 1)
    half, quarter = LANES // 2, LANES // 4

    def spread(t, sign):
        swapped = pltpu.roll(t, half, 1)
        full = jnp.where(lane < half, sign * t, swapped)
        narrow = jnp.where(lane < quarter, sign * swapped,
                           jnp.where(lane < half, pltpu.roll(t, LANES - quarter, 1),
                                     jnp.where(lane < half + quarter, sign * t, pltpu.roll(t, quarter, 1))))
        return full, narrow

    c_full_ref[...], c_half_ref[...] = spread(cos, 1.0)
    s_full_ref[...], s_half_ref[...] = spread(sin, -1.0)


def _rope_tables(pos_col):
    t = pos_col.shape[0]

    def inv_freq(head_dim):
        return jnp.exp(jnp.arange(head_dim // 2, dtype=F32) * (-2.0 * math.log(ROPE_THETA) / head_dim))

    invf = jnp.concatenate([inv_freq(HEAD_DIM), inv_freq(C_HEAD_DIM),
                            jnp.zeros((LANES - HEAD_DIM // 2 - C_HEAD_DIM // 2,), F32)])[None, :]
    tm = min(t, 1024)
    tab = pl.BlockSpec((tm, LANES), lambda i: (i, 0))
    return pl.pallas_call(
        _rope_table_kernel,
        grid=(t // tm,),
        in_specs=[pl.BlockSpec((tm, 1), lambda i: (i, 0)),
                  pl.BlockSpec((1, LANES), lambda i: (0, 0))],
        out_specs=[tab] * 4,
        out_shape=[jax.ShapeDtypeStruct((t, LANES), F32)] * 4,
        compiler_params=_params(("arbitrary",)),
        name="rope_tables",
    )(pos_col, invf)


def _rmsnorm_kernel(x_ref, g_ref, o_ref):
    x = x_ref[...]
    ms = jnp.mean(x * x, axis=-1, keepdims=True)
    o_ref[...] = ((x * lax.rsqrt(ms + EPS)) * g_ref[...]).astype(o_ref.dtype)


def _rmsnorm(x2d, g):
    t, d = x2d.shape
    tm = min(t, 512)
    return pl.pallas_call(
        _rmsnorm_kernel,
        grid=(t // tm,),
        in_specs=[pl.BlockSpec((tm, d), lambda i: (i, 0)),
                  pl.BlockSpec((1, d), lambda i: (0, 0))],
        out_specs=pl.BlockSpec((tm, d), lambda i: (i, 0)),
        out_shape=jax.ShapeDtypeStruct((t, d), BF16),
        compiler_params=_params(("arbitrary",)),
        name="rmsnorm",
    )(x2d, g.reshape(1, d))


ROPE_NONE, ROPE_FULL, ROPE_FIRST_HALF = 0, 1, 2


def _rope_partner(x, head_dim):
    if head_dim == LANES:
        return pltpu.roll(x, LANES // 2, 1)
    lane = lax.broadcasted_iota(jnp.int32, x.shape, 1)
    half = head_dim // 2
    return jnp.where((lane % head_dim) < half, pltpu.roll(x, LANES - half, 1), pltpu.roll(x, half, 1))


def _proj_kernel(types_ref, h_ref, w_ref, cos_ref, sin_ref, o_ref, wbf_ref, *, tn, head_dim):
    j = pl.program_id(0)
    i = pl.program_id(1)

    @pl.when(i == 0)
    def _():
        wbf_ref[...] = w_ref[...].astype(BF16)

    nchunk = tn // LANES
    any_rope = types_ref[pl.num_programs(0) * nchunk + j]

    @pl.when(any_rope == 0)
    def _():
        o_ref[...] = jnp.dot(h_ref[...], wbf_ref[...], preferred_element_type=F32).astype(o_ref.dtype)

    @pl.when(any_rope != 0)
    def _():
        acc = jnp.dot(h_ref[...], wbf_ref[...], preferred_element_type=F32)
        cos = cos_ref[...]
        sin = sin_ref[...]
        upper = lax.broadcasted_iota(jnp.int32, cos.shape, 1) >= LANES // 2
        for c in range(nchunk):
            kind = types_ref[j * nchunk + c]
            x = acc[:, c * LANES:(c + 1) * LANES]
            roped = x * cos + _rope_partner(x, head_dim) * sin
            plain = jnp.logical_or(kind == ROPE_NONE, jnp.logical_and(kind == ROPE_FIRST_HALF, upper))
            o_ref[:, c * LANES:(c + 1) * LANES] = jnp.where(plain, x, roped).astype(o_ref.dtype)


def _proj_plain_kernel(h_ref, w_ref, o_ref, wbf_ref):
    @pl.when(pl.program_id(1) == 0)
    def _():
        wbf_ref[...] = w_ref[...].astype(BF16)

    o_ref[...] = jnp.dot(h_ref[...], wbf_ref[...], preferred_element_type=F32).astype(o_ref.dtype)


def _proj(h, w_stack, idx, tn, rope=None, first_tile=0, n_tiles=None, tm=1024):
    t, d = h.shape
    if n_tiles is None:
        n_tiles = w_stack.shape[2] // tn - first_tile
    assert tn % LANES == 0 and (first_tile + n_tiles) * tn <= w_stack.shape[2]
    n = n_tiles * tn
    tm = min(t, tm)
    grid = (n_tiles, t // tm)
    out_shape = jax.ShapeDtypeStruct((t, n), BF16)
    scratch = [pltpu.VMEM((d, tn), BF16)]
    if rope is None:
        return pl.pallas_call(
            _proj_plain_kernel,
            grid=grid,
            in_specs=[pl.BlockSpec((tm, d), lambda j, i: (i, 0)),
                      pl.BlockSpec((None, d, tn), lambda j, i: (idx, 0, first_tile + j))],
            out_specs=pl.BlockSpec((tm, tn), lambda j, i: (i, j)),
            out_shape=out_shape,
            scratch_shapes=scratch,
            compiler_params=_params(("arbitrary", "arbitrary")),
            name="proj_plain",
        )(h, w_stack)
    kinds, cos, sin, head_dim = rope
    per_tile = kinds.reshape(n_tiles, tn // LANES)
    kinds = jnp.concatenate([kinds, (per_tile != ROPE_NONE).any(axis=1).astype(jnp.int32)])
    return pl.pallas_call(
        functools.partial(_proj_kernel, tn=tn, head_dim=head_dim),
        grid_spec=pltpu.PrefetchScalarGridSpec(
            num_scalar_prefetch=1,
            grid=grid,
            in_specs=[pl.BlockSpec((tm, d), lambda j, i, k: (i, 0)),
                      pl.BlockSpec((None, d, tn), lambda j, i, k: (idx, 0, first_tile + j)),
                      pl.BlockSpec((tm, LANES), lambda j, i, k: (i, 0)),
                      pl.BlockSpec((tm, LANES), lambda j, i, k: (i, 0))],
            out_specs=pl.BlockSpec((tm, tn), lambda j, i, k: (i, j)),
            scratch_shapes=scratch),
        out_shape=out_shape,
        compiler_params=_params(("arbitrary", "arbitrary")),
        name=f"proj_rope{head_dim}",
    )(kinds, h, w_stack, cos, sin)


def _even_chunk_kinds():
    kinds = [ROPE_NONE] * (EVEN_IN // LANES)
    for start in (E_QA, E_KA, E_QB, E_KB):
        for c in range(start, start + A_HEADS):
            kinds[c] = ROPE_FULL
    return jnp.asarray(kinds, jnp.int32)


def _odd_chunk_kinds():
    kinds = [ROPE_NONE] * (ODD_IN // LANES)
    k_end = C_HEADS * C_HEAD_DIM + C_KV_HEADS * C_HEAD_DIM
    for c in range(len(kinds)):
        if (c + 1) * LANES <= k_end:
            kinds[c] = ROPE_FULL
        elif c * LANES < k_end:
            assert k_end - c * LANES == LANES // 2
            kinds[c] = ROPE_FIRST_HALF
    return jnp.asarray(kinds, jnp.int32)


def _locate(parts, block):
    arr, first = [pt for pt in parts if pt[1] <= block][-1]
    assert block - first < arr.shape[2] // LANES
    return arr, block - first


def _scores_t(k, q):
    return lax.dot_general(k, q, (((1,), (1,)), ((), ())), preferred_element_type=F32)


def _transpose_bf16(x):
    return x.astype(F32).T.astype(BF16)


def _softmax_probs_t(tiles, biases, weights, c):
    m = None
    for s, b in zip(tiles, biases):
        if b is None:
            mt = jnp.max(s, axis=0, keepdims=True)
        elif b.shape[0] == 1:
            mt = jnp.max(s, axis=0, keepdims=True) + b
        else:
            mt = jnp.max(s + b, axis=0, keepdims=True)
        m = mt if m is None else jnp.maximum(m, mt)
    shifts = {}
    l = None
    ps = []
    for s, b, w in zip(tiles, biases, weights):
        if b is None:
            shift = -m
        else:
            if id(b) not in shifts:
                shifts[id(b)] = b - m
            shift = shifts[id(b)]
        p = jnp.exp2((s + shift) * c)
        if w is not None:
            p = p * w
        lt = jnp.sum(p, axis=0, keepdims=True)
        l = lt if l is None else l + lt
        ps.append(p.astype(BF16))
    p_all = ps[0] if len(ps) == 1 else jnp.concatenate(ps, axis=0)
    return p_all, 1.0 / l


SCORE_LOOKAHEAD = 2
DILATED_LOOKAHEAD = 3


def _tile_delta(blk):
    row = lax.broadcasted_iota(jnp.int32, (blk, blk), 0)
    col = lax.broadcasted_iota(jnp.int32, (blk, blk), 1)
    return col - row


def _dilated_kernel(q_ref, k_ref, v_ref, g_ref, o_ref, *, blk, nblk, c, ahead):
    vt = _transpose_bf16(v_ref[...])
    cr = _tile_delta(blk)

    def near(delta):
        dist = cr + delta * blk
        cnt = (jnp.where(dist <= 128, 1.0, 0.0)
               + jnp.where(jnp.logical_and((dist & 3) == 0, dist <= 512), 1.0, 0.0)
               + jnp.where((dist & 15) == 0, 1.0, 0.0))
        cnt = jnp.where(dist >= 0, cnt, 0.0)
        return jnp.where(cnt > 0.0, 0.0, NEG), cnt

    n_near = -(-512 // blk) + 1
    nears = [near(d) for d in range(min(n_near, nblk))]
    far_bias = jnp.where((cr & 15) == 0, 0.0, NEG)

    def scores(u):
        return _scores_t(k_ref[0:(u + 1) * blk, :], q_ref[u * blk:(u + 1) * blk, :])

    sts = [scores(u) for u in range(min(ahead, nblk))]
    for i in range(nblk):
        rows = slice(i * blk, (i + 1) * blk)
        ext = (i + 1) * blk
        if i + ahead < nblk:
            sts.append(scores(i + ahead))
        st = sts[i]
        tiles, biases, weights = [], [], []
        for n in range(i + 1):
            tiles.append(st[n * blk:(n + 1) * blk])
            if i - n < len(nears):
                bias, cnt = nears[i - n]
                biases.append(bias)
                weights.append(cnt)
            else:
                biases.append(far_bias)
                weights.append(None)
        p_all, inv_l = _softmax_probs_t(tiles, biases, weights, c)
        ot = jnp.dot(vt[:, 0:ext], p_all, preferred_element_type=F32) * inv_l
        o_ref[rows, :] = (ot.T * _silu(g_ref[rows, :].astype(F32))).astype(o_ref.dtype)


def _dilated_attention(p3):
    b, s, _ = p3.shape
    blk = 256
    assert blk * (-(-512 // blk)) >= 512 and s % blk == 0
    return pl.pallas_call(
        functools.partial(_dilated_kernel, blk=blk, nblk=s // blk, c=HEAD_DIM ** -0.5 * LOG2E,
                          ahead=DILATED_LOOKAHEAD),
        grid=(b, A_HEADS),
        in_specs=[pl.BlockSpec((None, s, LANES), lambda bi, h: (bi, 0, E_QA + h)),
                  pl.BlockSpec((None, s, LANES), lambda bi, h: (bi, 0, E_KA + h)),
                  pl.BlockSpec((None, s, LANES), lambda bi, h: (bi, 0, E_VA + h)),
                  pl.BlockSpec((None, s, LANES), lambda bi, h: (bi, 0, E_GATE + h))],
        out_specs=pl.BlockSpec((None, s, LANES), lambda bi, h: (bi, 0, h)),
        out_shape=jax.ShapeDtypeStruct((b, s, A_HEADS * HEAD_DIM), BF16),
        compiler_params=_params(("arbitrary", "arbitrary")),
        name="dilated_attention",
    )(p3, p3, p3, p3)


def _moba_kernel(q_ref, k_ref, v_ref, g_ref, o_ref, *, blk, nblk, c):
    s_len = nblk * blk
    q = q_ref[...]

    def scores(u):
        return _scores_t(k_ref[0:(u + 1) * blk, :], q[u * blk:(u + 1) * blk, :])

    sts = [scores(u) for u in range(min(SCORE_LOOKAHEAD, nblk))]
    vt = _transpose_bf16(v_ref[...])

    km = jnp.concatenate(
        [jnp.sum(k_ref[n * blk:(n + 1) * blk, :].astype(F32), axis=0, keepdims=True) * (1.0 / blk)
         for n in range(nblk)], axis=0)
    hi = km.astype(BF16).astype(F32)
    ksplit = jnp.concatenate([hi, km - hi], axis=0).astype(BF16)
    gt2 = _scores_t(ksplit, q)
    gate = gt2[:nblk] + gt2[nblk:]

    nid = lax.broadcasted_iota(jnp.int32, (nblk, s_len), 0)
    own = jnp.right_shift(lax.broadcasted_iota(jnp.int32, (nblk, s_len), 1), blk.bit_length() - 1)
    rank = jnp.zeros((nblk, s_len), jnp.int32)
    for mm in range(nblk):
        gm = gate[mm:mm + 1, :]
        beats = jnp.logical_or(gm > gate, jnp.logical_and(gm == gate, mm < nid))
        rank = rank + jnp.where(jnp.logical_and(beats, mm < own), 1, 0)
    sel_bias = jnp.where(jnp.logical_and(rank < MOBA_TOPK, nid < own), 0.0, NEG)
    causal_bias = jnp.where(_tile_delta(blk) >= 0, 0.0, NEG)

    for i in range(nblk):
        rows = slice(i * blk, (i + 1) * blk)
        ext = (i + 1) * blk
        if i + SCORE_LOOKAHEAD < nblk:
            sts.append(scores(i + SCORE_LOOKAHEAD))
        st = sts[i]
        tiles = [st[n * blk:(n + 1) * blk] for n in range(i + 1)]
        biases = [sel_bias[n:n + 1, rows] for n in range(i)] + [causal_bias]
        p_all, inv_l = _softmax_probs_t(tiles, biases, [None] * (i + 1), c)
        ot = jnp.dot(vt[:, 0:ext], p_all, preferred_element_type=F32) * inv_l
        o_ref[rows, :] = (ot.T * _silu(g_ref[rows, :].astype(F32))).astype(o_ref.dtype)


def _moba_attention(p3):
    b, s, _ = p3.shape
    blk = MOBA_BLOCK
    nblk = s // blk
    assert nblk >= MOBA_TOPK and blk & (blk - 1) == 0
    return pl.pallas_call(
        functools.partial(_moba_kernel, blk=blk, nblk=nblk, c=HEAD_DIM ** -0.5 * LOG2E),
        grid=(b, B_HEADS),
        in_specs=[pl.BlockSpec((None, s, LANES), lambda bi, h: (bi, 0, E_QB + h)),
                  pl.BlockSpec((None, s, LANES), lambda bi, h: (bi, 0, E_KB + h)),
                  pl.BlockSpec((None, s, LANES), lambda bi, h: (bi, 0, E_VB + h)),
                  pl.BlockSpec((None, s, LANES), lambda bi, h: (bi, 0, E_GATE + A_HEADS + h))],
        out_specs=pl.BlockSpec((None, s, LANES), lambda bi, h: (bi, 0, h)),
        out_shape=jax.ShapeDtypeStruct((b, s, B_HEADS * HEAD_DIM), BF16),
        compiler_params=_params(("arbitrary", "arbitrary")),
        name="moba_attention",
    )(p3, p3, p3, p3)


def _mem_kernel(*refs, tq, c):
    q_refs = refs[:MEM_HEADS]
    g_refs = refs[MEM_HEADS:2 * MEM_HEADS]
    memkv_ref, o_ref = refs[2 * MEM_HEADS:]
    kvw = MEM_HEADS * HEAD_DIM
    units = [(h, j) for h in range(MEM_HEADS) for j in range(o_ref.shape[0] // tq)]

    def scores(unit):
        h, j = unit
        return _scores_t(memkv_ref[:, h * HEAD_DIM:(h + 1) * HEAD_DIM], q_refs[h][j * tq:(j + 1) * tq, :])

    sts = [scores(u) for u in units[:SCORE_LOOKAHEAD]]
    mvts = [_transpose_bf16(memkv_ref[:, kvw + h * HEAD_DIM:kvw + (h + 1) * HEAD_DIM]) for h in range(MEM_HEADS)]
    for ui, (h, j) in enumerate(units):
        rows = slice(j * tq, (j + 1) * tq)
        if ui + SCORE_LOOKAHEAD < len(units):
            sts.append(scores(units[ui + SCORE_LOOKAHEAD]))
        p_all, inv_l = _softmax_probs_t([sts[ui]], [None], [None], c)
        ot = jnp.dot(mvts[h], p_all, preferred_element_type=F32) * inv_l
        gate = g_refs[h][rows, :].astype(F32)
        o_ref[rows, h * HEAD_DIM:(h + 1) * HEAD_DIM] = (ot.T * _silu(gate)).astype(o_ref.dtype)


def _mem_attention(parts, memkv, q_block, gate_block):
    b, s, _ = parts[0][0].shape
    arrays, specs = [], []
    for first in (q_block, gate_block):
        for h in range(MEM_HEADS):
            arr, local = _locate(parts, first + h)
            arrays.append(arr)
            specs.append(pl.BlockSpec((None, s, LANES), functools.partial(lambda bi, c: (bi, 0, c), c=local)))
    width = MEM_HEADS * HEAD_DIM
    return pl.pallas_call(
        functools.partial(_mem_kernel, tq=min(s, 512), c=HEAD_DIM ** -0.5 * LOG2E),
        grid=(b,),
        in_specs=specs + [pl.BlockSpec((None, MEM_LEN, 2 * width), lambda bi: (bi, 0, 0))],
        out_specs=pl.BlockSpec((None, s, width), lambda bi: (bi, 0, 0)),
        out_shape=jax.ShapeDtypeStruct((b, s, width), BF16),
        compiler_params=_params(("arbitrary",)),
        name="memory_attention",
    )(*arrays, memkv)


SWA_QBLK = C_WINDOW
SWA_GATE_BLOCKS = (C_HEADS * C_HEAD_DIM) // LANES
SWA_VT_WIDTH = 2 * LANES


def _swa_kernel(sinks_ref, q_ref, kv_ref, *rest, tq, c, scale):
    gate_refs = rest[:SWA_GATE_BLOCKS]
    o_ref, kvt_ref = rest[SWA_GATE_BLOCKS:]
    i = pl.program_id(1)
    heads_per_group = C_HEADS // C_KV_HEADS
    kw = C_KV_HEADS * C_HEAD_DIM

    @pl.when(i == 0)
    def _():
        for ch in range(kv_ref.shape[0] // LANES):
            t = kv_ref[ch * LANES:(ch + 1) * LANES, LANES:LANES + SWA_VT_WIDTH].astype(F32)
            kvt_ref[ch] = t.T.astype(BF16)

    krow = lax.broadcasted_iota(jnp.int32, (C_WINDOW, SWA_QBLK), 0)
    qcol = lax.broadcasted_iota(jnp.int32, (C_WINDOW, SWA_QBLK), 1)
    from_prev = krow > qcol
    zero_b = jnp.zeros((C_WINDOW, SWA_QBLK), BF16)
    shift = LANES.bit_length() - 1
    units = [(sub, g) for sub in range(tq // SWA_QBLK) for g in range(C_KV_HEADS)]

    def chunks(sub):
        r0 = i * tq + sub * SWA_QBLK
        prev = pl.multiple_of(jnp.maximum(r0 - C_WINDOW, 0), LANES)
        own = pl.multiple_of(r0, LANES)
        return r0, prev, own

    def scores(unit):
        sub, g = unit
        r0, prev, own = chunks(sub)
        rows = slice(sub * SWA_QBLK, (sub + 1) * SWA_QBLK)
        kcols = slice(g * C_HEAD_DIM, (g + 1) * C_HEAD_DIM)
        qg = jnp.concatenate([q_ref[rows, hd * C_HEAD_DIM:(hd + 1) * C_HEAD_DIM]
                              for hd in range(g * heads_per_group, (g + 1) * heads_per_group)], axis=0)
        s_prev = _scores_t(kv_ref[pl.ds(prev, C_WINDOW), kcols], qg)
        if sub == 0:
            s_prev = s_prev + jnp.where(r0 > 0, 0.0, NEG)
        return s_prev, _scores_t(kv_ref[pl.ds(own, C_WINDOW), kcols], qg)

    sts = [scores(u) for u in units[:SCORE_LOOKAHEAD]]
    for ui, (sub, g) in enumerate(units):
        if ui + SCORE_LOOKAHEAD < len(units):
            sts.append(scores(units[ui + SCORE_LOOKAHEAD]))
        s_prev, s_own = sts[ui]
        _, prev, own = chunks(sub)
        rows = slice(sub * SWA_QBLK, (sub + 1) * SWA_QBLK)
        v0 = (kw + g * C_HEAD_DIM) - LANES
        vt_prev = kvt_ref[jnp.right_shift(prev, shift)][v0:v0 + C_HEAD_DIM, :]
        vt_own = kvt_ref[jnp.right_shift(own, shift)][v0:v0 + C_HEAD_DIM, :]
        p_prev, p_own, inv_l = [], [], []
        for u in range(heads_per_group):
            cols = slice(u * SWA_QBLK, (u + 1) * SWA_QBLK)
            s_h = jnp.where(from_prev, s_prev[:, cols], s_own[:, cols])
            sink = sinks_ref[g * heads_per_group + u] * (1.0 / scale)
            m = jnp.maximum(jnp.max(s_h, axis=0, keepdims=True), sink)
            p = jnp.exp2((s_h - m) * c)
            inv_l.append(1.0 / (jnp.sum(p, axis=0, keepdims=True) + jnp.exp2((sink - m) * c)))
            pb = p.astype(BF16)
            p_prev.append(jnp.where(from_prev, pb, zero_b))
            p_own.append(jnp.where(from_prev, zero_b, pb))
        ot = (jnp.dot(vt_prev, jnp.concatenate(p_prev, axis=1), preferred_element_type=F32)
              + jnp.dot(vt_own, jnp.concatenate(p_own, axis=1), preferred_element_type=F32))
        ot = ot * jnp.concatenate(inv_l, axis=1)
        for pair in range(heads_per_group // 2):
            both = jnp.concatenate([ot[:, (2 * pair + u) * SWA_QBLK:(2 * pair + u + 1) * SWA_QBLK]
                                    for u in range(2)], axis=0)
            blk_i = (g * heads_per_group) // 2 + pair
            gate = gate_refs[blk_i][rows, :].astype(F32)
            o_ref[rows, blk_i * LANES:(blk_i + 1) * LANES] = (both.T * _silu(gate)).astype(o_ref.dtype)


def _swa_attention(parts, sinks):
    p_qkv = parts[0][0]
    b, s, _ = p_qkv.shape
    tq = min(s, 512)
    qw = C_HEADS * C_HEAD_DIM
    kvw = 2 * C_KV_HEADS * C_HEAD_DIM
    assert (O_KV * LANES) % kvw == 0 and kvw == LANES + SWA_VT_WIDTH and p_qkv.shape[2] >= qw + kvw
    gate_arrays, gate_specs = [], []
    for u in range(SWA_GATE_BLOCKS):
        arr, local = _locate(parts, O_GATE + u)
        gate_arrays.append(arr)
        gate_specs.append(pl.BlockSpec((None, tq, LANES),
                                       functools.partial(lambda bi, i, sk, c: (bi, i, c), c=local)))
    scale = C_HEAD_DIM ** -0.5
    return pl.pallas_call(
        functools.partial(_swa_kernel, tq=tq, c=scale * LOG2E, scale=scale),
        grid_spec=pltpu.PrefetchScalarGridSpec(
            num_scalar_prefetch=1,
            grid=(b, s // tq),
            in_specs=[pl.BlockSpec((None, tq, qw), lambda bi, i, sk: (bi, i, 0)),
                      pl.BlockSpec((None, s, kvw), lambda bi, i, sk: (bi, 0, (O_KV * LANES) // kvw))]
                     + gate_specs,
            out_specs=pl.BlockSpec((None, tq, qw), lambda bi, i, sk: (bi, i, 0)),
            scratch_shapes=[pltpu.VMEM((s // LANES, SWA_VT_WIDTH, LANES), BF16)]),
        out_shape=jax.ShapeDtypeStruct((b, s, qw), BF16),
        compiler_params=_params(("arbitrary", "arbitrary")),
        name="swa_sink_attention",
    )(sinks, p_qkv, p_qkv, *gate_arrays)


def _cast_kernel(w_ref, o_ref):
    o_ref[...] = w_ref[...].astype(o_ref.dtype)


def _cast_bf16(w_stack):
    nl, k, n = w_stack.shape
    tk = min(k, 512)
    return pl.pallas_call(
        _cast_kernel,
        grid=(nl, k // tk),
        in_specs=[pl.BlockSpec((None, tk, n), lambda l, i: (l, i, 0))],
        out_specs=pl.BlockSpec((None, tk, n), lambda l, i: (l, i, 0)),
        out_shape=jax.ShapeDtypeStruct(w_stack.shape, BF16),
        compiler_params=_params(("arbitrary", "arbitrary")),
        name="cast_bf16",
    )(w_stack)


def _outproj_kernel(*refs, widths, tn, emit_x):
    ny = len(widths)
    y_refs = refs[:ny]
    w_ref, x_ref, g_ref = refs[ny:ny + 3]
    ho_ref = refs[-1]
    hold_ref = refs[ny + 3] if emit_x else ho_ref
    tm, d = x_ref.shape
    ss = jnp.zeros((tm, 1), F32)
    for j in range(d // tn):
        cols = slice(j * tn, (j + 1) * tn)
        acc = x_ref[:, cols]
        off = 0
        for y_ref, wd in zip(y_refs, widths):
            acc = acc + jnp.dot(y_ref[...], w_ref[off:off + wd, cols], preferred_element_type=F32)
            off += wd
        hold_ref[:, cols] = acc
        ss = ss + jnp.sum(acc * acc, axis=1, keepdims=True)
    r = lax.rsqrt(ss * (1.0 / d) + EPS)
    for j in range(d // tn):
        cols = slice(j * tn, (j + 1) * tn)
        ho_ref[:, cols] = ((hold_ref[:, cols] * r) * g_ref[:, cols]).astype(ho_ref.dtype)


def _outproj(ys, w_bf16, idx, x2d, g_next, final):
    t, d = x2d.shape
    widths = tuple(y.shape[1] for y in ys)
    kdim = w_bf16.shape[1]
    assert sum(widths) == kdim and w_bf16.shape[2] == d
    tm = min(t, 512)
    tn = min(d, 512)
    row_spec = pl.BlockSpec((tm, d), lambda i: (i, 0))
    in_specs = [pl.BlockSpec((tm, wd), lambda i: (i, 0)) for wd in widths]
    in_specs += [pl.BlockSpec((None, kdim, d), lambda i: (idx, 0, 0)), row_spec,
                 pl.BlockSpec((1, d), lambda i: (0, 0))]
    if final:
        out_specs = [row_spec]
        out_shape = [jax.ShapeDtypeStruct((t, d), F32)]
    else:
        out_specs = [row_spec, row_spec]
        out_shape = [jax.ShapeDtypeStruct((t, d), F32), jax.ShapeDtypeStruct((t, d), BF16)]
    res = pl.pallas_call(
        functools.partial(_outproj_kernel, widths=widths, tn=tn, emit_x=not final),
        grid=(t // tm,),
        in_specs=in_specs,
        out_specs=out_specs,
        out_shape=out_shape,
        compiler_params=_params(("arbitrary",)),
        name="outproj_final" if final else "outproj",
    )(*ys, w_bf16, x2d, g_next.reshape(1, d))
    return res[0] if final else (res[0], res[1])


def kernel(x, mem, positions, even_norm, even_w_in, even_w_mem_kv, even_w_out, odd_norm, odd_w_in,
           odd_w_mem_kv, odd_w_out, odd_sinks, mem_norm, final_norm):
    b, s, d = x.shape
    t = b * s
    depth = even_norm.shape[0] + odd_norm.shape[0]
    pos_col = positions.reshape(t, 1)
    cos_e, sin_e, cos_o, sin_o = _rope_tables(pos_col)
    kinds_e = _even_chunk_kinds()
    kinds_o = _odd_chunk_kinds()
    mem_n = _rmsnorm(mem.reshape(b * MEM_LEN, d), mem_norm)

    w_out_even = _cast_bf16(even_w_out)
    w_out_odd = _cast_bf16(odd_w_out)

    x2d = x.reshape(t, d)
    h = _rmsnorm(x2d, even_norm[0])
    out = None
    for layer in range(depth):
        idx = layer // 2
        last = layer == depth - 1
        if last:
            g_next = final_norm
        elif layer % 2 == 0:
            g_next = odd_norm[idx]
        else:
            g_next = even_norm[idx + 1]
        if layer % 2 == 0:
            memkv = _proj(mem_n, even_w_mem_kv, idx, 512).reshape(b, MEM_LEN, 2 * MEM_HEADS * HEAD_DIM)
            p3 = _proj(h, even_w_in, idx, 1024, (kinds_e, cos_e, sin_e, HEAD_DIM)).reshape(b, s, EVEN_IN)
            ys = [_dilated_attention(p3), _moba_attention(p3),
                  _mem_attention([(p3, 0)], memkv, E_QM, E_GATE + 2 * A_HEADS)]
            w_out = w_out_even
        else:
            memkv = _proj(mem_n, odd_w_mem_kv, idx, 512).reshape(b, MEM_LEN, 2 * MEM_HEADS * HEAD_DIM)
            wide = ODD_SPLIT_TILES * ODD_TN
            pa = _proj(h, odd_w_in, idx, wide // 2, (kinds_o[:wide // LANES], cos_o, sin_o, C_HEAD_DIM),
                       n_tiles=2, tm=512).reshape(b, s, wide)
            pb = _proj(h, odd_w_in, idx, ODD_TN, first_tile=ODD_SPLIT_TILES).reshape(b, s, ODD_IN - wide)
            parts = [(pa, 0), (pb, wide // LANES)]
            ys = [_swa_attention(parts, odd_sinks[idx]), _mem_attention(parts, memkv, O_QM, O_GATE + SWA_GATE_BLOCKS)]
            w_out = w_out_odd
        ys = [y.reshape(t, y.shape[-1]) for y in ys]
        if last:
            out = _outproj(ys, w_out, idx, x2d, g_next, final=True)
        else:
            x2d, h = _outproj(ys, w_out, idx, x2d, g_next, final=False)
    return out.reshape(b, s, d)
```

```python
import functools
import math

import jax
import jax.numpy as jnp
from jax import lax
from jax.experimental import pallas as pl
from jax.experimental.pallas import tpu as pltpu

F32 = jnp.float32
BF16 = jnp.bfloat16

LANES = 128
HEAD_DIM = 128
MEM_LEN = 256
MEM_HEADS = 4
A_HEADS = 6
B_HEADS = 6
MOBA_BLOCK = 256
MOBA_TOPK = 3
C_HEADS = 24
C_KV_HEADS = 3
C_HEAD_DIM = 64
C_WINDOW = 128
ROPE_THETA = 10000.0
EPS = 1e-6
NEG = -1e30
LOG2E = 1.4426950408889634

EVEN_WIDTH = (A_HEADS + B_HEADS + MEM_HEADS) * HEAD_DIM
ODD_WIDTH = C_HEADS * C_HEAD_DIM + MEM_HEADS * HEAD_DIM
EVEN_IN = 3 * A_HEADS * HEAD_DIM + 3 * B_HEADS * HEAD_DIM + MEM_HEADS * HEAD_DIM + EVEN_WIDTH
ODD_IN = C_HEADS * C_HEAD_DIM + 2 * C_KV_HEADS * C_HEAD_DIM + MEM_HEADS * HEAD_DIM + ODD_WIDTH

E_QA, E_KA, E_VA = 0, A_HEADS, 2 * A_HEADS
E_QB, E_KB, E_VB = 3 * A_HEADS, 3 * A_HEADS + B_HEADS, 3 * A_HEADS + 2 * B_HEADS
E_QM = 3 * A_HEADS + 3 * B_HEADS
E_GATE = E_QM + MEM_HEADS
O_Q = 0
O_KV = (C_HEADS * C_HEAD_DIM) // LANES
O_QM = O_KV + (2 * C_KV_HEADS * C_HEAD_DIM) // LANES
O_GATE = O_QM + MEM_HEADS

PROJ_TM = 1024
EVEN_TN = 1024
MEMKV_TN = 512
ODD_TN = 896
ODD_SPLIT_TILES = 4
ODD_WIDE_TM = 512
OUT_TM, OUT_TN = 512, 512
NORM_TM = 512
ROPE_TM = 1024
CAST_TK = 512
ATTN_BLK = 256
MEM_TQ = 512
SWA_TQ = 1024

VMEM_LIMIT = 56 * 1024 * 1024


def _params(sem):
    return pltpu.CompilerParams(dimension_semantics=sem, vmem_limit_bytes=VMEM_LIMIT)


def _silu(g):
    half = 0.5 * g
    return half + half * jnp.tanh(half)


def _rope_table_kernel(pos_ref, invf_ref, c_full_ref, s_full_ref, c_half_ref, s_half_ref):
    ang = pos_ref[...].astype(F32) * invf_ref[...]
    cos = jnp.cos(ang)
    sin = jnp.sin(ang)
    lane = lax.broadcasted_iota(jnp.int32, cos.shape, 1)
    half, quarter = LANES // 2, LANES // 4

    def spread(t, sign):
        swapped = pltpu.roll(t, half, 1)
        full = jnp.where(lane < half, sign * t, swapped)
        narrow = jnp.where(lane < quarter, sign * swapped,
                           jnp.where(lane < half, pltpu.roll(t, LANES - quarter, 1),
                                     jnp.where(lane < half + quarter, sign * t, pltpu.roll(t, quarter, 1))))
        return full, narrow

    c_full_ref[...], c_half_ref[...] = spread(cos, 1.0)
    s_full_ref[...], s_half_ref[...] = spread(sin, -1.0)


def _rope_tables(pos_col):
    t = pos_col.shape[0]

    def inv_freq(head_dim):
        return jnp.exp(jnp.arange(head_dim // 2, dtype=F32) * (-2.0 * math.log(ROPE_THETA) / head_dim))

    invf = jnp.concatenate([inv_freq(HEAD_DIM), inv_freq(C_HEAD_DIM),
                            jnp.zeros((LANES - HEAD_DIM // 2 - C_HEAD_DIM // 2,), F32)])[None, :]
    tm = min(t, ROPE_TM)
    tab = pl.BlockSpec((tm, LANES), lambda i: (i, 0))
    return pl.pallas_call(
        _rope_table_kernel,
        grid=(t // tm,),
        in_specs=[pl.BlockSpec((tm, 1), lambda i: (i, 0)),
                  pl.BlockSpec((1, LANES), lambda i: (0, 0))],
        out_specs=[tab] * 4,
        out_shape=[jax.ShapeDtypeStruct((t, LANES), F32)] * 4,
        compiler_params=_params(("arbitrary",)),
        name="rope_tables",
    )(pos_col, invf)


def _rmsnorm_kernel(x_ref, g_ref, o_ref):
    x = x_ref[...]
    ms = jnp.mean(x * x, axis=-1, keepdims=True)
    o_ref[...] = ((x * lax.rsqrt(ms + EPS)) * g_ref[...]).astype(o_ref.dtype)


def _rmsnorm(x2d, g):
    t, d = x2d.shape
    tm = min(t, NORM_TM)
    return pl.pallas_call(
        _rmsnorm_kernel,
        grid=(t // tm,),
        in_specs=[pl.BlockSpec((tm, d), lambda i: (i, 0)),
                  pl.BlockSpec((1, d), lambda i: (0, 0))],
        out_specs=pl.BlockSpec((tm, d), lambda i: (i, 0)),
        out_shape=jax.ShapeDtypeStruct((t, d), BF16),
        compiler_params=_params(("arbitrary",)),
        name="rmsnorm",
    )(x2d, g.reshape(1, d))


ROPE_NONE, ROPE_FULL, ROPE_FIRST_HALF = 0, 1, 2


def _rope_partner(x, head_dim):
    if head_dim == LANES:
        return pltpu.roll(x, LANES // 2, 1)
    lane = lax.broadcasted_iota(jnp.int32, x.shape, 1)
    half = head_dim // 2
    return jnp.where((lane % head_dim) < half, pltpu.roll(x, LANES - half, 1), pltpu.roll(x, half, 1))


def _proj_kernel(types_ref, h_ref, w_ref, cos_ref, sin_ref, o_ref, wbf_ref, *, tn, head_dim):
    j = pl.program_id(0)
    i = pl.program_id(1)

    @pl.when(i == 0)
    def _():
        wbf_ref[...] = w_ref[...].astype(BF16)

    nchunk = tn // LANES
    any_rope = types_ref[pl.num_programs(0) * nchunk + j]

    @pl.when(any_rope == 0)
    def _():
        o_ref[...] = jnp.dot(h_ref[...], wbf_ref[...], preferred_element_type=F32).astype(o_ref.dtype)

    @pl.when(any_rope != 0)
    def _():
        acc = jnp.dot(h_ref[...], wbf_ref[...], preferred_element_type=F32)
        cos = cos_ref[...]
        sin = sin_ref[...]
        upper = lax.broadcasted_iota(jnp.int32, cos.shape, 1) >= LANES // 2
        for c in range(nchunk):
            kind = types_ref[j * nchunk + c]
            x = acc[:, c * LANES:(c + 1) * LANES]
            roped = x * cos + _rope_partner(x, head_dim) * sin
            plain = jnp.logical_or(kind == ROPE_NONE, jnp.logical_and(kind == ROPE_FIRST_HALF, upper))
            o_ref[:, c * LANES:(c + 1) * LANES] = jnp.where(plain, x, roped).astype(o_ref.dtype)


def _proj_plain_kernel(h_ref, w_ref, o_ref, wbf_ref):
    @pl.when(pl.program_id(1) == 0)
    def _():
        wbf_ref[...] = w_ref[...].astype(BF16)

    o_ref[...] = jnp.dot(h_ref[...], wbf_ref[...], preferred_element_type=F32).astype(o_ref.dtype)


def _proj(h, w_stack, idx, tn, rope=None, first_tile=0, n_tiles=None, tm=PROJ_TM):
    t, d = h.shape
    if n_tiles is None:
        n_tiles = w_stack.shape[2] // tn - first_tile
    assert tn % LANES == 0 and (first_tile + n_tiles) * tn <= w_stack.shape[2]
    n = n_tiles * tn
    tm = min(t, tm)
    grid = (n_tiles, t // tm)
    out_shape = jax.ShapeDtypeStruct((t, n), BF16)
    scratch = [pltpu.VMEM((d, tn), BF16)]
    if rope is None:
        return pl.pallas_call(
            _proj_plain_kernel,
            grid=grid,
            in_specs=[pl.BlockSpec((tm, d), lambda j, i: (i, 0)),
                      pl.BlockSpec((None, d, tn), lambda j, i: (idx, 0, first_tile + j))],
            out_specs=pl.BlockSpec((tm, tn), lambda j, i: (i, j)),
            out_shape=out_shape,
            scratch_shapes=scratch,
            compiler_params=_params(("arbitrary", "arbitrary")),
            name="proj_plain",
        )(h, w_stack)
    kinds, cos, sin, head_dim = rope
    per_tile = kinds.reshape(n_tiles, tn // LANES)
    kinds = jnp.concatenate([kinds, (per_tile != ROPE_NONE).any(axis=1).astype(jnp.int32)])
    return pl.pallas_call(
        functools.partial(_proj_kernel, tn=tn, head_dim=head_dim),
        grid_spec=pltpu.PrefetchScalarGridSpec(
            num_scalar_prefetch=1,
            grid=grid,
            in_specs=[pl.BlockSpec((tm, d), lambda j, i, k: (i, 0)),
                      pl.BlockSpec((None, d, tn), lambda j, i, k: (idx, 0, first_tile + j)),
                      pl.BlockSpec((tm, LANES), lambda j, i, k: (i, 0)),
                      pl.BlockSpec((tm, LANES), lambda j, i, k: (i, 0))],
            out_specs=pl.BlockSpec((tm, tn), lambda j, i, k: (i, j)),
            scratch_shapes=scratch),
        out_shape=out_shape,
        compiler_params=_params(("arbitrary", "arbitrary")),
        name=f"proj_rope{head_dim}",
    )(kinds, h, w_stack, cos, sin)


def _even_chunk_kinds():
    kinds = [ROPE_NONE] * (EVEN_IN // LANES)
    for start in (E_QA, E_KA, E_QB, E_KB):
        for c in range(start, start + A_HEADS):
            kinds[c] = ROPE_FULL
    return jnp.asarray(kinds, jnp.int32)


def _odd_chunk_kinds():
    kinds = [ROPE_NONE] * (ODD_IN // LANES)
    k_end = C_HEADS * C_HEAD_DIM + C_KV_HEADS * C_HEAD_DIM
    for c in range(len(kinds)):
        if (c + 1) * LANES <= k_end:
            kinds[c] = ROPE_FULL
        elif c * LANES < k_end:
            assert k_end - c * LANES == LANES // 2
            kinds[c] = ROPE_FIRST_HALF
    return jnp.asarray(kinds, jnp.int32)


def _locate(parts, block):
    arr, first = [pt for pt in parts if pt[1] <= block][-1]
    assert block - first < arr.shape[2] // LANES
    return arr, block - first


def _scores_t(k, q):
    return lax.dot_general(k, q, (((1,), (1,)), ((), ())), preferred_element_type=F32)


def _transpose_bf16(x):
    return x.astype(F32).T.astype(BF16)


def _softmax_probs_t(tiles, biases, weights, c):
    m = None
    for s, b in zip(tiles, biases):
        if b is None:
            mt = jnp.max(s, axis=0, keepdims=True)
        elif b.shape[0] == 1:
            mt = jnp.max(s, axis=0, keepdims=True) + b
        else:
            mt = jnp.max(s + b, axis=0, keepdims=True)
        m = mt if m is None else jnp.maximum(m, mt)
    shifts = {}
    l = None
    ps = []
    for s, b, w in zip(tiles, biases, weights):
        if b is None:
            shift = -m
        else:
            if id(b) not in shifts:
                shifts[id(b)] = b - m
            shift = shifts[id(b)]
        p = jnp.exp2((s + shift) * c)
        if w is not None:
            p = p * w
        lt = jnp.sum(p, axis=0, keepdims=True)
        l = lt if l is None else l + lt
        ps.append(p.astype(BF16))
    p_all = ps[0] if len(ps) == 1 else jnp.concatenate(ps, axis=0)
    return p_all, 1.0 / l


SCORE_LOOKAHEAD = 2
DILATED_LOOKAHEAD = 3


def _tile_delta(blk):
    row = lax.broadcasted_iota(jnp.int32, (blk, blk), 0)
    col = lax.broadcasted_iota(jnp.int32, (blk, blk), 1)
    return col - row


def _dilated_kernel(q_ref, k_ref, v_ref, g_ref, o_ref, *, blk, nblk, c, ahead):
    vt = _transpose_bf16(v_ref[...])
    cr = _tile_delta(blk)

    def near(delta):
        dist = cr + delta * blk
        cnt = (jnp.where(dist <= 128, 1.0, 0.0)
               + jnp.where(jnp.logical_and((dist & 3) == 0, dist <= 512), 1.0, 0.0)
               + jnp.where((dist & 15) == 0, 1.0, 0.0))
        cnt = jnp.where(dist >= 0, cnt, 0.0)
        return jnp.where(cnt > 0.0, 0.0, NEG), cnt

    n_near = -(-512 // blk) + 1
    nears = [near(d) for d in range(min(n_near, nblk))]
    far_bias = jnp.where((cr & 15) == 0, 0.0, NEG)

    def scores(u):
        return _scores_t(k_ref[0:(u + 1) * blk, :], q_ref[u * blk:(u + 1) * blk, :])

    sts = [scores(u) for u in range(min(ahead, nblk))]
    for i in range(nblk):
        rows = slice(i * blk, (i + 1) * blk)
        ext = (i + 1) * blk
        if i + ahead < nblk:
            sts.append(scores(i + ahead))
        st = sts[i]
        tiles, biases, weights = [], [], []
        for n in range(i + 1):
            tiles.append(st[n * blk:(n + 1) * blk])
            if i - n < len(nears):
                bias, cnt = nears[i - n]
                biases.append(bias)
                weights.append(cnt)
            else:
                biases.append(far_bias)
                weights.append(None)
        p_all, inv_l = _softmax_probs_t(tiles, biases, weights, c)
        ot = jnp.dot(vt[:, 0:ext], p_all, preferred_element_type=F32) * inv_l
        o_ref[rows, :] = (ot.T * _silu(g_ref[rows, :].astype(F32))).astype(o_ref.dtype)


def _dilated_attention(p3):
    b, s, _ = p3.shape
    blk = ATTN_BLK
    assert s % blk == 0
    return pl.pallas_call(
        functools.partial(_dilated_kernel, blk=blk, nblk=s // blk, c=HEAD_DIM ** -0.5 * LOG2E,
                          ahead=DILATED_LOOKAHEAD),
        grid=(b, A_HEADS),
        in_specs=[pl.BlockSpec((None, s, LANES), lambda bi, h: (bi, 0, E_QA + h)),
                  pl.BlockSpec((None, s, LANES), lambda bi, h: (bi, 0, E_KA + h)),
                  pl.BlockSpec((None, s, LANES), lambda bi, h: (bi, 0, E_VA + h)),
                  pl.BlockSpec((None, s, LANES), lambda bi, h: (bi, 0, E_GATE + h))],
        out_specs=pl.BlockSpec((None, s, LANES), lambda bi, h: (bi, 0, h)),
        out_shape=jax.ShapeDtypeStruct((b, s, A_HEADS * HEAD_DIM), BF16),
        compiler_params=_params(("arbitrary", "arbitrary")),
        name="dilated_attention",
    )(p3, p3, p3, p3)


def _moba_kernel(q_ref, k_ref, v_ref, g_ref, o_ref, *, blk, nblk, c):
    s_len = nblk * blk
    q = q_ref[...]

    def scores(u):
        return _scores_t(k_ref[0:(u + 1) * blk, :], q[u * blk:(u + 1) * blk, :])

    sts = [scores(u) for u in range(min(SCORE_LOOKAHEAD, nblk))]
    vt = _transpose_bf16(v_ref[...])

    km = jnp.concatenate(
        [jnp.sum(k_ref[n * blk:(n + 1) * blk, :].astype(F32), axis=0, keepdims=True) * (1.0 / blk)
         for n in range(nblk)], axis=0)
    hi = km.astype(BF16).astype(F32)
    ksplit = jnp.concatenate([hi, km - hi], axis=0).astype(BF16)
    gt2 = _scores_t(ksplit, q)
    gate = gt2[:nblk] + gt2[nblk:]

    nid = lax.broadcasted_iota(jnp.int32, (nblk, s_len), 0)
    own = jnp.right_shift(lax.broadcasted_iota(jnp.int32, (nblk, s_len), 1), blk.bit_length() - 1)
    rank = jnp.zeros((nblk, s_len), jnp.int32)
    for mm in range(nblk):
        gm = gate[mm:mm + 1, :]
        beats = jnp.logical_or(gm > gate, jnp.logical_and(gm == gate, mm < nid))
        rank = rank + jnp.where(jnp.logical_and(beats, mm < own), 1, 0)
    sel_bias = jnp.where(jnp.logical_and(rank < MOBA_TOPK, nid < own), 0.0, NEG)
    causal_bias = jnp.where(_tile_delta(blk) >= 0, 0.0, NEG)

    for i in range(nblk):
        rows = slice(i * blk, (i + 1) * blk)
        ext = (i + 1) * blk
        if i + SCORE_LOOKAHEAD < nblk:
            sts.append(scores(i + SCORE_LOOKAHEAD))
        st = sts[i]
        tiles = [st[n * blk:(n + 1) * blk] for n in range(i + 1)]
        biases = [sel_bias[n:n + 1, rows] for n in range(i)] + [causal_bias]
        p_all, inv_l = _softmax_probs_t(tiles, biases, [None] * (i + 1), c)
        ot = jnp.dot(vt[:, 0:ext], p_all, preferred_element_type=F32) * inv_l
        o_ref[rows, :] = (ot.T * _silu(g_ref[rows, :].astype(F32))).astype(o_ref.dtype)


def _moba_attention(p3):
    b, s, _ = p3.shape
    blk = MOBA_BLOCK
    nblk = s // blk
    assert nblk >= MOBA_TOPK and blk & (blk - 1) == 0
    return pl.pallas_call(
        functools.partial(_moba_kernel, blk=blk, nblk=nblk, c=HEAD_DIM ** -0.5 * LOG2E),
        grid=(b, B_HEADS),
        in_specs=[pl.BlockSpec((None, s, LANES), lambda bi, h: (bi, 0, E_QB + h)),
                  pl.BlockSpec((None, s, LANES), lambda bi, h: (bi, 0, E_KB + h)),
                  pl.BlockSpec((None, s, LANES), lambda bi, h: (bi, 0, E_VB + h)),
                  pl.BlockSpec((None, s, LANES), lambda bi, h: (bi, 0, E_GATE + A_HEADS + h))],
        out_specs=pl.BlockSpec((None, s, LANES), lambda bi, h: (bi, 0, h)),
        out_shape=jax.ShapeDtypeStruct((b, s, B_HEADS * HEAD_DIM), BF16),
        compiler_params=_params(("arbitrary", "arbitrary")),
        name="moba_attention",
    )(p3, p3, p3, p3)


def _mem_kernel(*refs, tq, c):
    q_refs = refs[:MEM_HEADS]
    g_refs = refs[MEM_HEADS:2 * MEM_HEADS]
    memkv_ref, o_ref = refs[2 * MEM_HEADS:]
    kvw = MEM_HEADS * HEAD_DIM
    units = [(h, j) for h in range(MEM_HEADS) for j in range(o_ref.shape[0] // tq)]

    def scores(unit):
        h, j = unit
        return _scores_t(memkv_ref[:, h * HEAD_DIM:(h + 1) * HEAD_DIM], q_refs[h][j * tq:(j + 1) * tq, :])

    sts = [scores(u) for u in units[:SCORE_LOOKAHEAD]]
    mvts = [_transpose_bf16(memkv_ref[:, kvw + h * HEAD_DIM:kvw + (h + 1) * HEAD_DIM]) for h in range(MEM_HEADS)]
    for ui, (h, j) in enumerate(units):
        rows = slice(j * tq, (j + 1) * tq)
        if ui + SCORE_LOOKAHEAD < len(units):
            sts.append(scores(units[ui + SCORE_LOOKAHEAD]))
        p_all, inv_l = _softmax_probs_t([sts[ui]], [None], [None], c)
        ot = jnp.dot(mvts[h], p_all, preferred_element_type=F32) * inv_l
        gate = g_refs[h][rows, :].astype(F32)
        o_ref[rows, h * HEAD_DIM:(h + 1) * HEAD_DIM] = (ot.T * _silu(gate)).astype(o_ref.dtype)


def _mem_attention(parts, memkv, q_block, gate_block):
    b, s, _ = parts[0][0].shape
    arrays, specs = [], []
    for first in (q_block, gate_block):
        for h in range(MEM_HEADS):
            arr, local = _locate(parts, first + h)
            arrays.append(arr)
            specs.append(pl.BlockSpec((None, s, LANES), functools.partial(lambda bi, c: (bi, 0, c), c=local)))
    width = MEM_HEADS * HEAD_DIM
    return pl.pallas_call(
        functools.partial(_mem_kernel, tq=min(s, MEM_TQ), c=HEAD_DIM ** -0.5 * LOG2E),
        grid=(b,),
        in_specs=specs + [pl.BlockSpec((None, MEM_LEN, 2 * width), lambda bi: (bi, 0, 0))],
        out_specs=pl.BlockSpec((None, s, width), lambda bi: (bi, 0, 0)),
        out_shape=jax.ShapeDtypeStruct((b, s, width), BF16),
        compiler_params=_params(("arbitrary",)),
        name="memory_attention",
    )(*arrays, memkv)


SWA_QBLK = C_WINDOW
SWA_GATE_BLOCKS = (C_HEADS * C_HEAD_DIM) // LANES
SWA_VT_WIDTH = 2 * LANES


def _swa_kernel(sinks_ref, q_ref, kv_ref, *rest, tq, c, scale):
    gate_refs = rest[:SWA_GATE_BLOCKS]
    o_ref, kvt_ref = rest[SWA_GATE_BLOCKS:]
    i = pl.program_id(1)
    heads_per_group = C_HEADS // C_KV_HEADS
    kw = C_KV_HEADS * C_HEAD_DIM

    @pl.when(i == 0)
    def _():
        for ch in range(kv_ref.shape[0] // LANES):
            t = kv_ref[ch * LANES:(ch + 1) * LANES, LANES:LANES + SWA_VT_WIDTH].astype(F32)
            kvt_ref[ch] = t.T.astype(BF16)

    krow = lax.broadcasted_iota(jnp.int32, (C_WINDOW, SWA_QBLK), 0)
    qcol = lax.broadcasted_iota(jnp.int32, (C_WINDOW, SWA_QBLK), 1)
    from_prev = krow > qcol
    zero_b = jnp.zeros((C_WINDOW, SWA_QBLK), BF16)
    shift = LANES.bit_length() - 1
    units = [(sub, g) for sub in range(tq // SWA_QBLK) for g in range(C_KV_HEADS)]

    def chunks(sub):
        r0 = i * tq + sub * SWA_QBLK
        prev = pl.multiple_of(jnp.maximum(r0 - C_WINDOW, 0), LANES)
        own = pl.multiple_of(r0, LANES)
        return r0, prev, own

    def scores(unit):
        sub, g = unit
        r0, prev, own = chunks(sub)
        rows = slice(sub * SWA_QBLK, (sub + 1) * SWA_QBLK)
        kcols = slice(g * C_HEAD_DIM, (g + 1) * C_HEAD_DIM)
        qg = jnp.concatenate([q_ref[rows, hd * C_HEAD_DIM:(hd + 1) * C_HEAD_DIM]
                              for hd in range(g * heads_per_group, (g + 1) * heads_per_group)], axis=0)
        s_prev = _scores_t(kv_ref[pl.ds(prev, C_WINDOW), kcols], qg)
        if sub == 0:
            s_prev = s_prev + jnp.where(r0 > 0, 0.0, NEG)
        return s_prev, _scores_t(kv_ref[pl.ds(own, C_WINDOW), kcols], qg)

    sts = [scores(u) for u in units[:SCORE_LOOKAHEAD]]
    for ui, (sub, g) in enumerate(units):
        if ui + SCORE_LOOKAHEAD < len(units):
            sts.append(scores(units[ui + SCORE_LOOKAHEAD]))
        s_prev, s_own = sts[ui]
        _, prev, own = chunks(sub)
        rows = slice(sub * SWA_QBLK, (sub + 1) * SWA_QBLK)
        v0 = (kw + g * C_HEAD_DIM) - LANES
        vt_prev = kvt_ref[jnp.right_shift(prev, shift)][v0:v0 + C_HEAD_DIM, :]
        vt_own = kvt_ref[jnp.right_shift(own, shift)][v0:v0 + C_HEAD_DIM, :]
        p_prev, p_own, inv_l = [], [], []
        for u in range(heads_per_group):
            cols = slice(u * SWA_QBLK, (u + 1) * SWA_QBLK)
            s_h = jnp.where(from_prev, s_prev[:, cols], s_own[:, cols])
            sink = sinks_ref[g * heads_per_group + u] * (1.0 / scale)
            m = jnp.maximum(jnp.max(s_h, axis=0, keepdims=True), sink)
            p = jnp.exp2((s_h - m) * c)
            inv_l.append(1.0 / (jnp.sum(p, axis=0, keepdims=True) + jnp.exp2((sink - m) * c)))
            pb = p.astype(BF16)
            p_prev.append(jnp.where(from_prev, pb, zero_b))
            p_own.append(jnp.where(from_prev, zero_b, pb))
        ot = (jnp.dot(vt_prev, jnp.concatenate(p_prev, axis=1), preferred_element_type=F32)
              + jnp.dot(vt_own, jnp.concatenate(p_own, axis=1), preferred_element_type=F32))
        ot = ot * jnp.concatenate(inv_l, axis=1)
        for pair in range(heads_per_group // 2):
            both = jnp.concatenate([ot[:, (2 * pair + u) * SWA_QBLK:(2 * pair + u + 1) * SWA_QBLK]
                                    for u in range(2)], axis=0)
            blk_i = (g * heads_per_group) // 2 + pair
            gate = gate_refs[blk_i][rows, :].astype(F32)
            o_ref[rows, blk_i * LANES:(blk_i + 1) * LANES] = (both.T * _silu(gate)).astype(o_ref.dtype)


def _swa_attention(parts, sinks):
    p_qkv = parts[0][0]
    b, s, _ = p_qkv.shape
    tq = min(s, SWA_TQ)
    qw = C_HEADS * C_HEAD_DIM
    kvw = 2 * C_KV_HEADS * C_HEAD_DIM
    assert (O_KV * LANES) % kvw == 0 and kvw == LANES + SWA_VT_WIDTH and p_qkv.shape[2] >= qw + kvw
    gate_arrays, gate_specs = [], []
    for u in range(SWA_GATE_BLOCKS):
        arr, local = _locate(parts, O_GATE + u)
        gate_arrays.append(arr)
        gate_specs.append(pl.BlockSpec((None, tq, LANES),
                                       functools.partial(lambda bi, i, sk, c: (bi, i, c), c=local)))
    scale = C_HEAD_DIM ** -0.5
    return pl.pallas_call(
        functools.partial(_swa_kernel, tq=tq, c=scale * LOG2E, scale=scale),
        grid_spec=pltpu.PrefetchScalarGridSpec(
            num_scalar_prefetch=1,
            grid=(b, s // tq),
            in_specs=[pl.BlockSpec((None, tq, qw), lambda bi, i, sk: (bi, i, 0)),
                      pl.BlockSpec((None, s, kvw), lambda bi, i, sk: (bi, 0, (O_KV * LANES) // kvw))]
                     + gate_specs,
            out_specs=pl.BlockSpec((None, tq, qw), lambda bi, i, sk: (bi, i, 0)),
            scratch_shapes=[pltpu.VMEM((s // LANES, SWA_VT_WIDTH, LANES), BF16)]),
        out_shape=jax.ShapeDtypeStruct((b, s, qw), BF16),
        compiler_params=_params(("arbitrary", "arbitrary")),
        name="swa_sink_attention",
    )(sinks, p_qkv, p_qkv, *gate_arrays)


def _cast_kernel(w_ref, o_ref):
    o_ref[...] = w_ref[...].astype(o_ref.dtype)


def _cast_bf16(w_stack):
    nl, k, n = w_stack.shape
    tk = min(k, CAST_TK)
    return pl.pallas_call(
        _cast_kernel,
        grid=(nl, k // tk),
        in_specs=[pl.BlockSpec((None, tk, n), lambda l, i: (l, i, 0))],
        out_specs=pl.BlockSpec((None, tk, n), lambda l, i: (l, i, 0)),
        out_shape=jax.ShapeDtypeStruct(w_stack.shape, BF16),
        compiler_params=_params(("arbitrary", "arbitrary")),
        name="cast_bf16",
    )(w_stack)


def _outproj_kernel(*refs, widths, tn, emit_x):
    ny = len(widths)
    y_refs = refs[:ny]
    w_ref, x_ref, g_ref = refs[ny:ny + 3]
    ho_ref = refs[-1]
    hold_ref = refs[ny + 3] if emit_x else ho_ref
    tm, d = x_ref.shape
    ss = jnp.zeros((tm, 1), F32)
    for j in range(d // tn):
        cols = slice(j * tn, (j + 1) * tn)
        acc = x_ref[:, cols]
        off = 0
        for y_ref, wd in zip(y_refs, widths):
            acc = acc + jnp.dot(y_ref[...], w_ref[off:off + wd, cols], preferred_element_type=F32)
            off += wd
        hold_ref[:, cols] = acc
        ss = ss + jnp.sum(acc * acc, axis=1, keepdims=True)
    r = lax.rsqrt(ss * (1.0 / d) + EPS)
    for j in range(d // tn):
        cols = slice(j * tn, (j + 1) * tn)
        ho_ref[:, cols] = ((hold_ref[:, cols] * r) * g_ref[:, cols]).astype(ho_ref.dtype)


def _outproj(ys, w_bf16, idx, x2d, g_next, final):
    t, d = x2d.shape
    widths = tuple(y.shape[1] for y in ys)
    kdim = w_bf16.shape[1]
    assert sum(widths) == kdim and w_bf16.shape[2] == d
    tm = min(t, OUT_TM)
    tn = min(d, OUT_TN)
    row_spec = pl.BlockSpec((tm, d), lambda i: (i, 0))
    in_specs = [pl.BlockSpec((tm, wd), lambda i: (i, 0)) for wd in widths]
    in_specs += [pl.BlockSpec((None, kdim, d), lambda i: (idx, 0, 0)), row_spec,
                 pl.BlockSpec((1, d), lambda i: (0, 0))]
    if final:
        out_specs = [row_spec]
        out_shape = [jax.ShapeDtypeStruct((t, d), F32)]
    else:
        out_specs = [row_spec, row_spec]
        out_shape = [jax.ShapeDtypeStruct((t, d), F32), jax.ShapeDtypeStruct((t, d), BF16)]
    res = pl.pallas_call(
        functools.partial(_outproj_kernel, widths=widths, tn=tn, emit_x=not final),
        grid=(t // tm,),
        in_specs=in_specs,
        out_specs=out_specs,
        out_shape=out_shape,
        compiler_params=_params(("arbitrary",)),
        name="outproj_final" if final else "outproj",
    )(*ys, w_bf16, x2d, g_next.reshape(1, d))
    return res[0] if final else (res[0], res[1])


def kernel(x, mem, positions, even_norm, even_w_in, even_w_mem_kv, even_w_out, odd_norm, odd_w_in,
           odd_w_mem_kv, odd_w_out, odd_sinks, mem_norm, final_norm):
    b, s, d = x.shape
    t = b * s
    depth = even_norm.shape[0] + odd_norm.shape[0]
    pos_col = positions.reshape(t, 1)
    cos_e, sin_e, cos_o, sin_o = _rope_tables(pos_col)
    kinds_e = _even_chunk_kinds()
    kinds_o = _odd_chunk_kinds()
    mem_n = _rmsnorm(mem.reshape(b * MEM_LEN, d), mem_norm)

    w_out_even = _cast_bf16(even_w_out)
    w_out_odd = _cast_bf16(odd_w_out)

    x2d = x.reshape(t, d)
    h = _rmsnorm(x2d, even_norm[0])
    out = None
    for layer in range(depth):
        idx = layer // 2
        last = layer == depth - 1
        if last:
            g_next = final_norm
        elif layer % 2 == 0:
            g_next = odd_norm[idx]
        else:
            g_next = even_norm[idx + 1]
        if layer % 2 == 0:
            memkv = _proj(mem_n, even_w_mem_kv, idx, MEMKV_TN).reshape(b, MEM_LEN, 2 * MEM_HEADS * HEAD_DIM)
            p3 = _proj(h, even_w_in, idx, EVEN_TN, (kinds_e, cos_e, sin_e, HEAD_DIM)).reshape(b, s, EVEN_IN)
            ys = [_dilated_attention(p3), _moba_attention(p3),
                  _mem_attention([(p3, 0)], memkv, E_QM, E_GATE + 2 * A_HEADS)]
            w_out = w_out_even
        else:
            memkv = _proj(mem_n, odd_w_mem_kv, idx, MEMKV_TN).reshape(b, MEM_LEN, 2 * MEM_HEADS * HEAD_DIM)
            wide = ODD_SPLIT_TILES * ODD_TN
            pa = _proj(h, odd_w_in, idx, wide // 2, (kinds_o[:wide // LANES], cos_o, sin_o, C_HEAD_DIM),
                       n_tiles=2, tm=ODD_WIDE_TM).reshape(b, s, wide)
            pb = _proj(h, odd_w_in, idx, ODD_TN, first_tile=ODD_SPLIT_TILES).reshape(b, s, ODD_IN - wide)
            parts = [(pa, 0), (pb, wide // LANES)]
            ys = [_swa_attention(parts, odd_sinks[idx]), _mem_attention(parts, memkv, O_QM, O_GATE + SWA_GATE_BLOCKS)]
            w_out = w_out_odd
        ys = [y.reshape(t, y.shape[-1]) for y in ys]
        if last:
            out = _outproj(ys, w_out, idx, x2d, g_next, final=True)
        else:
            x2d, h = _outproj(ys, w_out, idx, x2d, g_next, final=False)
    return out.reshape(b, s, d)
```

```python
import functools
import math

import jax
import jax.numpy as jnp
from jax import lax
from jax.experimental import pallas as pl
from jax.experimental.pallas import tpu as pltpu

F32 = jnp.float32
BF16 = jnp.bfloat16

LANES = 128
HEAD_DIM = 128
MEM_LEN = 256
MEM_HEADS = 4
A_HEADS = 6
B_HEADS = 6
MOBA_BLOCK = 256
MOBA_TOPK = 3
C_HEADS = 24
C_KV_HEADS = 3
C_HEAD_DIM = 64
C_WINDOW = 128
ROPE_THETA = 10000.0
EPS = 1e-6
NEG = -1e30
LOG2E = 1.4426950408889634

EVEN_WIDTH = (A_HEADS + B_HEADS + MEM_HEADS) * HEAD_DIM
ODD_WIDTH = C_HEADS * C_HEAD_DIM + MEM_HEADS * HEAD_DIM
EVEN_IN = 3 * A_HEADS * HEAD_DIM + 3 * B_HEADS * HEAD_DIM + MEM_HEADS * HEAD_DIM + EVEN_WIDTH
ODD_IN = C_HEADS * C_HEAD_DIM + 2 * C_KV_HEADS * C_HEAD_DIM + MEM_HEADS * HEAD_DIM + ODD_WIDTH

E_QA, E_KA, E_VA = 0, A_HEADS, 2 * A_HEADS
E_QB, E_KB, E_VB = 3 * A_HEADS, 3 * A_HEADS + B_HEADS, 3 * A_HEADS + 2 * B_HEADS
E_QM = 3 * A_HEADS + 3 * B_HEADS
E_GATE = E_QM + MEM_HEADS
O_Q = 0
O_KV = (C_HEADS * C_HEAD_DIM) // LANES
O_QM = O_KV + (2 * C_KV_HEADS * C_HEAD_DIM) // LANES
O_GATE = O_QM + MEM_HEADS

PROJ_TM = 1024
EVEN_TN = 1024
MEMKV_TN = 512
ODD_TN = 896
ODD_SPLIT_TILES = 4
ODD_WIDE_TM = 512
OUT_TM, OUT_TN = 512, 512
NORM_TM = 512
ROPE_TM = 1024
ATTN_BLK = 256
MEM_TQ = 512
SWA_TQ = 1024

VMEM_LIMIT = 56 * 1024 * 1024


def _params(sem):
    return pltpu.CompilerParams(dimension_semantics=sem, vmem_limit_bytes=VMEM_LIMIT)


def _silu(g):
    half = 0.5 * g
    return half + half * jnp.tanh(half)


def _rope_table_kernel(pos_ref, invf_ref, c_full_ref, s_full_ref, c_half_ref, s_half_ref):
    ang = pos_ref[...].astype(F32) * invf_ref[...]
    cos = jnp.cos(ang)
    sin = jnp.sin(ang)
    lane = lax.broadcasted_iota(jnp.int32, cos.shape, 1)
    half, quarter = LANES // 2, LANES // 4

    def spread(t, sign):
        swapped = pltpu.roll(t, half, 1)
        full = jnp.where(lane < half, sign * t, swapped)
        narrow = jnp.where(lane < quarter, sign * swapped,
                           jnp.where(lane < half, pltpu.roll(t, LANES - quarter, 1),
                                     jnp.where(lane < half + quarter, sign * t, pltpu.roll(t, quarter, 1))))
        return full, narrow

    c_full_ref[...], c_half_ref[...] = spread(cos, 1.0)
    s_full_ref[...], s_half_ref[...] = spread(sin, -1.0)


def _rope_tables(pos_col):
    t = pos_col.shape[0]

    def inv_freq(head_dim):
        return jnp.exp(jnp.arange(head_dim // 2, dtype=F32) * (-2.0 * math.log(ROPE_THETA) / head_dim))

    invf = jnp.concatenate([inv_freq(HEAD_DIM), inv_freq(C_HEAD_DIM),
                            jnp.zeros((LANES - HEAD_DIM // 2 - C_HEAD_DIM // 2,), F32)])[None, :]
    tm = min(t, ROPE_TM)
    tab = pl.BlockSpec((tm, LANES), lambda i: (i, 0))
    return pl.pallas_call(
        _rope_table_kernel,
        grid=(t // tm,),
        in_specs=[pl.BlockSpec((tm, 1), lambda i: (i, 0)),
                  pl.BlockSpec((1, LANES), lambda i: (0, 0))],
        out_specs=[tab] * 4,
        out_shape=[jax.ShapeDtypeStruct((t, LANES), F32)] * 4,
        compiler_params=_params(("arbitrary",)),
        name="rope_tables",
    )(pos_col, invf)


def _rmsnorm_kernel(x_ref, g_ref, o_ref):
    x = x_ref[...]
    ms = jnp.mean(x * x, axis=-1, keepdims=True)
    o_ref[...] = ((x * lax.rsqrt(ms + EPS)) * g_ref[...]).astype(o_ref.dtype)


def _rmsnorm(x2d, g):
    t, d = x2d.shape
    tm = min(t, NORM_TM)
    return pl.pallas_call(
        _rmsnorm_kernel,
        grid=(t // tm,),
        in_specs=[pl.BlockSpec((tm, d), lambda i: (i, 0)),
                  pl.BlockSpec((1, d), lambda i: (0, 0))],
        out_specs=pl.BlockSpec((tm, d), lambda i: (i, 0)),
        out_shape=jax.ShapeDtypeStruct((t, d), BF16),
        compiler_params=_params(("arbitrary",)),
        name="rmsnorm",
    )(x2d, g.reshape(1, d))


ROPE_NONE, ROPE_FULL, ROPE_FIRST_HALF = 0, 1, 2


def _rope_partner(x, head_dim):
    if head_dim == LANES:
        return pltpu.roll(x, LANES // 2, 1)
    lane = lax.broadcasted_iota(jnp.int32, x.shape, 1)
    half = head_dim // 2
    return jnp.where((lane % head_dim) < half, pltpu.roll(x, LANES - half, 1), pltpu.roll(x, half, 1))


def _proj_kernel(types_ref, h_ref, w_ref, cos_ref, sin_ref, side_ref, o_ref, side_o_ref, wbf_ref, *, tn, head_dim):
    j = pl.program_id(0)
    i = pl.program_id(1)
    side_o_ref[...] = side_ref[...].astype(side_o_ref.dtype)

    @pl.when(i == 0)
    def _():
        wbf_ref[...] = w_ref[...].astype(BF16)

    nchunk = tn // LANES
    any_rope = types_ref[pl.num_programs(0) * nchunk + j]

    @pl.when(any_rope == 0)
    def _():
        o_ref[...] = jnp.dot(h_ref[...], wbf_ref[...], preferred_element_type=F32).astype(o_ref.dtype)

    @pl.when(any_rope != 0)
    def _():
        acc = jnp.dot(h_ref[...], wbf_ref[...], preferred_element_type=F32)
        cos = cos_ref[...]
        sin = sin_ref[...]
        upper = lax.broadcasted_iota(jnp.int32, cos.shape, 1) >= LANES // 2
        for c in range(nchunk):
            kind = types_ref[j * nchunk + c]
            x = acc[:, c * LANES:(c + 1) * LANES]
            roped = x * cos + _rope_partner(x, head_dim) * sin
            plain = jnp.logical_or(kind == ROPE_NONE, jnp.logical_and(kind == ROPE_FIRST_HALF, upper))
            o_ref[:, c * LANES:(c + 1) * LANES] = jnp.where(plain, x, roped).astype(o_ref.dtype)


def _proj_plain_kernel(h_ref, w_ref, o_ref, wbf_ref):
    @pl.when(pl.program_id(1) == 0)
    def _():
        wbf_ref[...] = w_ref[...].astype(BF16)

    o_ref[...] = jnp.dot(h_ref[...], wbf_ref[...], preferred_element_type=F32).astype(o_ref.dtype)


def _proj(h, w_stack, idx, tn, rope=None, first_tile=0, n_tiles=None, tm=PROJ_TM, side_stack=None):
    t, d = h.shape
    if n_tiles is None:
        n_tiles = w_stack.shape[2] // tn - first_tile
    assert tn % LANES == 0 and (first_tile + n_tiles) * tn <= w_stack.shape[2]
    n = n_tiles * tn
    tm = min(t, tm)
    grid = (n_tiles, t // tm)
    out_shape = jax.ShapeDtypeStruct((t, n), BF16)
    scratch = [pltpu.VMEM((d, tn), BF16)]
    if rope is None:
        return pl.pallas_call(
            _proj_plain_kernel,
            grid=grid,
            in_specs=[pl.BlockSpec((tm, d), lambda j, i: (i, 0)),
                      pl.BlockSpec((None, d, tn), lambda j, i: (idx, 0, first_tile + j))],
            out_specs=pl.BlockSpec((tm, tn), lambda j, i: (i, j)),
            out_shape=out_shape,
            scratch_shapes=scratch,
            compiler_params=_params(("arbitrary", "arbitrary")),
            name="proj_plain",
        )(h, w_stack)
    kinds, cos, sin, head_dim = rope
    per_tile = kinds.reshape(n_tiles, tn // LANES)
    kinds = jnp.concatenate([kinds, (per_tile != ROPE_NONE).any(axis=1).astype(jnp.int32)])
    side_rows, side_cols = side_stack.shape[1:]
    n_i = t // tm
    n_side = 1 << ((n_tiles * n_i).bit_length() - 1)
    assert side_rows % n_side == 0 and (side_rows // n_side) % 16 == 0

    def side_block(j, i):
        return jnp.minimum(j * n_i + i, n_side - 1)

    return pl.pallas_call(
        functools.partial(_proj_kernel, tn=tn, head_dim=head_dim),
        grid_spec=pltpu.PrefetchScalarGridSpec(
            num_scalar_prefetch=1,
            grid=grid,
            in_specs=[pl.BlockSpec((tm, d), lambda j, i, k: (i, 0)),
                      pl.BlockSpec((None, d, tn), lambda j, i, k: (idx, 0, first_tile + j)),
                      pl.BlockSpec((tm, LANES), lambda j, i, k: (i, 0)),
                      pl.BlockSpec((tm, LANES), lambda j, i, k: (i, 0)),
                      pl.BlockSpec((None, side_rows // n_side, side_cols),
                                   lambda j, i, k: (idx, side_block(j, i), 0))],
            out_specs=[pl.BlockSpec((tm, tn), lambda j, i, k: (i, j)),
                       pl.BlockSpec((side_rows // n_side, side_cols), lambda j, i, k: (side_block(j, i), 0))],
            scratch_shapes=scratch),
        out_shape=[out_shape, jax.ShapeDtypeStruct((side_rows, side_cols), BF16)],
        compiler_params=_params(("arbitrary", "arbitrary")),
        name=f"proj_rope{head_dim}",
    )(kinds, h, w_stack, cos, sin, side_stack)


def _even_chunk_kinds():
    kinds = [ROPE_NONE] * (EVEN_IN // LANES)
    for start in (E_QA, E_KA, E_QB, E_KB):
        for c in range(start, start + A_HEADS):
            kinds[c] = ROPE_FULL
    return jnp.asarray(kinds, jnp.int32)


def _odd_chunk_kinds():
    kinds = [ROPE_NONE] * (ODD_IN // LANES)
    k_end = C_HEADS * C_HEAD_DIM + C_KV_HEADS * C_HEAD_DIM
    for c in range(len(kinds)):
        if (c + 1) * LANES <= k_end:
            kinds[c] = ROPE_FULL
        elif c * LANES < k_end:
            assert k_end - c * LANES == LANES // 2
            kinds[c] = ROPE_FIRST_HALF
    return jnp.asarray(kinds, jnp.int32)


def _locate(parts, block):
    arr, first = [pt for pt in parts if pt[1] <= block][-1]
    assert block - first < arr.shape[2] // LANES
    return arr, block - first


def _scores_t(k, q):
    return lax.dot_general(k, q, (((1,), (1,)), ((), ())), preferred_element_type=F32)


def _transpose_bf16(x):
    return x.astype(F32).T.astype(BF16)


def _softmax_probs_t(tiles, biases, weights, c):
    m = None
    for s, b in zip(tiles, biases):
        if b is None:
            mt = jnp.max(s, axis=0, keepdims=True)
        elif b.shape[0] == 1:
            mt = jnp.max(s, axis=0, keepdims=True) + b
        else:
            mt = jnp.max(s + b, axis=0, keepdims=True)
        m = mt if m is None else jnp.maximum(m, mt)
    shifts = {}
    l = None
    ps = []
    for s, b, w in zip(tiles, biases, weights):
        if b is None:
            shift = -m
        else:
            if id(b) not in shifts:
                shifts[id(b)] = b - m
            shift = shifts[id(b)]
        p = jnp.exp2((s + shift) * c)
        if w is not None:
            p = p * w
        lt = jnp.sum(p, axis=0, keepdims=True)
        l = lt if l is None else l + lt
        ps.append(p.astype(BF16))
    p_all = ps[0] if len(ps) == 1 else jnp.concatenate(ps, axis=0)
    return p_all, 1.0 / l


SCORE_LOOKAHEAD = 2
DILATED_LOOKAHEAD = 3


def _tile_delta(blk):
    row = lax.broadcasted_iota(jnp.int32, (blk, blk), 0)
    col = lax.broadcasted_iota(jnp.int32, (blk, blk), 1)
    return col - row


def _dilated_kernel(q_ref, k_ref, v_ref, g_ref, o_ref, *, blk, nblk, c, ahead):
    vt = _transpose_bf16(v_ref[...])
    cr = _tile_delta(blk)

    def near(delta):
        dist = cr + delta * blk
        cnt = (jnp.where(dist <= 128, 1.0, 0.0)
               + jnp.where(jnp.logical_and((dist & 3) == 0, dist <= 512), 1.0, 0.0)
               + jnp.where((dist & 15) == 0, 1.0, 0.0))
        cnt = jnp.where(dist >= 0, cnt, 0.0)
        return jnp.where(cnt > 0.0, 0.0, NEG), cnt

    n_near = -(-512 // blk) + 1
    nears = [near(d) for d in range(min(n_near, nblk))]
    far_bias = jnp.where((cr & 15) == 0, 0.0, NEG)

    def scores(u):
        return _scores_t(k_ref[0:(u + 1) * blk, :], q_ref[u * blk:(u + 1) * blk, :])

    sts = [scores(u) for u in range(min(ahead, nblk))]
    for i in range(nblk):
        rows = slice(i * blk, (i + 1) * blk)
        ext = (i + 1) * blk
        if i + ahead < nblk:
            sts.append(scores(i + ahead))
        st = sts[i]
        tiles, biases, weights = [], [], []
        for n in range(i + 1):
            tiles.append(st[n * blk:(n + 1) * blk])
            if i - n < len(nears):
                bias, cnt = nears[i - n]
                biases.append(bias)
                weights.append(cnt)
            else:
                biases.append(far_bias)
                weights.append(None)
        p_all, inv_l = _softmax_probs_t(tiles, biases, weights, c)
        ot = jnp.dot(vt[:, 0:ext], p_all, preferred_element_type=F32) * inv_l
        o_ref[rows, :] = (ot.T * _silu(g_ref[rows, :].astype(F32))).astype(o_ref.dtype)


def _dilated_attention(p3):
    b, s, _ = p3.shape
    blk = ATTN_BLK
    assert s % blk == 0
    return pl.pallas_call(
        functools.partial(_dilated_kernel, blk=blk, nblk=s // blk, c=HEAD_DIM ** -0.5 * LOG2E,
                          ahead=DILATED_LOOKAHEAD),
        grid=(b, A_HEADS),
        in_specs=[pl.BlockSpec((None, s, LANES), lambda bi, h: (bi, 0, E_QA + h)),
                  pl.BlockSpec((None, s, LANES), lambda bi, h: (bi, 0, E_KA + h)),
                  pl.BlockSpec((None, s, LANES), lambda bi, h: (bi, 0, E_VA + h)),
                  pl.BlockSpec((None, s, LANES), lambda bi, h: (bi, 0, E_GATE + h))],
        out_specs=pl.BlockSpec((None, s, LANES), lambda bi, h: (bi, 0, h)),
        out_shape=jax.ShapeDtypeStruct((b, s, A_HEADS * HEAD_DIM), BF16),
        compiler_params=_params(("arbitrary", "arbitrary")),
        name="dilated_attention",
    )(p3, p3, p3, p3)


def _moba_kernel(q_ref, k_ref, v_ref, g_ref, o_ref, *, blk, nblk, c):
    s_len = nblk * blk
    q = q_ref[...]

    def scores(u):
        return _scores_t(k_ref[0:(u + 1) * blk, :], q[u * blk:(u + 1) * blk, :])

    sts = [scores(u) for u in range(min(SCORE_LOOKAHEAD, nblk))]
    vt = _transpose_bf16(v_ref[...])

    km = jnp.concatenate(
        [jnp.sum(k_ref[n * blk:(n + 1) * blk, :].astype(F32), axis=0, keepdims=True) * (1.0 / blk)
         for n in range(nblk)], axis=0)
    hi = km.astype(BF16).astype(F32)
    ksplit = jnp.concatenate([hi, km - hi], axis=0).astype(BF16)
    gt2 = _scores_t(ksplit, q)
    gate = gt2[:nblk] + gt2[nblk:]

    nid = lax.broadcasted_iota(jnp.int32, (nblk, s_len), 0)
    own = jnp.right_shift(lax.broadcasted_iota(jnp.int32, (nblk, s_len), 1), blk.bit_length() - 1)
    rank = jnp.zeros((nblk, s_len), jnp.int32)
    for mm in range(nblk):
        gm = gate[mm:mm + 1, :]
        beats = jnp.logical_or(gm > gate, jnp.logical_and(gm == gate, mm < nid))
        rank = rank + jnp.where(jnp.logical_and(beats, mm < own), 1, 0)
    sel_bias = jnp.where(jnp.logical_and(rank < MOBA_TOPK, nid < own), 0.0, NEG)
    causal_bias = jnp.where(_tile_delta(blk) >= 0, 0.0, NEG)

    for i in range(nblk):
        rows = slice(i * blk, (i + 1) * blk)
        ext = (i + 1) * blk
        if i + SCORE_LOOKAHEAD < nblk:
            sts.append(scores(i + SCORE_LOOKAHEAD))
        st = sts[i]
        tiles = [st[n * blk:(n + 1) * blk] for n in range(i + 1)]
        biases = [sel_bias[n:n + 1, rows] for n in range(i)] + [causal_bias]
        p_all, inv_l = _softmax_probs_t(tiles, biases, [None] * (i + 1), c)
        ot = jnp.dot(vt[:, 0:ext], p_all, preferred_element_type=F32) * inv_l
        o_ref[rows, :] = (ot.T * _silu(g_ref[rows, :].astype(F32))).astype(o_ref.dtype)


def _moba_attention(p3):
    b, s, _ = p3.shape
    blk = MOBA_BLOCK
    nblk = s // blk
    assert nblk >= MOBA_TOPK and blk & (blk - 1) == 0
    return pl.pallas_call(
        functools.partial(_moba_kernel, blk=blk, nblk=nblk, c=HEAD_DIM ** -0.5 * LOG2E),
        grid=(b, B_HEADS),
        in_specs=[pl.BlockSpec((None, s, LANES), lambda bi, h: (bi, 0, E_QB + h)),
                  pl.BlockSpec((None, s, LANES), lambda bi, h: (bi, 0, E_KB + h)),
                  pl.BlockSpec((None, s, LANES), lambda bi, h: (bi, 0, E_VB + h)),
                  pl.BlockSpec((None, s, LANES), lambda bi, h: (bi, 0, E_GATE + A_HEADS + h))],
        out_specs=pl.BlockSpec((None, s, LANES), lambda bi, h: (bi, 0, h)),
        out_shape=jax.ShapeDtypeStruct((b, s, B_HEADS * HEAD_DIM), BF16),
        compiler_params=_params(("arbitrary", "arbitrary")),
        name="moba_attention",
    )(p3, p3, p3, p3)


def _mem_kernel(*refs, tq, c):
    q_refs = refs[:MEM_HEADS]
    g_refs = refs[MEM_HEADS:2 * MEM_HEADS]
    memkv_ref, o_ref = refs[2 * MEM_HEADS:]
    kvw = MEM_HEADS * HEAD_DIM
    units = [(h, j) for h in range(MEM_HEADS) for j in range(o_ref.shape[0] // tq)]

    def scores(unit):
        h, j = unit
        return _scores_t(memkv_ref[:, h * HEAD_DIM:(h + 1) * HEAD_DIM], q_refs[h][j * tq:(j + 1) * tq, :])

    sts = [scores(u) for u in units[:SCORE_LOOKAHEAD]]
    mvts = [_transpose_bf16(memkv_ref[:, kvw + h * HEAD_DIM:kvw + (h + 1) * HEAD_DIM]) for h in range(MEM_HEADS)]
    for ui, (h, j) in enumerate(units):
        rows = slice(j * tq, (j + 1) * tq)
        if ui + SCORE_LOOKAHEAD < len(units):
            sts.append(scores(units[ui + SCORE_LOOKAHEAD]))
        p_all, inv_l = _softmax_probs_t([sts[ui]], [None], [None], c)
        ot = jnp.dot(mvts[h], p_all, preferred_element_type=F32) * inv_l
        gate = g_refs[h][rows, :].astype(F32)
        o_ref[rows, h * HEAD_DIM:(h + 1) * HEAD_DIM] = (ot.T * _silu(gate)).astype(o_ref.dtype)


def _mem_attention(parts, memkv, q_block, gate_block):
    b, s, _ = parts[0][0].shape
    arrays, specs = [], []
    for first in (q_block, gate_block):
        for h in range(MEM_HEADS):
            arr, local = _locate(parts, first + h)
            arrays.append(arr)
            specs.append(pl.BlockSpec((None, s, LANES), functools.partial(lambda bi, c: (bi, 0, c), c=local)))
    width = MEM_HEADS * HEAD_DIM
    return pl.pallas_call(
        functools.partial(_mem_kernel, tq=min(s, MEM_TQ), c=HEAD_DIM ** -0.5 * LOG2E),
        grid=(b,),
        in_specs=specs + [pl.BlockSpec((None, MEM_LEN, 2 * width), lambda bi: (bi, 0, 0))],
        out_specs=pl.BlockSpec((None, s, width), lambda bi: (bi, 0, 0)),
        out_shape=jax.ShapeDtypeStruct((b, s, width), BF16),
        compiler_params=_params(("arbitrary",)),
        name="memory_attention",
    )(*arrays, memkv)


SWA_QBLK = C_WINDOW
SWA_GATE_BLOCKS = (C_HEADS * C_HEAD_DIM) // LANES
SWA_VT_WIDTH = 2 * LANES


def _swa_kernel(sinks_ref, q_ref, kv_ref, *rest, tq, c, scale):
    gate_refs = rest[:SWA_GATE_BLOCKS]
    o_ref, kvt_ref = rest[SWA_GATE_BLOCKS:]
    i = pl.program_id(1)
    heads_per_group = C_HEADS // C_KV_HEADS
    kw = C_KV_HEADS * C_HEAD_DIM

    @pl.when(i == 0)
    def _():
        for ch in range(kv_ref.shape[0] // LANES):
            t = kv_ref[ch * LANES:(ch + 1) * LANES, LANES:LANES + SWA_VT_WIDTH].astype(F32)
            kvt_ref[ch] = t.T.astype(BF16)

    krow = lax.broadcasted_iota(jnp.int32, (C_WINDOW, SWA_QBLK), 0)
    qcol = lax.broadcasted_iota(jnp.int32, (C_WINDOW, SWA_QBLK), 1)
    from_prev = krow > qcol
    zero_b = jnp.zeros((C_WINDOW, SWA_QBLK), BF16)
    shift = LANES.bit_length() - 1
    units = [(sub, g) for sub in range(tq // SWA_QBLK) for g in range(C_KV_HEADS)]

    def chunks(sub):
        r0 = i * tq + sub * SWA_QBLK
        prev = pl.multiple_of(jnp.maximum(r0 - C_WINDOW, 0), LANES)
        own = pl.multiple_of(r0, LANES)
        return r0, prev, own

    def scores(unit):
        sub, g = unit
        r0, prev, own = chunks(sub)
        rows = slice(sub * SWA_QBLK, (sub + 1) * SWA_QBLK)
        kcols = slice(g * C_HEAD_DIM, (g + 1) * C_HEAD_DIM)
        qg = jnp.concatenate([q_ref[rows, hd * C_HEAD_DIM:(hd + 1) * C_HEAD_DIM]
                              for hd in range(g * heads_per_group, (g + 1) * heads_per_group)], axis=0)
        s_prev = _scores_t(kv_ref[pl.ds(prev, C_WINDOW), kcols], qg)
        if sub == 0:
            s_prev = s_prev + jnp.where(r0 > 0, 0.0, NEG)
        return s_prev, _scores_t(kv_ref[pl.ds(own, C_WINDOW), kcols], qg)

    sts = [scores(u) for u in units[:SCORE_LOOKAHEAD]]
    for ui, (sub, g) in enumerate(units):
        if ui + SCORE_LOOKAHEAD < len(units):
            sts.append(scores(units[ui + SCORE_LOOKAHEAD]))
        s_prev, s_own = sts[ui]
        _, prev, own = chunks(sub)
        rows = slice(sub * SWA_QBLK, (sub + 1) * SWA_QBLK)
        v0 = (kw + g * C_HEAD_DIM) - LANES
        vt_prev = kvt_ref[jnp.right_shift(prev, shift)][v0:v0 + C_HEAD_DIM, :]
        vt_own = kvt_ref[jnp.right_shift(own, shift)][v0:v0 + C_HEAD_DIM, :]
        p_prev, p_own, inv_l = [], [], []
        for u in range(heads_per_group):
            cols = slice(u * SWA_QBLK, (u + 1) * SWA_QBLK)
            s_h = jnp.where(from_prev, s_prev[:, cols], s_own[:, cols])
            sink = sinks_ref[g * heads_per_group + u] * (1.0 / scale)
            m = jnp.maximum(jnp.max(s_h, axis=0, keepdims=True), sink)
            p = jnp.exp2((s_h - m) * c)
            inv_l.append(1.0 / (jnp.sum(p, axis=0, keepdims=True) + jnp.exp2((sink - m) * c)))
            pb = p.astype(BF16)
            p_prev.append(jnp.where(from_prev, pb, zero_b))
            p_own.append(jnp.where(from_prev, zero_b, pb))
        ot = (jnp.dot(vt_prev, jnp.concatenate(p_prev, axis=1), preferred_element_type=F32)
              + jnp.dot(vt_own, jnp.concatenate(p_own, axis=1), preferred_element_type=F32))
        ot = ot * jnp.concatenate(inv_l, axis=1)
        for pair in range(heads_per_group // 2):
            both = jnp.concatenate([ot[:, (2 * pair + u) * SWA_QBLK:(2 * pair + u + 1) * SWA_QBLK]
                                    for u in range(2)], axis=0)
            blk_i = (g * heads_per_group) // 2 + pair
            gate = gate_refs[blk_i][rows, :].astype(F32)
            o_ref[rows, blk_i * LANES:(blk_i + 1) * LANES] = (both.T * _silu(gate)).astype(o_ref.dtype)


def _swa_attention(parts, sinks):
    p_qkv = parts[0][0]
    b, s, _ = p_qkv.shape
    tq = min(s, SWA_TQ)
    qw = C_HEADS * C_HEAD_DIM
    kvw = 2 * C_KV_HEADS * C_HEAD_DIM
    assert (O_KV * LANES) % kvw == 0 and kvw == LANES + SWA_VT_WIDTH and p_qkv.shape[2] >= qw + kvw
    gate_arrays, gate_specs = [], []
    for u in range(SWA_GATE_BLOCKS):
        arr, local = _locate(parts, O_GATE + u)
        gate_arrays.append(arr)
        gate_specs.append(pl.BlockSpec((None, tq, LANES),
                                       functools.partial(lambda bi, i, sk, c: (bi, i, c), c=local)))
    scale = C_HEAD_DIM ** -0.5
    return pl.pallas_call(
        functools.partial(_swa_kernel, tq=tq, c=scale * LOG2E, scale=scale),
        grid_spec=pltpu.PrefetchScalarGridSpec(
            num_scalar_prefetch=1,
            grid=(b, s // tq),
            in_specs=[pl.BlockSpec((None, tq, qw), lambda bi, i, sk: (bi, i, 0)),
                      pl.BlockSpec((None, s, kvw), lambda bi, i, sk: (bi, 0, (O_KV * LANES) // kvw))]
                     + gate_specs,
            out_specs=pl.BlockSpec((None, tq, qw), lambda bi, i, sk: (bi, i, 0)),
            scratch_shapes=[pltpu.VMEM((s // LANES, SWA_VT_WIDTH, LANES), BF16)]),
        out_shape=jax.ShapeDtypeStruct((b, s, qw), BF16),
        compiler_params=_params(("arbitrary", "arbitrary")),
        name="swa_sink_attention",
    )(sinks, p_qkv, p_qkv, *gate_arrays)


def _outproj_kernel(*refs, widths, tn, emit_x):
    ny = len(widths)
    y_refs = refs[:ny]
    w_ref, x_ref, g_ref = refs[ny:ny + 3]
    ho_ref = refs[-1]
    hold_ref = refs[ny + 3] if emit_x else ho_ref
    tm, d = x_ref.shape
    ss = jnp.zeros((tm, 1), F32)
    for j in range(d // tn):
        cols = slice(j * tn, (j + 1) * tn)
        acc = x_ref[:, cols]
        off = 0
        for y_ref, wd in zip(y_refs, widths):
            acc = acc + jnp.dot(y_ref[...], w_ref[off:off + wd, cols], preferred_element_type=F32)
            off += wd
        hold_ref[:, cols] = acc
        ss = ss + jnp.sum(acc * acc, axis=1, keepdims=True)
    r = lax.rsqrt(ss * (1.0 / d) + EPS)
    for j in range(d // tn):
        cols = slice(j * tn, (j + 1) * tn)
        ho_ref[:, cols] = ((hold_ref[:, cols] * r) * g_ref[:, cols]).astype(ho_ref.dtype)


def _outproj(ys, w_bf16, x2d, g_next, final):
    t, d = x2d.shape
    widths = tuple(y.shape[1] for y in ys)
    kdim = w_bf16.shape[0]
    assert sum(widths) == kdim and w_bf16.shape[1] == d
    tm = min(t, OUT_TM)
    tn = min(d, OUT_TN)
    row_spec = pl.BlockSpec((tm, d), lambda i: (i, 0))
    in_specs = [pl.BlockSpec((tm, wd), lambda i: (i, 0)) for wd in widths]
    in_specs += [pl.BlockSpec((kdim, d), lambda i: (0, 0)), row_spec,
                 pl.BlockSpec((1, d), lambda i: (0, 0))]
    if final:
        out_specs = [row_spec]
        out_shape = [jax.ShapeDtypeStruct((t, d), F32)]
    else:
        out_specs = [row_spec, row_spec]
        out_shape = [jax.ShapeDtypeStruct((t, d), F32), jax.ShapeDtypeStruct((t, d), BF16)]
    res = pl.pallas_call(
        functools.partial(_outproj_kernel, widths=widths, tn=tn, emit_x=not final),
        grid=(t // tm,),
        in_specs=in_specs,
        out_specs=out_specs,
        out_shape=out_shape,
        compiler_params=_params(("arbitrary",)),
        name="outproj_final" if final else "outproj",
    )(*ys, w_bf16, x2d, g_next.reshape(1, d))
    return res[0] if final else (res[0], res[1])


def kernel(x, mem, positions, even_norm, even_w_in, even_w_mem_kv, even_w_out, odd_norm, odd_w_in,
           odd_w_mem_kv, odd_w_out, odd_sinks, mem_norm, final_norm):
    b, s, d = x.shape
    t = b * s
    depth = even_norm.shape[0] + odd_norm.shape[0]
    pos_col = positions.reshape(t, 1)
    cos_e, sin_e, cos_o, sin_o = _rope_tables(pos_col)
    kinds_e = _even_chunk_kinds()
    kinds_o = _odd_chunk_kinds()
    mem_n = _rmsnorm(mem.reshape(b * MEM_LEN, d), mem_norm)

    x2d = x.reshape(t, d)
    h = _rmsnorm(x2d, even_norm[0])
    out = None
    for layer in range(depth):
        idx = layer // 2
        last = layer == depth - 1
        if last:
            g_next = final_norm
        elif layer % 2 == 0:
            g_next = odd_norm[idx]
        else:
            g_next = even_norm[idx + 1]
        if layer % 2 == 0:
            memkv = _proj(mem_n, even_w_mem_kv, idx, MEMKV_TN).reshape(b, MEM_LEN, 2 * MEM_HEADS * HEAD_DIM)
            p3, w_out = _proj(h, even_w_in, idx, EVEN_TN, (kinds_e, cos_e, sin_e, HEAD_DIM), side_stack=even_w_out)
            p3 = p3.reshape(b, s, EVEN_IN)
            ys = [_dilated_attention(p3), _moba_attention(p3),
                  _mem_attention([(p3, 0)], memkv, E_QM, E_GATE + 2 * A_HEADS)]
        else:
            memkv = _proj(mem_n, odd_w_mem_kv, idx, MEMKV_TN).reshape(b, MEM_LEN, 2 * MEM_HEADS * HEAD_DIM)
            wide = ODD_SPLIT_TILES * ODD_TN
            pa, w_out = _proj(h, odd_w_in, idx, wide // 2, (kinds_o[:wide // LANES], cos_o, sin_o, C_HEAD_DIM),
                              n_tiles=2, tm=ODD_WIDE_TM, side_stack=odd_w_out)
            pa = pa.reshape(b, s, wide)
            pb = _proj(h, odd_w_in, idx, ODD_TN, first_tile=ODD_SPLIT_TILES).reshape(b, s, ODD_IN - wide)
            parts = [(pa, 0), (pb, wide // LANES)]
            ys = [_swa_attention(parts, odd_sinks[idx]), _mem_attention(parts, memkv, O_QM, O_GATE + SWA_GATE_BLOCKS)]
        ys = [y.reshape(t, y.shape[-1]) for y in ys]
        if last:
            out = _outproj(ys, w_out, x2d, g_next, final=True)
        else:
            x2d, h = _outproj(ys, w_out, x2d, g_next, final=False)
    return out.reshape(b, s, d)
```

```python
import functools
import math

import jax
import jax.numpy as jnp
from jax import lax
from jax.experimental import pallas as pl
from jax.experimental.pallas import tpu as pltpu

F32 = jnp.float32
BF16 = jnp.bfloat16

LANES = 128
HEAD_DIM = 128
MEM_LEN = 256
MEM_HEADS = 4
A_HEADS = 6
B_HEADS = 6
MOBA_BLOCK = 256
MOBA_TOPK = 3
C_HEADS = 24
C_KV_HEADS = 3
C_HEAD_DIM = 64
C_WINDOW = 128
ROPE_THETA = 10000.0
EPS = 1e-6
NEG = -1e30
LOG2E = 1.4426950408889634

EVEN_WIDTH = (A_HEADS + B_HEADS + MEM_HEADS) * HEAD_DIM
ODD_WIDTH = C_HEADS * C_HEAD_DIM + MEM_HEADS * HEAD_DIM
EVEN_IN = 3 * A_HEADS * HEAD_DIM + 3 * B_HEADS * HEAD_DIM + MEM_HEADS * HEAD_DIM + EVEN_WIDTH
ODD_IN = C_HEADS * C_HEAD_DIM + 2 * C_KV_HEADS * C_HEAD_DIM + MEM_HEADS * HEAD_DIM + ODD_WIDTH

E_QA, E_KA, E_VA = 0, A_HEADS, 2 * A_HEADS
E_QB, E_KB, E_VB = 3 * A_HEADS, 3 * A_HEADS + B_HEADS, 3 * A_HEADS + 2 * B_HEADS
E_QM = 3 * A_HEADS + 3 * B_HEADS
E_GATE = E_QM + MEM_HEADS
O_Q = 0
O_KV = (C_HEADS * C_HEAD_DIM) // LANES
O_QM = O_KV + (2 * C_KV_HEADS * C_HEAD_DIM) // LANES
O_GATE = O_QM + MEM_HEADS

PROJ_TM = 1024
EVEN_TN = 1024
MEMKV_TN = 512
ODD_TN = 896
ODD_SPLIT_TILES = 4
ODD_WIDE_TM = 512
OUT_TM, OUT_TN = 512, 512
NORM_TM = 512
ATTN_BLK = 256
MEM_TQ = 512
SWA_TQ = 1024

VMEM_LIMIT = 56 * 1024 * 1024


def _params(sem):
    return pltpu.CompilerParams(dimension_semantics=sem, vmem_limit_bytes=VMEM_LIMIT)


def _silu(g):
    half = 0.5 * g
    return half + half * jnp.tanh(half)


def _rmsnorm_rope_kernel(x_ref, g_ref, pos_ref, invf_ref, o_ref, c_full_ref, s_full_ref, c_half_ref, s_half_ref):
    _rmsnorm_kernel(x_ref, g_ref, o_ref)
    ang = pos_ref[...].astype(F32) * invf_ref[...]
    cos = jnp.cos(ang)
    sin = jnp.sin(ang)
    lane = lax.broadcasted_iota(jnp.int32, cos.shape, 1)
    half, quarter = LANES // 2, LANES // 4

    def spread(t, sign):
        swapped = pltpu.roll(t, half, 1)
        full = jnp.where(lane < half, sign * t, swapped)
        narrow = jnp.where(lane < quarter, sign * swapped,
                           jnp.where(lane < half, pltpu.roll(t, LANES - quarter, 1),
                                     jnp.where(lane < half + quarter, sign * t, pltpu.roll(t, quarter, 1))))
        return full, narrow

    c_full_ref[...], c_half_ref[...] = spread(cos, 1.0)
    s_full_ref[...], s_half_ref[...] = spread(sin, -1.0)


def _rmsnorm_and_rope_tables(x2d, g, pos_col):
    t, d = x2d.shape

    def inv_freq(head_dim):
        return jnp.exp(jnp.arange(head_dim // 2, dtype=F32) * (-2.0 * math.log(ROPE_THETA) / head_dim))

    invf = jnp.concatenate([inv_freq(HEAD_DIM), inv_freq(C_HEAD_DIM),
                            jnp.zeros((LANES - HEAD_DIM // 2 - C_HEAD_DIM // 2,), F32)])[None, :]
    tm = min(t, NORM_TM)
    tab = pl.BlockSpec((tm, LANES), lambda i: (i, 0))
    return pl.pallas_call(
        _rmsnorm_rope_kernel,
        grid=(t // tm,),
        in_specs=[pl.BlockSpec((tm, d), lambda i: (i, 0)),
                  pl.BlockSpec((1, d), lambda i: (0, 0)),
                  pl.BlockSpec((tm, 1), lambda i: (i, 0)),
                  pl.BlockSpec((1, LANES), lambda i: (0, 0))],
        out_specs=[pl.BlockSpec((tm, d), lambda i: (i, 0))] + [tab] * 4,
        out_shape=[jax.ShapeDtypeStruct((t, d), BF16)] + [jax.ShapeDtypeStruct((t, LANES), F32)] * 4,
        compiler_params=_params(("arbitrary",)),
        name="rmsnorm_rope_tables",
    )(x2d, g.reshape(1, d), pos_col, invf)


def _rmsnorm_kernel(x_ref, g_ref, o_ref):
    x = x_ref[...]
    ms = jnp.mean(x * x, axis=-1, keepdims=True)
    o_ref[...] = ((x * lax.rsqrt(ms + EPS)) * g_ref[...]).astype(o_ref.dtype)


def _rmsnorm(x2d, g):
    t, d = x2d.shape
    tm = min(t, NORM_TM)
    return pl.pallas_call(
        _rmsnorm_kernel,
        grid=(t // tm,),
        in_specs=[pl.BlockSpec((tm, d), lambda i: (i, 0)),
                  pl.BlockSpec((1, d), lambda i: (0, 0))],
        out_specs=pl.BlockSpec((tm, d), lambda i: (i, 0)),
        out_shape=jax.ShapeDtypeStruct((t, d), BF16),
        compiler_params=_params(("arbitrary",)),
        name="rmsnorm",
    )(x2d, g.reshape(1, d))


ROPE_NONE, ROPE_FULL, ROPE_FIRST_HALF = 0, 1, 2


def _rope_partner(x, head_dim):
    if head_dim == LANES:
        return pltpu.roll(x, LANES // 2, 1)
    lane = lax.broadcasted_iota(jnp.int32, x.shape, 1)
    half = head_dim // 2
    return jnp.where((lane % head_dim) < half, pltpu.roll(x, LANES - half, 1), pltpu.roll(x, half, 1))


def _proj_kernel(types_ref, h_ref, w_ref, cos_ref, sin_ref, side_ref, o_ref, side_o_ref, wbf_ref, *, tn, head_dim):
    j = pl.program_id(0)
    i = pl.program_id(1)
    side_o_ref[...] = side_ref[...].astype(side_o_ref.dtype)

    @pl.when(i == 0)
    def _():
        wbf_ref[...] = w_ref[...].astype(BF16)

    nchunk = tn // LANES
    any_rope = types_ref[pl.num_programs(0) * nchunk + j]

    @pl.when(any_rope == 0)
    def _():
        o_ref[...] = jnp.dot(h_ref[...], wbf_ref[...], preferred_element_type=F32).astype(o_ref.dtype)

    @pl.when(any_rope != 0)
    def _():
        acc = jnp.dot(h_ref[...], wbf_ref[...], preferred_element_type=F32)
        cos = cos_ref[...]
        sin = sin_ref[...]
        upper = lax.broadcasted_iota(jnp.int32, cos.shape, 1) >= LANES // 2
        for c in range(nchunk):
            kind = types_ref[j * nchunk + c]
            x = acc[:, c * LANES:(c + 1) * LANES]
            roped = x * cos + _rope_partner(x, head_dim) * sin
            plain = jnp.logical_or(kind == ROPE_NONE, jnp.logical_and(kind == ROPE_FIRST_HALF, upper))
            o_ref[:, c * LANES:(c + 1) * LANES] = jnp.where(plain, x, roped).astype(o_ref.dtype)


def _proj_plain_kernel(h_ref, w_ref, o_ref, wbf_ref):
    @pl.when(pl.program_id(1) == 0)
    def _():
        wbf_ref[...] = w_ref[...].astype(BF16)

    o_ref[...] = jnp.dot(h_ref[...], wbf_ref[...], preferred_element_type=F32).astype(o_ref.dtype)


def _proj(h, w_stack, idx, tn, rope=None, first_tile=0, n_tiles=None, tm=PROJ_TM, side_stack=None):
    t, d = h.shape
    if n_tiles is None:
        n_tiles = w_stack.shape[2] // tn - first_tile
    assert tn % LANES == 0 and (first_tile + n_tiles) * tn <= w_stack.shape[2]
    n = n_tiles * tn
    tm = min(t, tm)
    grid = (n_tiles, t // tm)
    out_shape = jax.ShapeDtypeStruct((t, n), BF16)
    scratch = [pltpu.VMEM((d, tn), BF16)]
    if rope is None:
        return pl.pallas_call(
            _proj_plain_kernel,
            grid=grid,
            in_specs=[pl.BlockSpec((tm, d), lambda j, i: (i, 0)),
                      pl.BlockSpec((None, d, tn), lambda j, i: (idx, 0, first_tile + j))],
            out_specs=pl.BlockSpec((tm, tn), lambda j, i: (i, j)),
            out_shape=out_shape,
            scratch_shapes=scratch,
            compiler_params=_params(("arbitrary", "arbitrary")),
            name="proj_plain",
        )(h, w_stack)
    kinds, cos, sin, head_dim = rope
    per_tile = kinds.reshape(n_tiles, tn // LANES)
    kinds = jnp.concatenate([kinds, (per_tile != ROPE_NONE).any(axis=1).astype(jnp.int32)])
    side_rows, side_cols = side_stack.shape[1:]
    n_i = t // tm
    n_side = 1 << ((n_tiles * n_i).bit_length() - 1)
    assert side_rows % n_side == 0 and (side_rows // n_side) % 16 == 0

    def side_block(j, i):
        return jnp.minimum(j * n_i + i, n_side - 1)

    return pl.pallas_call(
        functools.partial(_proj_kernel, tn=tn, head_dim=head_dim),
        grid_spec=pltpu.PrefetchScalarGridSpec(
            num_scalar_prefetch=1,
            grid=grid,
            in_specs=[pl.BlockSpec((tm, d), lambda j, i, k: (i, 0)),
                      pl.BlockSpec((None, d, tn), lambda j, i, k: (idx, 0, first_tile + j)),
                      pl.BlockSpec((tm, LANES), lambda j, i, k: (i, 0)),
                      pl.BlockSpec((tm, LANES), lambda j, i, k: (i, 0)),
                      pl.BlockSpec((None, side_rows // n_side, side_cols),
                                   lambda j, i, k: (idx, side_block(j, i), 0))],
            out_specs=[pl.BlockSpec((tm, tn), lambda j, i, k: (i, j)),
                       pl.BlockSpec((side_rows // n_side, side_cols), lambda j, i, k: (side_block(j, i), 0))],
            scratch_shapes=scratch),
        out_shape=[out_shape, jax.ShapeDtypeStruct((side_rows, side_cols), BF16)],
        compiler_params=_params(("arbitrary", "arbitrary")),
        name=f"proj_rope{head_dim}",
    )(kinds, h, w_stack, cos, sin, side_stack)


def _even_chunk_kinds():
    kinds = [ROPE_NONE] * (EVEN_IN // LANES)
    for start in (E_QA, E_KA, E_QB, E_KB):
        for c in range(start, start + A_HEADS):
            kinds[c] = ROPE_FULL
    return jnp.asarray(kinds, jnp.int32)


def _odd_chunk_kinds():
    kinds = [ROPE_NONE] * (ODD_IN // LANES)
    k_end = C_HEADS * C_HEAD_DIM + C_KV_HEADS * C_HEAD_DIM
    for c in range(len(kinds)):
        if (c + 1) * LANES <= k_end:
            kinds[c] = ROPE_FULL
        elif c * LANES < k_end:
            assert k_end - c * LANES == LANES // 2
            kinds[c] = ROPE_FIRST_HALF
    return jnp.asarray(kinds, jnp.int32)


def _locate(parts, block):
    arr, first = [pt for pt in parts if pt[1] <= block][-1]
    assert block - first < arr.shape[2] // LANES
    return arr, block - first


def _scores_t(k, q):
    return lax.dot_general(k, q, (((1,), (1,)), ((), ())), preferred_element_type=F32)


def _transpose_bf16(x):
    return x.astype(F32).T.astype(BF16)


def _softmax_probs_t(tiles, biases, weights, c):
    m = None
    for s, b in zip(tiles, biases):
        if b is None:
            mt = jnp.max(s, axis=0, keepdims=True)
        elif b.shape[0] == 1:
            mt = jnp.max(s, axis=0, keepdims=True) + b
        else:
            mt = jnp.max(s + b, axis=0, keepdims=True)
        m = mt if m is None else jnp.maximum(m, mt)
    shifts = {}
    l = None
    ps = []
    for s, b, w in zip(tiles, biases, weights):
        if b is None:
            shift = -m
        else:
            if id(b) not in shifts:
                shifts[id(b)] = b - m
            shift = shifts[id(b)]
        p = jnp.exp2((s + shift) * c)
        if w is not None:
            p = p * w
        lt = jnp.sum(p, axis=0, keepdims=True)
        l = lt if l is None else l + lt
        ps.append(p.astype(BF16))
    p_all = ps[0] if len(ps) == 1 else jnp.concatenate(ps, axis=0)
    return p_all, 1.0 / l


SCORE_LOOKAHEAD = 2
DILATED_LOOKAHEAD = 3


def _tile_delta(blk):
    row = lax.broadcasted_iota(jnp.int32, (blk, blk), 0)
    col = lax.broadcasted_iota(jnp.int32, (blk, blk), 1)
    return col - row


def _dilated_kernel(q_ref, k_ref, v_ref, g_ref, o_ref, *, blk, nblk, c, ahead):
    vt = _transpose_bf16(v_ref[...])
    cr = _tile_delta(blk)

    def near(delta):
        dist = cr + delta * blk
        cnt = (jnp.where(dist <= 128, 1.0, 0.0)
               + jnp.where(jnp.logical_and((dist & 3) == 0, dist <= 512), 1.0, 0.0)
               + jnp.where((dist & 15) == 0, 1.0, 0.0))
        cnt = jnp.where(dist >= 0, cnt, 0.0)
        return jnp.where(cnt > 0.0, 0.0, NEG), cnt

    n_near = -(-512 // blk) + 1
    nears = [near(d) for d in range(min(n_near, nblk))]
    far_bias = jnp.where((cr & 15) == 0, 0.0, NEG)

    def scores(u):
        return _scores_t(k_ref[0:(u + 1) * blk, :], q_ref[u * blk:(u + 1) * blk, :])

    sts = [scores(u) for u in range(min(ahead, nblk))]
    for i in range(nblk):
        rows = slice(i * blk, (i + 1) * blk)
        ext = (i + 1) * blk
        if i + ahead < nblk:
            sts.append(scores(i + ahead))
        st = sts[i]
        tiles, biases, weights = [], [], []
        for n in range(i + 1):
            tiles.append(st[n * blk:(n + 1) * blk])
            if i - n < len(nears):
                bias, cnt = nears[i - n]
                biases.append(bias)
                weights.append(cnt)
            else:
                biases.append(far_bias)
                weights.append(None)
        p_all, inv_l = _softmax_probs_t(tiles, biases, weights, c)
        ot = jnp.dot(vt[:, 0:ext], p_all, preferred_element_type=F32) * inv_l
        o_ref[rows, :] = (ot.T * _silu(g_ref[rows, :].astype(F32))).astype(o_ref.dtype)


def _dilated_attention(p3):
    b, s, _ = p3.shape
    blk = ATTN_BLK
    assert s % blk == 0
    return pl.pallas_call(
        functools.partial(_dilated_kernel, blk=blk, nblk=s // blk, c=HEAD_DIM ** -0.5 * LOG2E,
                          ahead=DILATED_LOOKAHEAD),
        grid=(b, A_HEADS),
        in_specs=[pl.BlockSpec((None, s, LANES), lambda bi, h: (bi, 0, E_QA + h)),
                  pl.BlockSpec((None, s, LANES), lambda bi, h: (bi, 0, E_KA + h)),
                  pl.BlockSpec((None, s, LANES), lambda bi, h: (bi, 0, E_VA + h)),
                  pl.BlockSpec((None, s, LANES), lambda bi, h: (bi, 0, E_GATE + h))],
        out_specs=pl.BlockSpec((None, s, LANES), lambda bi, h: (bi, 0, h)),
        out_shape=jax.ShapeDtypeStruct((b, s, A_HEADS * HEAD_DIM), BF16),
        compiler_params=_params(("arbitrary", "arbitrary")),
        name="dilated_attention",
    )(p3, p3, p3, p3)


def _moba_kernel(q_ref, k_ref, v_ref, g_ref, o_ref, *, blk, nblk, c):
    s_len = nblk * blk
    q = q_ref[...]

    def scores(u):
        return _scores_t(k_ref[0:(u + 1) * blk, :], q[u * blk:(u + 1) * blk, :])

    sts = [scores(u) for u in range(min(SCORE_LOOKAHEAD, nblk))]
    vt = _transpose_bf16(v_ref[...])

    km = jnp.concatenate(
        [jnp.sum(k_ref[n * blk:(n + 1) * blk, :].astype(F32), axis=0, keepdims=True) * (1.0 / blk)
         for n in range(nblk)], axis=0)
    hi = km.astype(BF16).astype(F32)
    ksplit = jnp.concatenate([hi, km - hi], axis=0).astype(BF16)
    gt2 = _scores_t(ksplit, q)
    gate = gt2[:nblk] + gt2[nblk:]

    nid = lax.broadcasted_iota(jnp.int32, (nblk, s_len), 0)
    own = jnp.right_shift(lax.broadcasted_iota(jnp.int32, (nblk, s_len), 1), blk.bit_length() - 1)
    rank = jnp.zeros((nblk, s_len), jnp.int32)
    for mm in range(nblk):
        gm = gate[mm:mm + 1, :]
        beats = jnp.logical_or(gm > gate, jnp.logical_and(gm == gate, mm < nid))
        rank = rank + jnp.where(jnp.logical_and(beats, mm < own), 1, 0)
    sel_bias = jnp.where(jnp.logical_and(rank < MOBA_TOPK, nid < own), 0.0, NEG)
    causal_bias = jnp.where(_tile_delta(blk) >= 0, 0.0, NEG)

    for i in range(nblk):
        rows = slice(i * blk, (i + 1) * blk)
        ext = (i + 1) * blk
        if i + SCORE_LOOKAHEAD < nblk:
            sts.append(scores(i + SCORE_LOOKAHEAD))
        st = sts[i]
        tiles = [st[n * blk:(n + 1) * blk] for n in range(i + 1)]
        biases = [sel_bias[n:n + 1, rows] for n in range(i)] + [causal_bias]
        p_all, inv_l = _softmax_probs_t(tiles, biases, [None] * (i + 1), c)
        ot = jnp.dot(vt[:, 0:ext], p_all, preferred_element_type=F32) * inv_l
        o_ref[rows, :] = (ot.T * _silu(g_ref[rows, :].astype(F32))).astype(o_ref.dtype)


def _moba_attention(p3):
    b, s, _ = p3.shape
    blk = MOBA_BLOCK
    nblk = s // blk
    assert nblk >= MOBA_TOPK and blk & (blk - 1) == 0
    return pl.pallas_call(
        functools.partial(_moba_kernel, blk=blk, nblk=nblk, c=HEAD_DIM ** -0.5 * LOG2E),
        grid=(b, B_HEADS),
        in_specs=[pl.BlockSpec((None, s, LANES), lambda bi, h: (bi, 0, E_QB + h)),
                  pl.BlockSpec((None, s, LANES), lambda bi, h: (bi, 0, E_KB + h)),
                  pl.BlockSpec((None, s, LANES), lambda bi, h: (bi, 0, E_VB + h)),
                  pl.BlockSpec((None, s, LANES), lambda bi, h: (bi, 0, E_GATE + A_HEADS + h))],
        out_specs=pl.BlockSpec((None, s, LANES), lambda bi, h: (bi, 0, h)),
        out_shape=jax.ShapeDtypeStruct((b, s, B_HEADS * HEAD_DIM), BF16),
        compiler_params=_params(("arbitrary", "arbitrary")),
        name="moba_attention",
    )(p3, p3, p3, p3)


def _mem_kernel(*refs, tq, c):
    q_refs = refs[:MEM_HEADS]
    g_refs = refs[MEM_HEADS:2 * MEM_HEADS]
    memkv_ref, o_ref = refs[2 * MEM_HEADS:]
    kvw = MEM_HEADS * HEAD_DIM
    units = [(h, j) for h in range(MEM_HEADS) for j in range(o_ref.shape[0] // tq)]

    def scores(unit):
        h, j = unit
        return _scores_t(memkv_ref[:, h * HEAD_DIM:(h + 1) * HEAD_DIM], q_refs[h][j * tq:(j + 1) * tq, :])

    sts = [scores(u) for u in units[:SCORE_LOOKAHEAD]]
    mvts = [_transpose_bf16(memkv_ref[:, kvw + h * HEAD_DIM:kvw + (h + 1) * HEAD_DIM]) for h in range(MEM_HEADS)]
    for ui, (h, j) in enumerate(units):
        rows = slice(j * tq, (j + 1) * tq)
        if ui + SCORE_LOOKAHEAD < len(units):
            sts.append(scores(units[ui + SCORE_LOOKAHEAD]))
        p_all, inv_l = _softmax_probs_t([sts[ui]], [None], [None], c)
        ot = jnp.dot(mvts[h], p_all, preferred_element_type=F32) * inv_l
        gate = g_refs[h][rows, :].astype(F32)
        o_ref[rows, h * HEAD_DIM:(h + 1) * HEAD_DIM] = (ot.T * _silu(gate)).astype(o_ref.dtype)


def _mem_attention(parts, memkv, q_block, gate_block):
    b, s, _ = parts[0][0].shape
    arrays, specs = [], []
    for first in (q_block, gate_block):
        for h in range(MEM_HEADS):
            arr, local = _locate(parts, first + h)
            arrays.append(arr)
            specs.append(pl.BlockSpec((None, s, LANES), functools.partial(lambda bi, c: (bi, 0, c), c=local)))
    width = MEM_HEADS * HEAD_DIM
    return pl.pallas_call(
        functools.partial(_mem_kernel, tq=min(s, MEM_TQ), c=HEAD_DIM ** -0.5 * LOG2E),
        grid=(b,),
        in_specs=specs + [pl.BlockSpec((None, MEM_LEN, 2 * width), lambda bi: (bi, 0, 0))],
        out_specs=pl.BlockSpec((None, s, width), lambda bi: (bi, 0, 0)),
        out_shape=jax.ShapeDtypeStruct((b, s, width), BF16),
        compiler_params=_params(("arbitrary",)),
        name="memory_attention",
    )(*arrays, memkv)


SWA_QBLK = C_WINDOW
SWA_GATE_BLOCKS = (C_HEADS * C_HEAD_DIM) // LANES
SWA_VT_WIDTH = 2 * LANES


def _swa_kernel(sinks_ref, q_ref, kv_ref, *rest, tq, c, scale):
    gate_refs = rest[:SWA_GATE_BLOCKS]
    o_ref, kvt_ref = rest[SWA_GATE_BLOCKS:]
    i = pl.program_id(1)
    heads_per_group = C_HEADS // C_KV_HEADS
    kw = C_KV_HEADS * C_HEAD_DIM

    @pl.when(i == 0)
    def _():
        for ch in range(kv_ref.shape[0] // LANES):
            t = kv_ref[ch * LANES:(ch + 1) * LANES, LANES:LANES + SWA_VT_WIDTH].astype(F32)
            kvt_ref[ch] = t.T.astype(BF16)

    krow = lax.broadcasted_iota(jnp.int32, (C_WINDOW, SWA_QBLK), 0)
    qcol = lax.broadcasted_iota(jnp.int32, (C_WINDOW, SWA_QBLK), 1)
    from_prev = krow > qcol
    zero_b = jnp.zeros((C_WINDOW, SWA_QBLK), BF16)
    shift = LANES.bit_length() - 1
    units = [(sub, g) for sub in range(tq // SWA_QBLK) for g in range(C_KV_HEADS)]

    def chunks(sub):
        r0 = i * tq + sub * SWA_QBLK
        prev = pl.multiple_of(jnp.maximum(r0 - C_WINDOW, 0), LANES)
        own = pl.multiple_of(r0, LANES)
        return r0, prev, own

    def scores(unit):
        sub, g = unit
        r0, prev, own = chunks(sub)
        rows = slice(sub * SWA_QBLK, (sub + 1) * SWA_QBLK)
        kcols = slice(g * C_HEAD_DIM, (g + 1) * C_HEAD_DIM)
        qg = jnp.concatenate([q_ref[rows, hd * C_HEAD_DIM:(hd + 1) * C_HEAD_DIM]
                              for hd in range(g * heads_per_group, (g + 1) * heads_per_group)], axis=0)
        s_prev = _scores_t(kv_ref[pl.ds(prev, C_WINDOW), kcols], qg)
        if sub == 0:
            s_prev = s_prev + jnp.where(r0 > 0, 0.0, NEG)
        return s_prev, _scores_t(kv_ref[pl.ds(own, C_WINDOW), kcols], qg)

    sts = [scores(u) for u in units[:SCORE_LOOKAHEAD]]
    for ui, (sub, g) in enumerate(units):
        if ui + SCORE_LOOKAHEAD < len(units):
            sts.append(scores(units[ui + SCORE_LOOKAHEAD]))
        s_prev, s_own = sts[ui]
        _, prev, own = chunks(sub)
        rows = slice(sub * SWA_QBLK, (sub + 1) * SWA_QBLK)
        v0 = (kw + g * C_HEAD_DIM) - LANES
        vt_prev = kvt_ref[jnp.right_shift(prev, shift)][v0:v0 + C_HEAD_DIM, :]
        vt_own = kvt_ref[jnp.right_shift(own, shift)][v0:v0 + C_HEAD_DIM, :]
        p_prev, p_own, inv_l = [], [], []
        for u in range(heads_per_group):
            cols = slice(u * SWA_QBLK, (u + 1) * SWA_QBLK)
            s_h = jnp.where(from_prev, s_prev[:, cols], s_own[:, cols])
            sink = sinks_ref[g * heads_per_group + u] * (1.0 / scale)
            m = jnp.maximum(jnp.max(s_h, axis=0, keepdims=True), sink)
            p = jnp.exp2((s_h - m) * c)
            inv_l.append(1.0 / (jnp.sum(p, axis=0, keepdims=True) + jnp.exp2((sink - m) * c)))
            pb = p.astype(BF16)
            p_prev.append(jnp.where(from_prev, pb, zero_b))
            p_own.append(jnp.where(from_prev, zero_b, pb))
        ot = (jnp.dot(vt_prev, jnp.concatenate(p_prev, axis=1), preferred_element_type=F32)
              + jnp.dot(vt_own, jnp.concatenate(p_own, axis=1), preferred_element_type=F32))
        ot = ot * jnp.concatenate(inv_l, axis=1)
        for pair in range(heads_per_group // 2):
            both = jnp.concatenate([ot[:, (2 * pair + u) * SWA_QBLK:(2 * pair + u + 1) * SWA_QBLK]
                                    for u in range(2)], axis=0)
            blk_i = (g * heads_per_group) // 2 + pair
            gate = gate_refs[blk_i][rows, :].astype(F32)
            o_ref[rows, blk_i * LANES:(blk_i + 1) * LANES] = (both.T * _silu(gate)).astype(o_ref.dtype)


def _swa_attention(parts, sinks):
    p_qkv = parts[0][0]
    b, s, _ = p_qkv.shape
    tq = min(s, SWA_TQ)
    qw = C_HEADS * C_HEAD_DIM
    kvw = 2 * C_KV_HEADS * C_HEAD_DIM
    assert (O_KV * LANES) % kvw == 0 and kvw == LANES + SWA_VT_WIDTH and p_qkv.shape[2] >= qw + kvw
    gate_arrays, gate_specs = [], []
    for u in range(SWA_GATE_BLOCKS):
        arr, local = _locate(parts, O_GATE + u)
        gate_arrays.append(arr)
        gate_specs.append(pl.BlockSpec((None, tq, LANES),
                                       functools.partial(lambda bi, i, sk, c: (bi, i, c), c=local)))
    scale = C_HEAD_DIM ** -0.5
    return pl.pallas_call(
        functools.partial(_swa_kernel, tq=tq, c=scale * LOG2E, scale=scale),
        grid_spec=pltpu.PrefetchScalarGridSpec(
            num_scalar_prefetch=1,
            grid=(b, s // tq),
            in_specs=[pl.BlockSpec((None, tq, qw), lambda bi, i, sk: (bi, i, 0)),
                      pl.BlockSpec((None, s, kvw), lambda bi, i, sk: (bi, 0, (O_KV * LANES) // kvw))]
                     + gate_specs,
            out_specs=pl.BlockSpec((None, tq, qw), lambda bi, i, sk: (bi, i, 0)),
            scratch_shapes=[pltpu.VMEM((s // LANES, SWA_VT_WIDTH, LANES), BF16)]),
        out_shape=jax.ShapeDtypeStruct((b, s, qw), BF16),
        compiler_params=_params(("arbitrary", "arbitrary")),
        name="swa_sink_attention",
    )(sinks, p_qkv, p_qkv, *gate_arrays)


def _outproj_kernel(*refs, widths, tn, emit_x):
    ny = len(widths)
    y_refs = refs[:ny]
    w_ref, x_ref, g_ref = refs[ny:ny + 3]
    ho_ref = refs[-1]
    hold_ref = refs[ny + 3] if emit_x else ho_ref
    tm, d = x_ref.shape
    ss = jnp.zeros((tm, 1), F32)
    for j in range(d // tn):
        cols = slice(j * tn, (j + 1) * tn)
        acc = x_ref[:, cols]
        off = 0
        for y_ref, wd in zip(y_refs, widths):
            acc = acc + jnp.dot(y_ref[...], w_ref[off:off + wd, cols], preferred_element_type=F32)
            off += wd
        hold_ref[:, cols] = acc
        ss = ss + jnp.sum(acc * acc, axis=1, keepdims=True)
    r = lax.rsqrt(ss * (1.0 / d) + EPS)
    for j in range(d // tn):
        cols = slice(j * tn, (j + 1) * tn)
        ho_ref[:, cols] = ((hold_ref[:, cols] * r) * g_ref[:, cols]).astype(ho_ref.dtype)


def _outproj(ys, w_bf16, x2d, g_next, final):
    t, d = x2d.shape
    widths = tuple(y.shape[1] for y in ys)
    kdim = w_bf16.shape[0]
    assert sum(widths) == kdim and w_bf16.shape[1] == d
    tm = min(t, OUT_TM)
    tn = min(d, OUT_TN)
    row_spec = pl.BlockSpec((tm, d), lambda i: (i, 0))
    in_specs = [pl.BlockSpec((tm, wd), lambda i: (i, 0)) for wd in widths]
    in_specs += [pl.BlockSpec((kdim, d), lambda i: (0, 0)), row_spec,
                 pl.BlockSpec((1, d), lambda i: (0, 0))]
    if final:
        out_specs = [row_spec]
        out_shape = [jax.ShapeDtypeStruct((t, d), F32)]
    else:
        out_specs = [row_spec, row_spec]
        out_shape = [jax.ShapeDtypeStruct((t, d), F32), jax.ShapeDtypeStruct((t, d), BF16)]
    res = pl.pallas_call(
        functools.partial(_outproj_kernel, widths=widths, tn=tn, emit_x=not final),
        grid=(t // tm,),
        in_specs=in_specs,
        out_specs=out_specs,
        out_shape=out_shape,
        compiler_params=_params(("arbitrary",)),
        name="outproj_final" if final else "outproj",
    )(*ys, w_bf16, x2d, g_next.reshape(1, d))
    return res[0] if final else (res[0], res[1])


def kernel(x, mem, positions, even_norm, even_w_in, even_w_mem_kv, even_w_out, odd_norm, odd_w_in,
           odd_w_mem_kv, odd_w_out, odd_sinks, mem_norm, final_norm):
    b, s, d = x.shape
    t = b * s
    depth = even_norm.shape[0] + odd_norm.shape[0]
    pos_col = positions.reshape(t, 1)
    kinds_e = _even_chunk_kinds()
    kinds_o = _odd_chunk_kinds()
    mem_n = _rmsnorm(mem.reshape(b * MEM_LEN, d), mem_norm)

    x2d = x.reshape(t, d)
    h, cos_e, sin_e, cos_o, sin_o = _rmsnorm_and_rope_tables(x2d, even_norm[0], pos_col)
    out = None
    for layer in range(depth):
        idx = layer // 2
        last = layer == depth - 1
        if last:
            g_next = final_norm
        elif layer % 2 == 0:
            g_next = odd_norm[idx]
        else:
            g_next = even_norm[idx + 1]
        if layer % 2 == 0:
            memkv = _proj(mem_n, even_w_mem_kv, idx, MEMKV_TN).reshape(b, MEM_LEN, 2 * MEM_HEADS * HEAD_DIM)
            p3, w_out = _proj(h, even_w_in, idx, EVEN_TN, (kinds_e, cos_e, sin_e, HEAD_DIM), side_stack=even_w_out)
            p3 = p3.reshape(b, s, EVEN_IN)
            ys = [_dilated_attention(p3), _moba_attention(p3),
                  _mem_attention([(p3, 0)], memkv, E_QM, E_GATE + 2 * A_HEADS)]
        else:
            memkv = _proj(mem_n, odd_w_mem_kv, idx, MEMKV_TN).reshape(b, MEM_LEN, 2 * MEM_HEADS * HEAD_DIM)
            wide = ODD_SPLIT_TILES * ODD_TN
            pa, w_out = _proj(h, odd_w_in, idx, wide // 2, (kinds_o[:wide // LANES], cos_o, sin_o, C_HEAD_DIM),
                              n_tiles=2, tm=ODD_WIDE_TM, side_stack=odd_w_out)
            pa = pa.reshape(b, s, wide)
            pb = _proj(h, odd_w_in, idx, ODD_TN, first_tile=ODD_SPLIT_TILES).reshape(b, s, ODD_IN - wide)
            parts = [(pa, 0), (pb, wide // LANES)]
            ys = [_swa_attention(parts, odd_sinks[idx]), _mem_attention(parts, memkv, O_QM, O_GATE + SWA_GATE_BLOCKS)]
        ys = [y.reshape(t, y.shape[-1]) for y in ys]
        if last:
            out = _outproj(ys, w_out, x2d, g_next, final=True)
        else:
            x2d, h = _outproj(ys, w_out, x2d, g_next, final=False)
    return out.reshape(b, s, d)
```

```python
import functools
import math

import jax
import jax.numpy as jnp
from jax import lax
from jax.experimental import pallas as pl
from jax.experimental.pallas import tpu as pltpu

F32 = jnp.float32
BF16 = jnp.bfloat16

LANES = 128
HEAD_DIM = 128
MEM_LEN = 256
MEM_HEADS = 4
A_HEADS = 6
B_HEADS = 6
MOBA_BLOCK = 256
MOBA_TOPK = 3
C_HEADS = 24
C_KV_HEADS = 3
C_HEAD_DIM = 64
C_WINDOW = 128
ROPE_THETA = 10000.0
EPS = 1e-6
NEG = -1e30
LOG2E = 1.4426950408889634

EVEN_WIDTH = (A_HEADS + B_HEADS + MEM_HEADS) * HEAD_DIM
ODD_WIDTH = C_HEADS * C_HEAD_DIM + MEM_HEADS * HEAD_DIM
EVEN_IN = 3 * A_HEADS * HEAD_DIM + 3 * B_HEADS * HEAD_DIM + MEM_HEADS * HEAD_DIM + EVEN_WIDTH
ODD_IN = C_HEADS * C_HEAD_DIM + 2 * C_KV_HEADS * C_HEAD_DIM + MEM_HEADS * HEAD_DIM + ODD_WIDTH

E_QA, E_KA, E_VA = 0, A_HEADS, 2 * A_HEADS
E_QB, E_KB, E_VB = 3 * A_HEADS, 3 * A_HEADS + B_HEADS, 3 * A_HEADS + 2 * B_HEADS
E_QM = 3 * A_HEADS + 3 * B_HEADS
E_GATE = E_QM + MEM_HEADS
O_Q = 0
O_KV = (C_HEADS * C_HEAD_DIM) // LANES
O_QM = O_KV + (2 * C_KV_HEADS * C_HEAD_DIM) // LANES
O_GATE = O_QM + MEM_HEADS

PROJ_TM = 1024
EVEN_TN = 1024
MEMKV_TN = 512
ODD_TN = 896
ODD_SPLIT_TILES = 4
ODD_WIDE_TM = 512
OUT_TM, OUT_TN = 512, 512
NORM_TM = 512
ATTN_BLK = 256
MEM_TQ = 512
SWA_TQ = 1024

VMEM_LIMIT = 56 * 1024 * 1024


def _params(sem):
    return pltpu.CompilerParams(dimension_semantics=sem, vmem_limit_bytes=VMEM_LIMIT)


def _silu(g):
    half = 0.5 * g
    return half + half * jnp.tanh(half)


def _rmsnorm_rope_kernel(x_ref, g_ref, pos_ref, invf_ref, o_ref, c_full_ref, s_full_ref, c_half_ref, s_half_ref):
    _rmsnorm_kernel(x_ref, g_ref, o_ref)
    ang = pos_ref[...].astype(F32) * invf_ref[...]
    cos = jnp.cos(ang)
    sin = jnp.sin(ang)
    lane = lax.broadcasted_iota(jnp.int32, cos.shape, 1)
    half, quarter = LANES // 2, LANES // 4

    def spread(t, sign):
        swapped = pltpu.roll(t, half, 1)
        full = jnp.where(lane < half, sign * t, swapped)
        narrow = jnp.where(lane < quarter, sign * swapped,
                           jnp.where(lane < half, pltpu.roll(t, LANES - quarter, 1),
                                     jnp.where(lane < half + quarter, sign * t, pltpu.roll(t, quarter, 1))))
        return full, narrow

    c_full_ref[...], c_half_ref[...] = spread(cos, 1.0)
    s_full_ref[...], s_half_ref[...] = spread(sin, -1.0)


def _rmsnorm_and_rope_tables(x2d, g, pos_col):
    t, d = x2d.shape

    def inv_freq(head_dim):
        return jnp.exp(jnp.arange(head_dim // 2, dtype=F32) * (-2.0 * math.log(ROPE_THETA) / head_dim))

    invf = jnp.concatenate([inv_freq(HEAD_DIM), inv_freq(C_HEAD_DIM),
                            jnp.zeros((LANES - HEAD_DIM // 2 - C_HEAD_DIM // 2,), F32)])[None, :]
    tm = min(t, NORM_TM)
    tab = pl.BlockSpec((tm, LANES), lambda i: (i, 0))
    return pl.pallas_call(
        _rmsnorm_rope_kernel,
        grid=(t // tm,),
        in_specs=[pl.BlockSpec((tm, d), lambda i: (i, 0)),
                  pl.BlockSpec((1, d), lambda i: (0, 0)),
                  pl.BlockSpec((tm, 1), lambda i: (i, 0)),
                  pl.BlockSpec((1, LANES), lambda i: (0, 0))],
        out_specs=[pl.BlockSpec((tm, d), lambda i: (i, 0))] + [tab] * 4,
        out_shape=[jax.ShapeDtypeStruct((t, d), BF16)] + [jax.ShapeDtypeStruct((t, LANES), F32)] * 4,
        compiler_params=_params(("arbitrary",)),
        name="rmsnorm_rope_tables",
    )(x2d, g.reshape(1, d), pos_col, invf)


def _rmsnorm_kernel(x_ref, g_ref, o_ref):
    x = x_ref[...]
    ms = jnp.mean(x * x, axis=-1, keepdims=True)
    o_ref[...] = ((x * lax.rsqrt(ms + EPS)) * g_ref[...]).astype(o_ref.dtype)


def _rmsnorm(x2d, g):
    t, d = x2d.shape
    tm = min(t, NORM_TM)
    return pl.pallas_call(
        _rmsnorm_kernel,
        grid=(t // tm,),
        in_specs=[pl.BlockSpec((tm, d), lambda i: (i, 0)),
                  pl.BlockSpec((1, d), lambda i: (0, 0))],
        out_specs=pl.BlockSpec((tm, d), lambda i: (i, 0)),
        out_shape=jax.ShapeDtypeStruct((t, d), BF16),
        compiler_params=_params(("arbitrary",)),
        name="rmsnorm",
    )(x2d, g.reshape(1, d))


ROPE_NONE, ROPE_FULL, ROPE_FIRST_HALF = 0, 1, 2


def _rope_partner(x, head_dim):
    if head_dim == LANES:
        return pltpu.roll(x, LANES // 2, 1)
    lane = lax.broadcasted_iota(jnp.int32, x.shape, 1)
    half = head_dim // 2
    return jnp.where((lane % head_dim) < half, pltpu.roll(x, LANES - half, 1), pltpu.roll(x, half, 1))


def _proj_kernel(types_ref, h_ref, w_ref, cos_ref, sin_ref, side_ref, o_ref, side_o_ref, wbf_ref, *, tn, head_dim):
    j = pl.program_id(0)
    i = pl.program_id(1)
    side_o_ref[...] = side_ref[...].astype(side_o_ref.dtype)

    @pl.when(i == 0)
    def _():
        wbf_ref[...] = w_ref[...].astype(BF16)

    nchunk = tn // LANES
    any_rope = types_ref[pl.num_programs(0) * nchunk + j]

    @pl.when(any_rope == 0)
    def _():
        o_ref[...] = jnp.dot(h_ref[...], wbf_ref[...], preferred_element_type=F32).astype(o_ref.dtype)

    @pl.when(any_rope != 0)
    def _():
        acc = jnp.dot(h_ref[...], wbf_ref[...], preferred_element_type=F32)
        cos = cos_ref[...]
        sin = sin_ref[...]
        upper = lax.broadcasted_iota(jnp.int32, cos.shape, 1) >= LANES // 2
        for c in range(nchunk):
            kind = types_ref[j * nchunk + c]
            x = acc[:, c * LANES:(c + 1) * LANES]
            roped = x * cos + _rope_partner(x, head_dim) * sin
            plain = jnp.logical_or(kind == ROPE_NONE, jnp.logical_and(kind == ROPE_FIRST_HALF, upper))
            o_ref[:, c * LANES:(c + 1) * LANES] = jnp.where(plain, x, roped).astype(o_ref.dtype)


def _proj_plain_kernel(h_ref, w_ref, o_ref, wbf_ref):
    @pl.when(pl.program_id(1) == 0)
    def _():
        wbf_ref[...] = w_ref[...].astype(BF16)

    o_ref[...] = jnp.dot(h_ref[...], wbf_ref[...], preferred_element_type=F32).astype(o_ref.dtype)


def _proj(h, w_stack, idx, tn, rope=None, first_tile=0, n_tiles=None, tm=PROJ_TM, side_stack=None):
    t, d = h.shape
    if n_tiles is None:
        n_tiles = w_stack.shape[2] // tn - first_tile
    assert tn % LANES == 0 and (first_tile + n_tiles) * tn <= w_stack.shape[2]
    n = n_tiles * tn
    tm = min(t, tm)
    grid = (n_tiles, t // tm)
    out_shape = jax.ShapeDtypeStruct((t, n), BF16)
    scratch = [pltpu.VMEM((d, tn), BF16)]
    if rope is None:
        return pl.pallas_call(
            _proj_plain_kernel,
            grid=grid,
            in_specs=[pl.BlockSpec((tm, d), lambda j, i: (i, 0)),
                      pl.BlockSpec((None, d, tn), lambda j, i: (idx, 0, first_tile + j))],
            out_specs=pl.BlockSpec((tm, tn), lambda j, i: (i, j)),
            out_shape=out_shape,
            scratch_shapes=scratch,
            compiler_params=_params(("arbitrary", "arbitrary")),
            name="proj_plain",
        )(h, w_stack)
    kinds, cos, sin, head_dim = rope
    per_tile = kinds.reshape(n_tiles, tn // LANES)
    kinds = jnp.concatenate([kinds, (per_tile != ROPE_NONE).any(axis=1).astype(jnp.int32)])
    side_rows, side_cols = side_stack.shape[1:]
    n_i = t // tm
    n_side = 1 << ((n_tiles * n_i).bit_length() - 1)
    assert side_rows % n_side == 0 and (side_rows // n_side) % 16 == 0

    def side_block(j, i):
        return jnp.minimum(j * n_i + i, n_side - 1)

    return pl.pallas_call(
        functools.partial(_proj_kernel, tn=tn, head_dim=head_dim),
        grid_spec=pltpu.PrefetchScalarGridSpec(
            num_scalar_prefetch=1,
            grid=grid,
            in_specs=[pl.BlockSpec((tm, d), lambda j, i, k: (i, 0)),
                      pl.BlockSpec((None, d, tn), lambda j, i, k: (idx, 0, first_tile + j)),
                      pl.BlockSpec((tm, LANES), lambda j, i, k: (i, 0)),
                      pl.BlockSpec((tm, LANES), lambda j, i, k: (i, 0)),
                      pl.BlockSpec((None, side_rows // n_side, side_cols),
                                   lambda j, i, k: (idx, side_block(j, i), 0))],
            out_specs=[pl.BlockSpec((tm, tn), lambda j, i, k: (i, j)),
                       pl.BlockSpec((side_rows // n_side, side_cols), lambda j, i, k: (side_block(j, i), 0))],
            scratch_shapes=scratch),
        out_shape=[out_shape, jax.ShapeDtypeStruct((side_rows, side_cols), BF16)],
        compiler_params=_params(("arbitrary", "arbitrary")),
        name=f"proj_rope{head_dim}",
    )(kinds, h, w_stack, cos, sin, side_stack)


def _even_chunk_kinds():
    kinds = [ROPE_NONE] * (EVEN_IN // LANES)
    for start in (E_QA, E_KA, E_QB, E_KB):
        for c in range(start, start + A_HEADS):
            kinds[c] = ROPE_FULL
    return jnp.asarray(kinds, jnp.int32)


def _odd_chunk_kinds():
    kinds = [ROPE_NONE] * (ODD_IN // LANES)
    k_end = C_HEADS * C_HEAD_DIM + C_KV_HEADS * C_HEAD_DIM
    for c in range(len(kinds)):
        if (c + 1) * LANES <= k_end:
            kinds[c] = ROPE_FULL
        elif c * LANES < k_end:
            assert k_end - c * LANES == LANES // 2
            kinds[c] = ROPE_FIRST_HALF
    return jnp.asarray(kinds, jnp.int32)


def _locate(parts, block):
    arr, first = [pt for pt in parts if pt[1] <= block][-1]
    assert block - first < arr.shape[2] // LANES
    return arr, block - first


def _scores_t(k, q):
    return lax.dot_general(k, q, (((1,), (1,)), ((), ())), preferred_element_type=F32)


def _transpose_bf16(x):
    return x.astype(F32).T.astype(BF16)


def _softmax_probs_t(tiles, biases, weights, c):
    m = None
    for s, b in zip(tiles, biases):
        if b is None:
            mt = jnp.max(s, axis=0, keepdims=True)
        elif b.shape[0] == 1:
            mt = jnp.max(s, axis=0, keepdims=True) + b
        else:
            mt = jnp.max(s + b, axis=0, keepdims=True)
        m = mt if m is None else jnp.maximum(m, mt)
    shifts = {}
    l = None
    ps = []
    for s, b, w in zip(tiles, biases, weights):
        if b is None:
            shift = -m
        else:
            if id(b) not in shifts:
                shifts[id(b)] = b - m
            shift = shifts[id(b)]
        p = jnp.exp2((s + shift) * c)
        if w is not None:
            p = p * w
        lt = jnp.sum(p, axis=0, keepdims=True)
        l = lt if l is None else l + lt
        ps.append(p.astype(BF16))
    p_all = ps[0] if len(ps) == 1 else jnp.concatenate(ps, axis=0)
    return p_all, 1.0 / l


SCORE_LOOKAHEAD = 2
DILATED_LOOKAHEAD = 3


def _interleave(generators):
    live = list(generators)
    while live:
        for gen in list(live):
            try:
                next(gen)
            except StopIteration:
                live.remove(gen)


HEADS_PER_STEP = 2


def _head_group_spec(s, first, heads):
    assert first % heads == 0
    return pl.BlockSpec((None, s, heads * LANES), lambda bi, h: (bi, 0, first // heads + h))


def _tile_delta(blk):
    row = lax.broadcasted_iota(jnp.int32, (blk, blk), 0)
    col = lax.broadcasted_iota(jnp.int32, (blk, blk), 1)
    return col - row


def _dilated_head(q_ref, k_ref, v_ref, g_ref, o_ref, cols, masks, *, blk, nblk, c, ahead):
    vt = _transpose_bf16(v_ref[:, cols])
    nears, far_bias = masks

    def scores(u):
        return _scores_t(k_ref[0:(u + 1) * blk, cols], q_ref[u * blk:(u + 1) * blk, cols])

    sts = [scores(u) for u in range(min(ahead, nblk))]
    yield
    for i in range(nblk):
        rows = slice(i * blk, (i + 1) * blk)
        ext = (i + 1) * blk
        if i + ahead < nblk:
            sts.append(scores(i + ahead))
        st = sts[i]
        tiles, biases, weights = [], [], []
        for n in range(i + 1):
            tiles.append(st[n * blk:(n + 1) * blk])
            if i - n < len(nears):
                bias, cnt = nears[i - n]
                biases.append(bias)
                weights.append(cnt)
            else:
                biases.append(far_bias)
                weights.append(None)
        p_all, inv_l = _softmax_probs_t(tiles, biases, weights, c)
        ot = jnp.dot(vt[:, 0:ext], p_all, preferred_element_type=F32) * inv_l
        o_ref[rows, cols] = (ot.T * _silu(g_ref[rows, cols].astype(F32))).astype(o_ref.dtype)
        yield


def _dilated_kernel(q_ref, k_ref, v_ref, g_ref, o_ref, *, blk, nblk, c, ahead, heads):
    cr = _tile_delta(blk)

    def near(delta):
        dist = cr + delta * blk
        cnt = (jnp.where(dist <= 128, 1.0, 0.0)
               + jnp.where(jnp.logical_and((dist & 3) == 0, dist <= 512), 1.0, 0.0)
               + jnp.where((dist & 15) == 0, 1.0, 0.0))
        cnt = jnp.where(dist >= 0, cnt, 0.0)
        return jnp.where(cnt > 0.0, 0.0, NEG), cnt

    n_near = -(-512 // blk) + 1
    nears = [near(d) for d in range(min(n_near, nblk))]
    far_bias = jnp.where((cr & 15) == 0, 0.0, NEG)
    masks = (nears, far_bias)
    _interleave([_dilated_head(q_ref, k_ref, v_ref, g_ref, o_ref, slice(hh * LANES, (hh + 1) * LANES), masks,
                               blk=blk, nblk=nblk, c=c, ahead=ahead) for hh in range(heads)])


def _dilated_attention(p3):
    b, s, _ = p3.shape
    blk = ATTN_BLK
    heads = HEADS_PER_STEP
    assert s % blk == 0 and A_HEADS % heads == 0
    return pl.pallas_call(
        functools.partial(_dilated_kernel, blk=blk, nblk=s // blk, c=HEAD_DIM ** -0.5 * LOG2E,
                          ahead=DILATED_LOOKAHEAD, heads=heads),
        grid=(b, A_HEADS // heads),
        in_specs=[_head_group_spec(s, first, heads) for first in (E_QA, E_KA, E_VA, E_GATE)],
        out_specs=_head_group_spec(s, 0, heads),
        out_shape=jax.ShapeDtypeStruct((b, s, A_HEADS * HEAD_DIM), BF16),
        compiler_params=_params(("arbitrary", "arbitrary")),
        name="dilated_attention",
    )(p3, p3, p3, p3)


def _moba_head(q_ref, k_ref, v_ref, g_ref, o_ref, cols, *, blk, nblk, c):
    s_len = nblk * blk
    q = q_ref[:, cols]

    def scores(u):
        return _scores_t(k_ref[0:(u + 1) * blk, cols], q[u * blk:(u + 1) * blk, :])

    sts = [scores(u) for u in range(min(SCORE_LOOKAHEAD, nblk))]
    vt = _transpose_bf16(v_ref[:, cols])

    km = jnp.concatenate(
        [jnp.sum(k_ref[n * blk:(n + 1) * blk, cols].astype(F32), axis=0, keepdims=True) * (1.0 / blk)
         for n in range(nblk)], axis=0)
    hi = km.astype(BF16).astype(F32)
    ksplit = jnp.concatenate([hi, km - hi], axis=0).astype(BF16)
    gt2 = _scores_t(ksplit, q)
    gate = gt2[:nblk] + gt2[nblk:]

    nid = lax.broadcasted_iota(jnp.int32, (nblk, s_len), 0)
    own = jnp.right_shift(lax.broadcasted_iota(jnp.int32, (nblk, s_len), 1), blk.bit_length() - 1)
    rank = jnp.zeros((nblk, s_len), jnp.int32)
    for mm in range(nblk):
        gm = gate[mm:mm + 1, :]
        beats = jnp.logical_or(gm > gate, jnp.logical_and(gm == gate, mm < nid))
        rank = rank + jnp.where(jnp.logical_and(beats, mm < own), 1, 0)
    sel_bias = jnp.where(jnp.logical_and(rank < MOBA_TOPK, nid < own), 0.0, NEG)
    causal_bias = jnp.where(_tile_delta(blk) >= 0, 0.0, NEG)
    yield

    for i in range(nblk):
        rows = slice(i * blk, (i + 1) * blk)
        ext = (i + 1) * blk
        if i + SCORE_LOOKAHEAD < nblk:
            sts.append(scores(i + SCORE_LOOKAHEAD))
        st = sts[i]
        tiles = [st[n * blk:(n + 1) * blk] for n in range(i + 1)]
        biases = [sel_bias[n:n + 1, rows] for n in range(i)] + [causal_bias]
        p_all, inv_l = _softmax_probs_t(tiles, biases, [None] * (i + 1), c)
        ot = jnp.dot(vt[:, 0:ext], p_all, preferred_element_type=F32) * inv_l
        o_ref[rows, cols] = (ot.T * _silu(g_ref[rows, cols].astype(F32))).astype(o_ref.dtype)
        yield


def _moba_kernel(q_ref, k_ref, v_ref, g_ref, o_ref, *, blk, nblk, c, heads):
    _interleave([_moba_head(q_ref, k_ref, v_ref, g_ref, o_ref, slice(hh * LANES, (hh + 1) * LANES),
                            blk=blk, nblk=nblk, c=c) for hh in range(heads)])


def _moba_attention(p3):
    b, s, _ = p3.shape
    blk = MOBA_BLOCK
    nblk = s // blk
    heads = HEADS_PER_STEP
    assert nblk >= MOBA_TOPK and blk & (blk - 1) == 0 and B_HEADS % heads == 0
    return pl.pallas_call(
        functools.partial(_moba_kernel, blk=blk, nblk=nblk, c=HEAD_DIM ** -0.5 * LOG2E, heads=heads),
        grid=(b, B_HEADS // heads),
        in_specs=[_head_group_spec(s, first, heads) for first in (E_QB, E_KB, E_VB, E_GATE + A_HEADS)],
        out_specs=_head_group_spec(s, 0, heads),
        out_shape=jax.ShapeDtypeStruct((b, s, B_HEADS * HEAD_DIM), BF16),
        compiler_params=_params(("arbitrary", "arbitrary")),
        name="moba_attention",
    )(p3, p3, p3, p3)


def _mem_kernel(*refs, tq, c):
    q_refs = refs[:MEM_HEADS]
    g_refs = refs[MEM_HEADS:2 * MEM_HEADS]
    memkv_ref, o_ref = refs[2 * MEM_HEADS:]
    kvw = MEM_HEADS * HEAD_DIM
    units = [(h, j) for h in range(MEM_HEADS) for j in range(o_ref.shape[0] // tq)]

    def scores(unit):
        h, j = unit
        return _scores_t(memkv_ref[:, h * HEAD_DIM:(h + 1) * HEAD_DIM], q_refs[h][j * tq:(j + 1) * tq, :])

    sts = [scores(u) for u in units[:SCORE_LOOKAHEAD]]
    mvts = [_transpose_bf16(memkv_ref[:, kvw + h * HEAD_DIM:kvw + (h + 1) * HEAD_DIM]) for h in range(MEM_HEADS)]
    for ui, (h, j) in enumerate(units):
        rows = slice(j * tq, (j + 1) * tq)
        if ui + SCORE_LOOKAHEAD < len(units):
            sts.append(scores(units[ui + SCORE_LOOKAHEAD]))
        p_all, inv_l = _softmax_probs_t([sts[ui]], [None], [None], c)
        ot = jnp.dot(mvts[h], p_all, preferred_element_type=F32) * inv_l
        gate = g_refs[h][rows, :].astype(F32)
        o_ref[rows, h * HEAD_DIM:(h + 1) * HEAD_DIM] = (ot.T * _silu(gate)).astype(o_ref.dtype)


def _mem_attention(parts, memkv, q_block, gate_block):
    b, s, _ = parts[0][0].shape
    arrays, specs = [], []
    for first in (q_block, gate_block):
        for h in range(MEM_HEADS):
            arr, local = _locate(parts, first + h)
            arrays.append(arr)
            specs.append(pl.BlockSpec((None, s, LANES), functools.partial(lambda bi, c: (bi, 0, c), c=local)))
    width = MEM_HEADS * HEAD_DIM
    return pl.pallas_call(
        functools.partial(_mem_kernel, tq=min(s, MEM_TQ), c=HEAD_DIM ** -0.5 * LOG2E),
        grid=(b,),
        in_specs=specs + [pl.BlockSpec((None, MEM_LEN, 2 * width), lambda bi: (bi, 0, 0))],
        out_specs=pl.BlockSpec((None, s, width), lambda bi: (bi, 0, 0)),
        out_shape=jax.ShapeDtypeStruct((b, s, width), BF16),
        compiler_params=_params(("arbitrary",)),
        name="memory_attention",
    )(*arrays, memkv)


SWA_QBLK = C_WINDOW
SWA_GATE_BLOCKS = (C_HEADS * C_HEAD_DIM) // LANES
SWA_VT_WIDTH = 2 * LANES


def _swa_kernel(sinks_ref, q_ref, kv_ref, *rest, tq, c, scale):
    gate_refs = rest[:SWA_GATE_BLOCKS]
    o_ref, kvt_ref = rest[SWA_GATE_BLOCKS:]
    i = pl.program_id(1)
    heads_per_group = C_HEADS // C_KV_HEADS
    kw = C_KV_HEADS * C_HEAD_DIM

    @pl.when(i == 0)
    def _():
        for ch in range(kv_ref.shape[0] // LANES):
            t = kv_ref[ch * LANES:(ch + 1) * LANES, LANES:LANES + SWA_VT_WIDTH].astype(F32)
            kvt_ref[ch] = t.T.astype(BF16)

    krow = lax.broadcasted_iota(jnp.int32, (C_WINDOW, SWA_QBLK), 0)
    qcol = lax.broadcasted_iota(jnp.int32, (C_WINDOW, SWA_QBLK), 1)
    from_prev = krow > qcol
    zero_b = jnp.zeros((C_WINDOW, SWA_QBLK), BF16)
    shift = LANES.bit_length() - 1
    units = [(sub, g) for sub in range(tq // SWA_QBLK) for g in range(C_KV_HEADS)]

    def chunks(sub):
        r0 = i * tq + sub * SWA_QBLK
        prev = pl.multiple_of(jnp.maximum(r0 - C_WINDOW, 0), LANES)
        own = pl.multiple_of(r0, LANES)
        return r0, prev, own

    def scores(unit):
        sub, g = unit
        r0, prev, own = chunks(sub)
        rows = slice(sub * SWA_QBLK, (sub + 1) * SWA_QBLK)
        kcols = slice(g * C_HEAD_DIM, (g + 1) * C_HEAD_DIM)
        qg = jnp.concatenate([q_ref[rows, hd * C_HEAD_DIM:(hd + 1) * C_HEAD_DIM]
                              for hd in range(g * heads_per_group, (g + 1) * heads_per_group)], axis=0)
        s_prev = _scores_t(kv_ref[pl.ds(prev, C_WINDOW), kcols], qg)
        if sub == 0:
            s_prev = s_prev + jnp.where(r0 > 0, 0.0, NEG)
        return s_prev, _scores_t(kv_ref[pl.ds(own, C_WINDOW), kcols], qg)

    sts = [scores(u) for u in units[:SCORE_LOOKAHEAD]]
    for ui, (sub, g) in enumerate(units):
        if ui + SCORE_LOOKAHEAD < len(units):
            sts.append(scores(units[ui + SCORE_LOOKAHEAD]))
        s_prev, s_own = sts[ui]
        _, prev, own = chunks(sub)
        rows = slice(sub * SWA_QBLK, (sub + 1) * SWA_QBLK)
        v0 = (kw + g * C_HEAD_DIM) - LANES
        vt_prev = kvt_ref[jnp.right_shift(prev, shift)][v0:v0 + C_HEAD_DIM, :]
        vt_own = kvt_ref[jnp.right_shift(own, shift)][v0:v0 + C_HEAD_DIM, :]
        p_prev, p_own, inv_l = [], [], []
        for u in range(heads_per_group):
            cols = slice(u * SWA_QBLK, (u + 1) * SWA_QBLK)
            s_h = jnp.where(from_prev, s_prev[:, cols], s_own[:, cols])
            sink = sinks_ref[g * heads_per_group + u] * (1.0 / scale)
            m = jnp.maximum(jnp.max(s_h, axis=0, keepdims=True), sink)
            p = jnp.exp2((s_h - m) * c)
            inv_l.append(1.0 / (jnp.sum(p, axis=0, keepdims=True) + jnp.exp2((sink - m) * c)))
            pb = p.astype(BF16)
            p_prev.append(jnp.where(from_prev, pb, zero_b))
            p_own.append(jnp.where(from_prev, zero_b, pb))
        ot = (jnp.dot(vt_prev, jnp.concatenate(p_prev, axis=1), preferred_element_type=F32)
              + jnp.dot(vt_own, jnp.concatenate(p_own, axis=1), preferred_element_type=F32))
        ot = ot * jnp.concatenate(inv_l, axis=1)
        for pair in range(heads_per_group // 2):
            both = jnp.concatenate([ot[:, (2 * pair + u) * SWA_QBLK:(2 * pair + u + 1) * SWA_QBLK]
                                    for u in range(2)], axis=0)
            blk_i = (g * heads_per_group) // 2 + pair
            gate = gate_refs[blk_i][rows, :].astype(F32)
            o_ref[rows, blk_i * LANES:(blk_i + 1) * LANES] = (both.T * _silu(gate)).astype(o_ref.dtype)


def _swa_attention(parts, sinks):
    p_qkv = parts[0][0]
    b, s, _ = p_qkv.shape
    tq = min(s, SWA_TQ)
    qw = C_HEADS * C_HEAD_DIM
    kvw = 2 * C_KV_HEADS * C_HEAD_DIM
    assert (O_KV * LANES) % kvw == 0 and kvw == LANES + SWA_VT_WIDTH and p_qkv.shape[2] >= qw + kvw
    gate_arrays, gate_specs = [], []
    for u in range(SWA_GATE_BLOCKS):
        arr, local = _locate(parts, O_GATE + u)
        gate_arrays.append(arr)
        gate_specs.append(pl.BlockSpec((None, tq, LANES),
                                       functools.partial(lambda bi, i, sk, c: (bi, i, c), c=local)))
    scale = C_HEAD_DIM ** -0.5
    return pl.pallas_call(
        functools.partial(_swa_kernel, tq=tq, c=scale * LOG2E, scale=scale),
        grid_spec=pltpu.PrefetchScalarGridSpec(
            num_scalar_prefetch=1,
            grid=(b, s // tq),
            in_specs=[pl.BlockSpec((None, tq, qw), lambda bi, i, sk: (bi, i, 0)),
                      pl.BlockSpec((None, s, kvw), lambda bi, i, sk: (bi, 0, (O_KV * LANES) // kvw))]
                     + gate_specs,
            out_specs=pl.BlockSpec((None, tq, qw), lambda bi, i, sk: (bi, i, 0)),
            scratch_shapes=[pltpu.VMEM((s // LANES, SWA_VT_WIDTH, LANES), BF16)]),
        out_shape=jax.ShapeDtypeStruct((b, s, qw), BF16),
        compiler_params=_params(("arbitrary", "arbitrary")),
        name="swa_sink_attention",
    )(sinks, p_qkv, p_qkv, *gate_arrays)


def _outproj_kernel(*refs, widths, tn, emit_x):
    ny = len(widths)
    y_refs = refs[:ny]
    w_ref, x_ref, g_ref = refs[ny:ny + 3]
    ho_ref = refs[-1]
    hold_ref = refs[ny + 3] if emit_x else ho_ref
    tm, d = x_ref.shape
    ss = jnp.zeros((tm, 1), F32)
    for j in range(d // tn):
        cols = slice(j * tn, (j + 1) * tn)
        acc = x_ref[:, cols]
        off = 0
        for y_ref, wd in zip(y_refs, widths):
            acc = acc + jnp.dot(y_ref[...], w_ref[off:off + wd, cols], preferred_element_type=F32)
            off += wd
        hold_ref[:, cols] = acc
        ss = ss + jnp.sum(acc * acc, axis=1, keepdims=True)
    r = lax.rsqrt(ss * (1.0 / d) + EPS)
    for j in range(d // tn):
        cols = slice(j * tn, (j + 1) * tn)
        ho_ref[:, cols] = ((hold_ref[:, cols] * r) * g_ref[:, cols]).astype(ho_ref.dtype)


def _outproj(ys, w_bf16, x2d, g_next, final):
    t, d = x2d.shape
    widths = tuple(y.shape[1] for y in ys)
    kdim = w_bf16.shape[0]
    assert sum(widths) == kdim and w_bf16.shape[1] == d
    tm = min(t, OUT_TM)
    tn = min(d, OUT_TN)
    row_spec = pl.BlockSpec((tm, d), lambda i: (i, 0))
    in_specs = [pl.BlockSpec((tm, wd), lambda i: (i, 0)) for wd in widths]
    in_specs += [pl.BlockSpec((kdim, d), lambda i: (0, 0)), row_spec,
                 pl.BlockSpec((1, d), lambda i: (0, 0))]
    if final:
        out_specs = [row_spec]
        out_shape = [jax.ShapeDtypeStruct((t, d), F32)]
    else:
        out_specs = [row_spec, row_spec]
        out_shape = [jax.ShapeDtypeStruct((t, d), F32), jax.ShapeDtypeStruct((t, d), BF16)]
    res = pl.pallas_call(
        functools.partial(_outproj_kernel, widths=widths, tn=tn, emit_x=not final),
        grid=(t // tm,),
        in_specs=in_specs,
        out_specs=out_specs,
        out_shape=out_shape,
        compiler_params=_params(("arbitrary",)),
        name="outproj_final" if final else "outproj",
    )(*ys, w_bf16, x2d, g_next.reshape(1, d))
    return res[0] if final else (res[0], res[1])


def kernel(x, mem, positions, even_norm, even_w_in, even_w_mem_kv, even_w_out, odd_norm, odd_w_in,
           odd_w_mem_kv, odd_w_out, odd_sinks, mem_norm, final_norm):
    b, s, d = x.shape
    t = b * s
    depth = even_norm.shape[0] + odd_norm.shape[0]
    pos_col = positions.reshape(t, 1)
    kinds_e = _even_chunk_kinds()
    kinds_o = _odd_chunk_kinds()
    mem_n = _rmsnorm(mem.reshape(b * MEM_LEN, d), mem_norm)

    x2d = x.reshape(t, d)
    h, cos_e, sin_e, cos_o, sin_o = _rmsnorm_and_rope_tables(x2d, even_norm[0], pos_col)
    out = None
    for layer in range(depth):
        idx = layer // 2
        last = layer == depth - 1
        if last:
            g_next = final_norm
        elif layer % 2 == 0:
            g_next = odd_norm[idx]
        else:
            g_next = even_norm[idx + 1]
        if layer % 2 == 0:
            memkv = _proj(mem_n, even_w_mem_kv, idx, MEMKV_TN).reshape(b, MEM_LEN, 2 * MEM_HEADS * HEAD_DIM)
            p3, w_out = _proj(h, even_w_in, idx, EVEN_TN, (kinds_e, cos_e, sin_e, HEAD_DIM), side_stack=even_w_out)
            p3 = p3.reshape(b, s, EVEN_IN)
            ys = [_dilated_attention(p3), _moba_attention(p3),
                  _mem_attention([(p3, 0)], memkv, E_QM, E_GATE + 2 * A_HEADS)]
        else:
            memkv = _proj(mem_n, odd_w_mem_kv, idx, MEMKV_TN).reshape(b, MEM_LEN, 2 * MEM_HEADS * HEAD_DIM)
            wide = ODD_SPLIT_TILES * ODD_TN
            pa, w_out = _proj(h, odd_w_in, idx, wide // 2, (kinds_o[:wide // LANES], cos_o, sin_o, C_HEAD_DIM),
                              n_tiles=2, tm=ODD_WIDE_TM, side_stack=odd_w_out)
            pa = pa.reshape(b, s, wide)
            pb = _proj(h, odd_w_in, idx, ODD_TN, first_tile=ODD_SPLIT_TILES).reshape(b, s, ODD_IN - wide)
            parts = [(pa, 0), (pb, wide // LANES)]
            ys = [_swa_attention(parts, odd_sinks[idx]), _mem_attention(parts, memkv, O_QM, O_GATE + SWA_GATE_BLOCKS)]
        ys = [y.reshape(t, y.shape[-1]) for y in ys]
        if last:
            out = _outproj(ys, w_out, x2d, g_next, final=True)
        else:
            x2d, h = _outproj(ys, w_out, x2d, g_next, final=False)
    return out.reshape(b, s, d)
```

```python
import functools
import math

import jax
import jax.numpy as jnp
from jax import lax
from jax.experimental import pallas as pl
from jax.experimental.pallas import tpu as pltpu

F32 = jnp.float32
BF16 = jnp.bfloat16

LANES = 128
HEAD_DIM = 128
MEM_LEN = 256
MEM_HEADS = 4
A_HEADS = 6
B_HEADS = 6
MOBA_BLOCK = 256
MOBA_TOPK = 3
C_HEADS = 24
C_KV_HEADS = 3
C_HEAD_DIM = 64
C_WINDOW = 128
ROPE_THETA = 10000.0
EPS = 1e-6
NEG = -1e30
LOG2E = 1.4426950408889634

EVEN_WIDTH = (A_HEADS + B_HEADS + MEM_HEADS) * HEAD_DIM
ODD_WIDTH = C_HEADS * C_HEAD_DIM + MEM_HEADS * HEAD_DIM
EVEN_IN = 3 * A_HEADS * HEAD_DIM + 3 * B_HEADS * HEAD_DIM + MEM_HEADS * HEAD_DIM + EVEN_WIDTH
ODD_IN = C_HEADS * C_HEAD_DIM + 2 * C_KV_HEADS * C_HEAD_DIM + MEM_HEADS * HEAD_DIM + ODD_WIDTH

E_QA, E_KA, E_VA = 0, A_HEADS, 2 * A_HEADS
E_QB, E_KB, E_VB = 3 * A_HEADS, 3 * A_HEADS + B_HEADS, 3 * A_HEADS + 2 * B_HEADS
E_QM = 3 * A_HEADS + 3 * B_HEADS
E_GATE = E_QM + MEM_HEADS
O_Q = 0
O_KV = (C_HEADS * C_HEAD_DIM) // LANES
O_QM = O_KV + (2 * C_KV_HEADS * C_HEAD_DIM) // LANES
O_GATE = O_QM + MEM_HEADS

PROJ_TM = 1024
EVEN_TN = 1024
MEMKV_TN = 512
ODD_TN = 896
ODD_SPLIT_TILES = 4
ODD_WIDE_TM = 512
OUT_TM, OUT_TN = 512, 512
NORM_TM = 512
ATTN_BLK = 256
MEM_TQ = 512
SWA_TQ = 1024

VMEM_LIMIT = 56 * 1024 * 1024


def _params(sem):
    return pltpu.CompilerParams(dimension_semantics=sem, vmem_limit_bytes=VMEM_LIMIT)


def _silu(g):
    half = 0.5 * g
    return half + half * jnp.tanh(half)


def _rmsnorm_rope_kernel(x_ref, g_ref, pos_ref, invf_ref, o_ref, c_full_ref, s_full_ref, c_half_ref, s_half_ref):
    _rmsnorm_kernel(x_ref, g_ref, o_ref)
    ang = pos_ref[...].astype(F32) * invf_ref[...]
    cos = jnp.cos(ang)
    sin = jnp.sin(ang)
    lane = lax.broadcasted_iota(jnp.int32, cos.shape, 1)
    half, quarter = LANES // 2, LANES // 4

    def spread(t, sign):
        swapped = pltpu.roll(t, half, 1)
        full = jnp.where(lane < half, sign * t, swapped)
        narrow = jnp.where(lane < quarter, sign * swapped,
                           jnp.where(lane < half, pltpu.roll(t, LANES - quarter, 1),
                                     jnp.where(lane < half + quarter, sign * t, pltpu.roll(t, quarter, 1))))
        return full, narrow

    c_full_ref[...], c_half_ref[...] = spread(cos, 1.0)
    s_full_ref[...], s_half_ref[...] = spread(sin, -1.0)


def _rmsnorm_and_rope_tables(x2d, g, pos_col):
    t, d = x2d.shape

    def inv_freq(head_dim):
        return jnp.exp(jnp.arange(head_dim // 2, dtype=F32) * (-2.0 * math.log(ROPE_THETA) / head_dim))

    invf = jnp.concatenate([inv_freq(HEAD_DIM), inv_freq(C_HEAD_DIM),
                            jnp.zeros((LANES - HEAD_DIM // 2 - C_HEAD_DIM // 2,), F32)])[None, :]
    tm = min(t, NORM_TM)
    tab = pl.BlockSpec((tm, LANES), lambda i: (i, 0))
    return pl.pallas_call(
        _rmsnorm_rope_kernel,
        grid=(t // tm,),
        in_specs=[pl.BlockSpec((tm, d), lambda i: (i, 0)),
                  pl.BlockSpec((1, d), lambda i: (0, 0)),
                  pl.BlockSpec((tm, 1), lambda i: (i, 0)),
                  pl.BlockSpec((1, LANES), lambda i: (0, 0))],
        out_specs=[pl.BlockSpec((tm, d), lambda i: (i, 0))] + [tab] * 4,
        out_shape=[jax.ShapeDtypeStruct((t, d), BF16)] + [jax.ShapeDtypeStruct((t, LANES), F32)] * 4,
        compiler_params=_params(("arbitrary",)),
        name="rmsnorm_rope_tables",
    )(x2d, g.reshape(1, d), pos_col, invf)


def _rmsnorm_kernel(x_ref, g_ref, o_ref):
    x = x_ref[...]
    ms = jnp.mean(x * x, axis=-1, keepdims=True)
    o_ref[...] = ((x * lax.rsqrt(ms + EPS)) * g_ref[...]).astype(o_ref.dtype)


def _rmsnorm(x2d, g):
    t, d = x2d.shape
    tm = min(t, NORM_TM)
    return pl.pallas_call(
        _rmsnorm_kernel,
        grid=(t // tm,),
        in_specs=[pl.BlockSpec((tm, d), lambda i: (i, 0)),
                  pl.BlockSpec((1, d), lambda i: (0, 0))],
        out_specs=pl.BlockSpec((tm, d), lambda i: (i, 0)),
        out_shape=jax.ShapeDtypeStruct((t, d), BF16),
        compiler_params=_params(("arbitrary",)),
        name="rmsnorm",
    )(x2d, g.reshape(1, d))


ROPE_NONE, ROPE_FULL, ROPE_FIRST_HALF = 0, 1, 2


def _rope_partner(x, head_dim):
    if head_dim == LANES:
        return pltpu.roll(x, LANES // 2, 1)
    lane = lax.broadcasted_iota(jnp.int32, x.shape, 1)
    half = head_dim // 2
    return jnp.where((lane % head_dim) < half, pltpu.roll(x, LANES - half, 1), pltpu.roll(x, half, 1))


def _proj_kernel(types_ref, h_ref, w_ref, cos_ref, sin_ref, side_ref, o_ref, side_o_ref, wbf_ref, *, tn, head_dim):
    j = pl.program_id(0)
    i = pl.program_id(1)
    side_o_ref[...] = side_ref[...].astype(side_o_ref.dtype)

    @pl.when(i == 0)
    def _():
        wbf_ref[...] = w_ref[...].astype(BF16)

    nchunk = tn // LANES
    any_rope = types_ref[pl.num_programs(0) * nchunk + j]

    @pl.when(any_rope == 0)
    def _():
        o_ref[...] = jnp.dot(h_ref[...], wbf_ref[...], preferred_element_type=F32).astype(o_ref.dtype)

    @pl.when(any_rope != 0)
    def _():
        acc = jnp.dot(h_ref[...], wbf_ref[...], preferred_element_type=F32)
        cos = cos_ref[...]
        sin = sin_ref[...]
        upper = lax.broadcasted_iota(jnp.int32, cos.shape, 1) >= LANES // 2
        for c in range(nchunk):
            kind = types_ref[j * nchunk + c]
            x = acc[:, c * LANES:(c + 1) * LANES]
            roped = x * cos + _rope_partner(x, head_dim) * sin
            plain = jnp.logical_or(kind == ROPE_NONE, jnp.logical_and(kind == ROPE_FIRST_HALF, upper))
            o_ref[:, c * LANES:(c + 1) * LANES] = jnp.where(plain, x, roped).astype(o_ref.dtype)


def _proj_plain_kernel(h_ref, w_ref, o_ref, wbf_ref):
    @pl.when(pl.program_id(1) == 0)
    def _():
        wbf_ref[...] = w_ref[...].astype(BF16)

    o_ref[...] = jnp.dot(h_ref[...], wbf_ref[...], preferred_element_type=F32).astype(o_ref.dtype)


def _proj(h, w_stack, idx, tn, rope=None, first_tile=0, n_tiles=None, tm=PROJ_TM, side_stack=None):
    t, d = h.shape
    if n_tiles is None:
        n_tiles = w_stack.shape[2] // tn - first_tile
    assert tn % LANES == 0 and (first_tile + n_tiles) * tn <= w_stack.shape[2]
    n = n_tiles * tn
    tm = min(t, tm)
    grid = (n_tiles, t // tm)
    out_shape = jax.ShapeDtypeStruct((t, n), BF16)
    scratch = [pltpu.VMEM((d, tn), BF16)]
    if rope is None:
        return pl.pallas_call(
            _proj_plain_kernel,
            grid=grid,
            in_specs=[pl.BlockSpec((tm, d), lambda j, i: (i, 0)),
                      pl.BlockSpec((None, d, tn), lambda j, i: (idx, 0, first_tile + j))],
            out_specs=pl.BlockSpec((tm, tn), lambda j, i: (i, j)),
            out_shape=out_shape,
            scratch_shapes=scratch,
            compiler_params=_params(("arbitrary", "arbitrary")),
            name="proj_plain",
        )(h, w_stack)
    kinds, cos, sin, head_dim = rope
    per_tile = kinds.reshape(n_tiles, tn // LANES)
    kinds = jnp.concatenate([kinds, (per_tile != ROPE_NONE).any(axis=1).astype(jnp.int32)])
    side_rows, side_cols = side_stack.shape[1:]
    n_i = t // tm
    n_side = 1 << ((n_tiles * n_i).bit_length() - 1)
    assert side_rows % n_side == 0 and (side_rows // n_side) % 16 == 0

    def side_block(j, i):
        return jnp.minimum(j * n_i + i, n_side - 1)

    return pl.pallas_call(
        functools.partial(_proj_kernel, tn=tn, head_dim=head_dim),
        grid_spec=pltpu.PrefetchScalarGridSpec(
            num_scalar_prefetch=1,
            grid=grid,
            in_specs=[pl.BlockSpec((tm, d), lambda j, i, k: (i, 0)),
                      pl.BlockSpec((None, d, tn), lambda j, i, k: (idx, 0, first_tile + j)),
                      pl.BlockSpec((tm, LANES), lambda j, i, k: (i, 0)),
                      pl.BlockSpec((tm, LANES), lambda j, i, k: (i, 0)),
                      pl.BlockSpec((None, side_rows // n_side, side_cols),
                                   lambda j, i, k: (idx, side_block(j, i), 0))],
            out_specs=[pl.BlockSpec((tm, tn), lambda j, i, k: (i, j)),
                       pl.BlockSpec((side_rows // n_side, side_cols), lambda j, i, k: (side_block(j, i), 0))],
            scratch_shapes=scratch),
        out_shape=[out_shape, jax.ShapeDtypeStruct((side_rows, side_cols), BF16)],
        compiler_params=_params(("arbitrary", "arbitrary")),
        name=f"proj_rope{head_dim}",
    )(kinds, h, w_stack, cos, sin, side_stack)


def _even_chunk_kinds():
    kinds = [ROPE_NONE] * (EVEN_IN // LANES)
    for start in (E_QA, E_KA, E_QB, E_KB):
        for c in range(start, start + A_HEADS):
            kinds[c] = ROPE_FULL
    return jnp.asarray(kinds, jnp.int32)


def _odd_chunk_kinds():
    kinds = [ROPE_NONE] * (ODD_IN // LANES)
    k_end = C_HEADS * C_HEAD_DIM + C_KV_HEADS * C_HEAD_DIM
    for c in range(len(kinds)):
        if (c + 1) * LANES <= k_end:
            kinds[c] = ROPE_FULL
        elif c * LANES < k_end:
            assert k_end - c * LANES == LANES // 2
            kinds[c] = ROPE_FIRST_HALF
    return jnp.asarray(kinds, jnp.int32)


def _locate(parts, block):
    arr, first = [pt for pt in parts if pt[1] <= block][-1]
    assert block - first < arr.shape[2] // LANES
    return arr, block - first


def _scores_t(k, q):
    return lax.dot_general(k, q, (((1,), (1,)), ((), ())), preferred_element_type=F32)


def _transpose_bf16(x):
    return x.astype(F32).T.astype(BF16)


def _softmax_probs_t(tiles, biases, weights, c):
    m = None
    for s, b in zip(tiles, biases):
        if b is None:
            mt = jnp.max(s, axis=0, keepdims=True)
        elif b.shape[0] == 1:
            mt = jnp.max(s, axis=0, keepdims=True) + b
        else:
            mt = jnp.max(s + b, axis=0, keepdims=True)
        m = mt if m is None else jnp.maximum(m, mt)
    shifts = {}
    l = None
    ps = []
    for s, b, w in zip(tiles, biases, weights):
        if b is None:
            shift = -m
        else:
            if id(b) not in shifts:
                shifts[id(b)] = b - m
            shift = shifts[id(b)]
        p = jnp.exp2((s + shift) * c)
        if w is not None:
            p = p * w
        lt = jnp.sum(p, axis=0, keepdims=True)
        l = lt if l is None else l + lt
        ps.append(p.astype(BF16))
    p_all = ps[0] if len(ps) == 1 else jnp.concatenate(ps, axis=0)
    return p_all, 1.0 / l


SCORE_LOOKAHEAD = 2
DILATED_LOOKAHEAD = 3
MOBA_LOOKAHEAD = 1


def _interleave(generators):
    live = list(generators)
    while live:
        for gen in list(live):
            try:
                next(gen)
            except StopIteration:
                live.remove(gen)


HEADS_PER_STEP = 2


def _head_group_spec(s, first, heads):
    assert first % heads == 0
    return pl.BlockSpec((None, s, heads * LANES), lambda bi, h: (bi, 0, first // heads + h))


def _tile_delta(blk):
    row = lax.broadcasted_iota(jnp.int32, (blk, blk), 0)
    col = lax.broadcasted_iota(jnp.int32, (blk, blk), 1)
    return col - row


def _dilated_head(q_ref, k_ref, v_ref, g_ref, o_ref, cols, masks, *, blk, nblk, c, ahead):
    vt = _transpose_bf16(v_ref[:, cols])
    nears, far_bias = masks

    def scores(u):
        return _scores_t(k_ref[0:(u + 1) * blk, cols], q_ref[u * blk:(u + 1) * blk, cols])

    sts = [scores(u) for u in range(min(ahead, nblk))]
    yield
    for i in range(nblk):
        rows = slice(i * blk, (i + 1) * blk)
        ext = (i + 1) * blk
        if i + ahead < nblk:
            sts.append(scores(i + ahead))
        st = sts[i]
        tiles, biases, weights = [], [], []
        for n in range(i + 1):
            tiles.append(st[n * blk:(n + 1) * blk])
            if i - n < len(nears):
                bias, cnt = nears[i - n]
                biases.append(bias)
                weights.append(cnt)
            else:
                biases.append(far_bias)
                weights.append(None)
        p_all, inv_l = _softmax_probs_t(tiles, biases, weights, c)
        ot = jnp.dot(vt[:, 0:ext], p_all, preferred_element_type=F32) * inv_l
        o_ref[rows, cols] = (ot.T * _silu(g_ref[rows, cols].astype(F32))).astype(o_ref.dtype)
        yield


def _dilated_kernel(q_ref, k_ref, v_ref, g_ref, o_ref, *, blk, nblk, c, ahead, heads):
    cr = _tile_delta(blk)

    def near(delta):
        dist = cr + delta * blk
        cnt = (jnp.where(dist <= 128, 1.0, 0.0)
               + jnp.where(jnp.logical_and((dist & 3) == 0, dist <= 512), 1.0, 0.0)
               + jnp.where((dist & 15) == 0, 1.0, 0.0))
        cnt = jnp.where(dist >= 0, cnt, 0.0)
        return jnp.where(cnt > 0.0, 0.0, NEG), cnt

    n_near = -(-512 // blk) + 1
    nears = [near(d) for d in range(min(n_near, nblk))]
    far_bias = jnp.where((cr & 15) == 0, 0.0, NEG)
    masks = (nears, far_bias)
    _interleave([_dilated_head(q_ref, k_ref, v_ref, g_ref, o_ref, slice(hh * LANES, (hh + 1) * LANES), masks,
                               blk=blk, nblk=nblk, c=c, ahead=ahead) for hh in range(heads)])


def _dilated_attention(p3):
    b, s, _ = p3.shape
    blk = ATTN_BLK
    heads = HEADS_PER_STEP
    assert s % blk == 0 and A_HEADS % heads == 0
    return pl.pallas_call(
        functools.partial(_dilated_kernel, blk=blk, nblk=s // blk, c=HEAD_DIM ** -0.5 * LOG2E,
                          ahead=DILATED_LOOKAHEAD, heads=heads),
        grid=(b, A_HEADS // heads),
        in_specs=[_head_group_spec(s, first, heads) for first in (E_QA, E_KA, E_VA, E_GATE)],
        out_specs=_head_group_spec(s, 0, heads),
        out_shape=jax.ShapeDtypeStruct((b, s, A_HEADS * HEAD_DIM), BF16),
        compiler_params=_params(("arbitrary", "arbitrary")),
        name="dilated_attention",
    )(p3, p3, p3, p3)


def _moba_head(q_ref, k_ref, v_ref, g_ref, o_ref, cols, *, blk, nblk, c):
    s_len = nblk * blk
    q = q_ref[:, cols]

    def scores(u):
        return _scores_t(k_ref[0:(u + 1) * blk, cols], q[u * blk:(u + 1) * blk, :])

    sts = [scores(u) for u in range(min(MOBA_LOOKAHEAD, nblk))]
    vt = _transpose_bf16(v_ref[:, cols])

    km = jnp.concatenate(
        [jnp.sum(k_ref[n * blk:(n + 1) * blk, cols].astype(F32), axis=0, keepdims=True) * (1.0 / blk)
         for n in range(nblk)], axis=0)
    hi = km.astype(BF16).astype(F32)
    ksplit = jnp.concatenate([hi, km - hi], axis=0).astype(BF16)
    gt2 = _scores_t(ksplit, q)
    gate = gt2[:nblk] + gt2[nblk:]

    nid = lax.broadcasted_iota(jnp.int32, (nblk, s_len), 0)
    own = jnp.right_shift(lax.broadcasted_iota(jnp.int32, (nblk, s_len), 1), blk.bit_length() - 1)
    rank = jnp.zeros((nblk, s_len), jnp.int32)
    for mm in range(nblk):
        gm = gate[mm:mm + 1, :]
        beats = jnp.logical_or(gm > gate, jnp.logical_and(gm == gate, mm < nid))
        rank = rank + jnp.where(jnp.logical_and(beats, mm < own), 1, 0)
    sel_bias = jnp.where(jnp.logical_and(rank < MOBA_TOPK, nid < own), 0.0, NEG)
    causal_bias = jnp.where(_tile_delta(blk) >= 0, 0.0, NEG)
    yield

    for i in range(nblk):
        rows = slice(i * blk, (i + 1) * blk)
        ext = (i + 1) * blk
        if i + MOBA_LOOKAHEAD < nblk:
            sts.append(scores(i + MOBA_LOOKAHEAD))
        st = sts[i]
        tiles = [st[n * blk:(n + 1) * blk] for n in range(i + 1)]
        biases = [sel_bias[n:n + 1, rows] for n in range(i)] + [causal_bias]
        p_all, inv_l = _softmax_probs_t(tiles, biases, [None] * (i + 1), c)
        ot = jnp.dot(vt[:, 0:ext], p_all, preferred_element_type=F32) * inv_l
        o_ref[rows, cols] = (ot.T * _silu(g_ref[rows, cols].astype(F32))).astype(o_ref.dtype)
        yield


def _moba_kernel(q_ref, k_ref, v_ref, g_ref, o_ref, *, blk, nblk, c, heads):
    _interleave([_moba_head(q_ref, k_ref, v_ref, g_ref, o_ref, slice(hh * LANES, (hh + 1) * LANES),
                            blk=blk, nblk=nblk, c=c) for hh in range(heads)])


def _moba_attention(p3):
    b, s, _ = p3.shape
    blk = MOBA_BLOCK
    nblk = s // blk
    heads = HEADS_PER_STEP
    assert nblk >= MOBA_TOPK and blk & (blk - 1) == 0 and B_HEADS % heads == 0
    return pl.pallas_call(
        functools.partial(_moba_kernel, blk=blk, nblk=nblk, c=HEAD_DIM ** -0.5 * LOG2E, heads=heads),
        grid=(b, B_HEADS // heads),
        in_specs=[_head_group_spec(s, first, heads) for first in (E_QB, E_KB, E_VB, E_GATE + A_HEADS)],
        out_specs=_head_group_spec(s, 0, heads),
        out_shape=jax.ShapeDtypeStruct((b, s, B_HEADS * HEAD_DIM), BF16),
        compiler_params=_params(("arbitrary", "arbitrary")),
        name="moba_attention",
    )(p3, p3, p3, p3)


def _mem_kernel(*refs, tq, c):
    q_refs = refs[:MEM_HEADS]
    g_refs = refs[MEM_HEADS:2 * MEM_HEADS]
    memkv_ref, o_ref = refs[2 * MEM_HEADS:]
    kvw = MEM_HEADS * HEAD_DIM
    units = [(h, j) for h in range(MEM_HEADS) for j in range(o_ref.shape[0] // tq)]

    def scores(unit):
        h, j = unit
        return _scores_t(memkv_ref[:, h * HEAD_DIM:(h + 1) * HEAD_DIM], q_refs[h][j * tq:(j + 1) * tq, :])

    sts = [scores(u) for u in units[:SCORE_LOOKAHEAD]]
    mvts = [_transpose_bf16(memkv_ref[:, kvw + h * HEAD_DIM:kvw + (h + 1) * HEAD_DIM]) for h in range(MEM_HEADS)]
    for ui, (h, j) in enumerate(units):
        rows = slice(j * tq, (j + 1) * tq)
        if ui + SCORE_LOOKAHEAD < len(units):
            sts.append(scores(units[ui + SCORE_LOOKAHEAD]))
        p_all, inv_l = _softmax_probs_t([sts[ui]], [None], [None], c)
        ot = jnp.dot(mvts[h], p_all, preferred_element_type=F32) * inv_l
        gate = g_refs[h][rows, :].astype(F32)
        o_ref[rows, h * HEAD_DIM:(h + 1) * HEAD_DIM] = (ot.T * _silu(gate)).astype(o_ref.dtype)


def _memkv_kernel(h_ref, w_ref, o_ref):
    o_ref[...] = jnp.dot(h_ref[...], w_ref[...].astype(BF16), preferred_element_type=F32).astype(o_ref.dtype)


def _memkv_all(mem_n, w_stack, batch):
    t, d = mem_n.shape
    nl, _, n = w_stack.shape
    tn = min(n, MEMKV_TN)
    out = pl.pallas_call(
        _memkv_kernel,
        grid=(nl, n // tn),
        in_specs=[pl.BlockSpec((t, d), lambda l, j: (0, 0)),
                  pl.BlockSpec((None, d, tn), lambda l, j: (l, 0, j))],
        out_specs=pl.BlockSpec((None, t, tn), lambda l, j: (l, 0, j)),
        out_shape=jax.ShapeDtypeStruct((nl, t, n), BF16),
        compiler_params=_params(("arbitrary", "arbitrary")),
        name="memkv_proj",
    )(mem_n, w_stack)
    return out.reshape(nl, batch, MEM_LEN, n)


def _mem_attention(parts, memkv_stack, idx, q_block, gate_block):
    b, s, _ = parts[0][0].shape
    arrays, specs = [], []
    for first in (q_block, gate_block):
        for h in range(MEM_HEADS):
            arr, local = _locate(parts, first + h)
            arrays.append(arr)
            specs.append(pl.BlockSpec((None, s, LANES), functools.partial(lambda bi, c: (bi, 0, c), c=local)))
    width = MEM_HEADS * HEAD_DIM
    return pl.pallas_call(
        functools.partial(_mem_kernel, tq=min(s, MEM_TQ), c=HEAD_DIM ** -0.5 * LOG2E),
        grid=(b,),
        in_specs=specs + [pl.BlockSpec((None, None, MEM_LEN, 2 * width), lambda bi: (idx, bi, 0, 0))],
        out_specs=pl.BlockSpec((None, s, width), lambda bi: (bi, 0, 0)),
        out_shape=jax.ShapeDtypeStruct((b, s, width), BF16),
        compiler_params=_params(("arbitrary",)),
        name="memory_attention",
    )(*arrays, memkv_stack)


SWA_QBLK = C_WINDOW
SWA_GATE_BLOCKS = (C_HEADS * C_HEAD_DIM) // LANES
SWA_VT_WIDTH = 2 * LANES


def _swa_kernel(sinks_ref, q_ref, kv_ref, *rest, tq, c, scale):
    gate_refs = rest[:SWA_GATE_BLOCKS]
    o_ref, kvt_ref = rest[SWA_GATE_BLOCKS:]
    i = pl.program_id(1)
    heads_per_group = C_HEADS // C_KV_HEADS
    kw = C_KV_HEADS * C_HEAD_DIM

    @pl.when(i == 0)
    def _():
        for ch in range(kv_ref.shape[0] // LANES):
            t = kv_ref[ch * LANES:(ch + 1) * LANES, LANES:LANES + SWA_VT_WIDTH].astype(F32)
            kvt_ref[ch] = t.T.astype(BF16)

    krow = lax.broadcasted_iota(jnp.int32, (C_WINDOW, SWA_QBLK), 0)
    qcol = lax.broadcasted_iota(jnp.int32, (C_WINDOW, SWA_QBLK), 1)
    from_prev = krow > qcol
    zero_b = jnp.zeros((C_WINDOW, SWA_QBLK), BF16)
    shift = LANES.bit_length() - 1
    units = [(sub, g) for sub in range(tq // SWA_QBLK) for g in range(C_KV_HEADS)]

    def chunks(sub):
        r0 = i * tq + sub * SWA_QBLK
        prev = pl.multiple_of(jnp.maximum(r0 - C_WINDOW, 0), LANES)
        own = pl.multiple_of(r0, LANES)
        return r0, prev, own

    def scores(unit):
        sub, g = unit
        r0, prev, own = chunks(sub)
        rows = slice(sub * SWA_QBLK, (sub + 1) * SWA_QBLK)
        kcols = slice(g * C_HEAD_DIM, (g + 1) * C_HEAD_DIM)
        qg = jnp.concatenate([q_ref[rows, hd * C_HEAD_DIM:(hd + 1) * C_HEAD_DIM]
                              for hd in range(g * heads_per_group, (g + 1) * heads_per_group)], axis=0)
        s_prev = _scores_t(kv_ref[pl.ds(prev, C_WINDOW), kcols], qg)
        if sub == 0:
            s_prev = s_prev + jnp.where(r0 > 0, 0.0, NEG)
        return s_prev, _scores_t(kv_ref[pl.ds(own, C_WINDOW), kcols], qg)

    sts = [scores(u) for u in units[:SCORE_LOOKAHEAD]]
    for ui, (sub, g) in enumerate(units):
        if ui + SCORE_LOOKAHEAD < len(units):
            sts.append(scores(units[ui + SCORE_LOOKAHEAD]))
        s_prev, s_own = sts[ui]
        _, prev, own = chunks(sub)
        rows = slice(sub * SWA_QBLK, (sub + 1) * SWA_QBLK)
        v0 = (kw + g * C_HEAD_DIM) - LANES
        vt_prev = kvt_ref[jnp.right_shift(prev, shift)][v0:v0 + C_HEAD_DIM, :]
        vt_own = kvt_ref[jnp.right_shift(own, shift)][v0:v0 + C_HEAD_DIM, :]
        p_prev, p_own, inv_l = [], [], []
        for u in range(heads_per_group):
            cols = slice(u * SWA_QBLK, (u + 1) * SWA_QBLK)
            s_h = jnp.where(from_prev, s_prev[:, cols], s_own[:, cols])
            sink = sinks_ref[g * heads_per_group + u] * (1.0 / scale)
            m = jnp.maximum(jnp.max(s_h, axis=0, keepdims=True), sink)
            p = jnp.exp2((s_h - m) * c)
            inv_l.append(1.0 / (jnp.sum(p, axis=0, keepdims=True) + jnp.exp2((sink - m) * c)))
            pb = p.astype(BF16)
            p_prev.append(jnp.where(from_prev, pb, zero_b))
            p_own.append(jnp.where(from_prev, zero_b, pb))
        ot = (jnp.dot(vt_prev, jnp.concatenate(p_prev, axis=1), preferred_element_type=F32)
              + jnp.dot(vt_own, jnp.concatenate(p_own, axis=1), preferred_element_type=F32))
        ot = ot * jnp.concatenate(inv_l, axis=1)
        for pair in range(heads_per_group // 2):
            both = jnp.concatenate([ot[:, (2 * pair + u) * SWA_QBLK:(2 * pair + u + 1) * SWA_QBLK]
                                    for u in range(2)], axis=0)
            blk_i = (g * heads_per_group) // 2 + pair
            gate = gate_refs[blk_i][rows, :].astype(F32)
            o_ref[rows, blk_i * LANES:(blk_i + 1) * LANES] = (both.T * _silu(gate)).astype(o_ref.dtype)


def _swa_attention(parts, sinks):
    p_qkv = parts[0][0]
    b, s, _ = p_qkv.shape
    tq = min(s, SWA_TQ)
    qw = C_HEADS * C_HEAD_DIM
    kvw = 2 * C_KV_HEADS * C_HEAD_DIM
    assert (O_KV * LANES) % kvw == 0 and kvw == LANES + SWA_VT_WIDTH and p_qkv.shape[2] >= qw + kvw
    gate_arrays, gate_specs = [], []
    for u in range(SWA_GATE_BLOCKS):
        arr, local = _locate(parts, O_GATE + u)
        gate_arrays.append(arr)
        gate_specs.append(pl.BlockSpec((None, tq, LANES),
                                       functools.partial(lambda bi, i, sk, c: (bi, i, c), c=local)))
    scale = C_HEAD_DIM ** -0.5
    return pl.pallas_call(
        functools.partial(_swa_kernel, tq=tq, c=scale * LOG2E, scale=scale),
        grid_spec=pltpu.PrefetchScalarGridSpec(
            num_scalar_prefetch=1,
            grid=(b, s // tq),
            in_specs=[pl.BlockSpec((None, tq, qw), lambda bi, i, sk: (bi, i, 0)),
                      pl.BlockSpec((None, s, kvw), lambda bi, i, sk: (bi, 0, (O_KV * LANES) // kvw))]
                     + gate_specs,
            out_specs=pl.BlockSpec((None, tq, qw), lambda bi, i, sk: (bi, i, 0)),
            scratch_shapes=[pltpu.VMEM((s // LANES, SWA_VT_WIDTH, LANES), BF16)]),
        out_shape=jax.ShapeDtypeStruct((b, s, qw), BF16),
        compiler_params=_params(("arbitrary", "arbitrary")),
        name="swa_sink_attention",
    )(sinks, p_qkv, p_qkv, *gate_arrays)


def _outproj_kernel(*refs, widths, tn, emit_x):
    ny = len(widths)
    y_refs = refs[:ny]
    w_ref, x_ref, g_ref = refs[ny:ny + 3]
    ho_ref = refs[-1]
    hold_ref = refs[ny + 3] if emit_x else ho_ref
    tm, d = x_ref.shape
    ss = jnp.zeros((tm, 1), F32)
    for j in range(d // tn):
        cols = slice(j * tn, (j + 1) * tn)
        acc = x_ref[:, cols]
        off = 0
        for y_ref, wd in zip(y_refs, widths):
            acc = acc + jnp.dot(y_ref[...], w_ref[off:off + wd, cols], preferred_element_type=F32)
            off += wd
        hold_ref[:, cols] = acc
        ss = ss + jnp.sum(acc * acc, axis=1, keepdims=True)
    r = lax.rsqrt(ss * (1.0 / d) + EPS)
    for j in range(d // tn):
        cols = slice(j * tn, (j + 1) * tn)
        ho_ref[:, cols] = ((hold_ref[:, cols] * r) * g_ref[:, cols]).astype(ho_ref.dtype)


def _outproj(ys, w_bf16, x2d, g_next, final):
    t, d = x2d.shape
    widths = tuple(y.shape[1] for y in ys)
    kdim = w_bf16.shape[0]
    assert sum(widths) == kdim and w_bf16.shape[1] == d
    tm = min(t, OUT_TM)
    tn = min(d, OUT_TN)
    row_spec = pl.BlockSpec((tm, d), lambda i: (i, 0))
    in_specs = [pl.BlockSpec((tm, wd), lambda i: (i, 0)) for wd in widths]
    in_specs += [pl.BlockSpec((kdim, d), lambda i: (0, 0)), row_spec,
                 pl.BlockSpec((1, d), lambda i: (0, 0))]
    if final:
        out_specs = [row_spec]
        out_shape = [jax.ShapeDtypeStruct((t, d), F32)]
    else:
        out_specs = [row_spec, row_spec]
        out_shape = [jax.ShapeDtypeStruct((t, d), F32), jax.ShapeDtypeStruct((t, d), BF16)]
    res = pl.pallas_call(
        functools.partial(_outproj_kernel, widths=widths, tn=tn, emit_x=not final),
        grid=(t // tm,),
        in_specs=in_specs,
        out_specs=out_specs,
        out_shape=out_shape,
        compiler_params=_params(("arbitrary",)),
        name="outproj_final" if final else "outproj",
    )(*ys, w_bf16, x2d, g_next.reshape(1, d))
    return res[0] if final else (res[0], res[1])


def kernel(x, mem, positions, even_norm, even_w_in, even_w_mem_kv, even_w_out, odd_norm, odd_w_in,
           odd_w_mem_kv, odd_w_out, odd_sinks, mem_norm, final_norm):
    b, s, d = x.shape
    t = b * s
    depth = even_norm.shape[0] + odd_norm.shape[0]
    pos_col = positions.reshape(t, 1)
    kinds_e = _even_chunk_kinds()
    kinds_o = _odd_chunk_kinds()
    mem_n = _rmsnorm(mem.reshape(b * MEM_LEN, d), mem_norm)
    memkv_even = _memkv_all(mem_n, even_w_mem_kv, b)
    memkv_odd = _memkv_all(mem_n, odd_w_mem_kv, b)

    x2d = x.reshape(t, d)
    h, cos_e, sin_e, cos_o, sin_o = _rmsnorm_and_rope_tables(x2d, even_norm[0], pos_col)
    out = None
    for layer in range(depth):
        idx = layer // 2
        last = layer == depth - 1
        if last:
            g_next = final_norm
        elif layer % 2 == 0:
            g_next = odd_norm[idx]
        else:
            g_next = even_norm[idx + 1]
        if layer % 2 == 0:
            p3, w_out = _proj(h, even_w_in, idx, EVEN_TN, (kinds_e, cos_e, sin_e, HEAD_DIM), side_stack=even_w_out)
            p3 = p3.reshape(b, s, EVEN_IN)
            ys = [_dilated_attention(p3), _moba_attention(p3),
                  _mem_attention([(p3, 0)], memkv_even, idx, E_QM, E_GATE + 2 * A_HEADS)]
        else:
            wide = ODD_SPLIT_TILES * ODD_TN
            pa, w_out = _proj(h, odd_w_in, idx, wide // 2, (kinds_o[:wide // LANES], cos_o, sin_o, C_HEAD_DIM),
                              n_tiles=2, tm=ODD_WIDE_TM, side_stack=odd_w_out)
            pa = pa.reshape(b, s, wide)
            pb = _proj(h, odd_w_in, idx, ODD_TN, first_tile=ODD_SPLIT_TILES).reshape(b, s, ODD_IN - wide)
            parts = [(pa, 0), (pb, wide // LANES)]
            ys = [_swa_attention(parts, odd_sinks[idx]),
                  _mem_attention(parts, memkv_odd, idx, O_QM, O_GATE + SWA_GATE_BLOCKS)]
        ys = [y.reshape(t, y.shape[-1]) for y in ys]
        if last:
            out = _outproj(ys, w_out, x2d, g_next, final=True)
        else:
            x2d, h = _outproj(ys, w_out, x2d, g_next, final=False)
    return out.reshape(b, s, d)
```

```python
import functools
import math

import jax
import jax.numpy as jnp
from jax import lax
from jax.experimental import pallas as pl
from jax.experimental.pallas import tpu as pltpu

F32 = jnp.float32
BF16 = jnp.bfloat16

LANES = 128
HEAD_DIM = 128
MEM_LEN = 256
MEM_HEADS = 4
A_HEADS = 6
B_HEADS = 6
MOBA_BLOCK = 256
MOBA_TOPK = 3
C_HEADS = 24
C_KV_HEADS = 3
C_HEAD_DIM = 64
C_WINDOW = 128
ROPE_THETA = 10000.0
EPS = 1e-6
NEG = -1e30
LOG2E = 1.4426950408889634

EVEN_WIDTH = (A_HEADS + B_HEADS + MEM_HEADS) * HEAD_DIM
ODD_WIDTH = C_HEADS * C_HEAD_DIM + MEM_HEADS * HEAD_DIM
EVEN_IN = 3 * A_HEADS * HEAD_DIM + 3 * B_HEADS * HEAD_DIM + MEM_HEADS * HEAD_DIM + EVEN_WIDTH
ODD_IN = C_HEADS * C_HEAD_DIM + 2 * C_KV_HEADS * C_HEAD_DIM + MEM_HEADS * HEAD_DIM + ODD_WIDTH

E_QA, E_KA, E_VA = 0, A_HEADS, 2 * A_HEADS
E_QB, E_KB, E_VB = 3 * A_HEADS, 3 * A_HEADS + B_HEADS, 3 * A_HEADS + 2 * B_HEADS
E_QM = 3 * A_HEADS + 3 * B_HEADS
E_GATE = E_QM + MEM_HEADS
O_Q = 0
O_KV = (C_HEADS * C_HEAD_DIM) // LANES
O_QM = O_KV + (2 * C_KV_HEADS * C_HEAD_DIM) // LANES
O_GATE = O_QM + MEM_HEADS

PROJ_TM = 1024
EVEN_TN = 1024
MEMKV_TN = 512
ODD_TN = 896
ODD_SPLIT_TILES = 4
ODD_WIDE_TM = 512
OUT_TM, OUT_TN = 512, 512
NORM_TM = 512
ATTN_BLK = 256
MEM_TQ = 512
SWA_TQ = 1024

VMEM_LIMIT = 56 * 1024 * 1024


def _params(sem):
    return pltpu.CompilerParams(dimension_semantics=sem, vmem_limit_bytes=VMEM_LIMIT)


def _silu(g):
    half = 0.5 * g
    return half + half * jnp.tanh(half)


def _rmsnorm_rope_kernel(x_ref, g_ref, pos_ref, invf_ref, o_ref, c_full_ref, s_full_ref, c_half_ref, s_half_ref):
    _rmsnorm_kernel(x_ref, g_ref, o_ref)
    ang = pos_ref[...].astype(F32) * invf_ref[...]
    cos = jnp.cos(ang)
    sin = jnp.sin(ang)
    lane = lax.broadcasted_iota(jnp.int32, cos.shape, 1)
    half, quarter = LANES // 2, LANES // 4

    def spread(t, sign):
        swapped = pltpu.roll(t, half, 1)
        full = jnp.where(lane < half, sign * t, swapped)
        narrow = jnp.where(lane < quarter, sign * swapped,
                           jnp.where(lane < half, pltpu.roll(t, LANES - quarter, 1),
                                     jnp.where(lane < half + quarter, sign * t, pltpu.roll(t, quarter, 1))))
        return full, narrow

    c_full_ref[...], c_half_ref[...] = spread(cos, 1.0)
    s_full_ref[...], s_half_ref[...] = spread(sin, -1.0)


def _rmsnorm_and_rope_tables(x2d, g, pos_col):
    t, d = x2d.shape

    def inv_freq(head_dim):
        return jnp.exp(jnp.arange(head_dim // 2, dtype=F32) * (-2.0 * math.log(ROPE_THETA) / head_dim))

    invf = jnp.concatenate([inv_freq(HEAD_DIM), inv_freq(C_HEAD_DIM),
                            jnp.zeros((LANES - HEAD_DIM // 2 - C_HEAD_DIM // 2,), F32)])[None, :]
    tm = min(t, NORM_TM)
    tab = pl.BlockSpec((tm, LANES), lambda i: (i, 0))
    return pl.pallas_call(
        _rmsnorm_rope_kernel,
        grid=(t // tm,),
        in_specs=[pl.BlockSpec((tm, d), lambda i: (i, 0)),
                  pl.BlockSpec((1, d), lambda i: (0, 0)),
                  pl.BlockSpec((tm, 1), lambda i: (i, 0)),
                  pl.BlockSpec((1, LANES), lambda i: (0, 0))],
        out_specs=[pl.BlockSpec((tm, d), lambda i: (i, 0))] + [tab] * 4,
        out_shape=[jax.ShapeDtypeStruct((t, d), BF16)] + [jax.ShapeDtypeStruct((t, LANES), F32)] * 4,
        compiler_params=_params(("arbitrary",)),
        name="rmsnorm_rope_tables",
    )(x2d, g.reshape(1, d), pos_col, invf)


def _rmsnorm_kernel(x_ref, g_ref, o_ref):
    x = x_ref[...]
    ms = jnp.mean(x * x, axis=-1, keepdims=True)
    o_ref[...] = ((x * lax.rsqrt(ms + EPS)) * g_ref[...]).astype(o_ref.dtype)


def _rmsnorm(x2d, g):
    t, d = x2d.shape
    tm = min(t, NORM_TM)
    return pl.pallas_call(
        _rmsnorm_kernel,
        grid=(t // tm,),
        in_specs=[pl.BlockSpec((tm, d), lambda i: (i, 0)),
                  pl.BlockSpec((1, d), lambda i: (0, 0))],
        out_specs=pl.BlockSpec((tm, d), lambda i: (i, 0)),
        out_shape=jax.ShapeDtypeStruct((t, d), BF16),
        compiler_params=_params(("arbitrary",)),
        name="rmsnorm",
    )(x2d, g.reshape(1, d))


ROPE_NONE, ROPE_FULL, ROPE_FIRST_HALF = 0, 1, 2


def _rope_partner(x, head_dim):
    if head_dim == LANES:
        return pltpu.roll(x, LANES // 2, 1)
    lane = lax.broadcasted_iota(jnp.int32, x.shape, 1)
    half = head_dim // 2
    return jnp.where((lane % head_dim) < half, pltpu.roll(x, LANES - half, 1), pltpu.roll(x, half, 1))


def _proj_kernel(types_ref, h_ref, w_ref, cos_ref, sin_ref, side_ref, o_ref, side_o_ref, wbf_ref, *, tn, head_dim):
    j = pl.program_id(0)
    i = pl.program_id(1)
    side_o_ref[...] = side_ref[...].astype(side_o_ref.dtype)

    @pl.when(i == 0)
    def _():
        wbf_ref[...] = w_ref[...].astype(BF16)

    nchunk = tn // LANES
    any_rope = types_ref[pl.num_programs(0) * nchunk + j]

    @pl.when(any_rope == 0)
    def _():
        o_ref[...] = jnp.dot(h_ref[...], wbf_ref[...], preferred_element_type=F32).astype(o_ref.dtype)

    @pl.when(any_rope != 0)
    def _():
        acc = jnp.dot(h_ref[...], wbf_ref[...], preferred_element_type=F32)
        cos = cos_ref[...]
        sin = sin_ref[...]
        upper = lax.broadcasted_iota(jnp.int32, cos.shape, 1) >= LANES // 2
        for c in range(nchunk):
            kind = types_ref[j * nchunk + c]
            x = acc[:, c * LANES:(c + 1) * LANES]
            roped = x * cos + _rope_partner(x, head_dim) * sin
            plain = jnp.logical_or(kind == ROPE_NONE, jnp.logical_and(kind == ROPE_FIRST_HALF, upper))
            o_ref[:, c * LANES:(c + 1) * LANES] = jnp.where(plain, x, roped).astype(o_ref.dtype)


def _proj_plain_kernel(h_ref, w_ref, o_ref, wbf_ref):
    @pl.when(pl.program_id(1) == 0)
    def _():
        wbf_ref[...] = w_ref[...].astype(BF16)

    o_ref[...] = jnp.dot(h_ref[...], wbf_ref[...], preferred_element_type=F32).astype(o_ref.dtype)


def _proj(h, w_stack, idx, tn, rope=None, first_tile=0, n_tiles=None, tm=PROJ_TM, side_stack=None):
    t, d = h.shape
    if n_tiles is None:
        n_tiles = w_stack.shape[2] // tn - first_tile
    assert tn % LANES == 0 and (first_tile + n_tiles) * tn <= w_stack.shape[2]
    n = n_tiles * tn
    tm = min(t, tm)
    grid = (n_tiles, t // tm)
    out_shape = jax.ShapeDtypeStruct((t, n), BF16)
    scratch = [pltpu.VMEM((d, tn), BF16)]
    if rope is None:
        return pl.pallas_call(
            _proj_plain_kernel,
            grid=grid,
            in_specs=[pl.BlockSpec((tm, d), lambda j, i: (i, 0)),
                      pl.BlockSpec((None, d, tn), lambda j, i: (idx, 0, first_tile + j))],
            out_specs=pl.BlockSpec((tm, tn), lambda j, i: (i, j)),
            out_shape=out_shape,
            scratch_shapes=scratch,
            compiler_params=_params(("arbitrary", "arbitrary")),
            name="proj_plain",
        )(h, w_stack)
    kinds, cos, sin, head_dim = rope
    per_tile = kinds.reshape(n_tiles, tn // LANES)
    kinds = jnp.concatenate([kinds, (per_tile != ROPE_NONE).any(axis=1).astype(jnp.int32)])
    side_rows, side_cols = side_stack.shape[1:]
    n_i = t // tm
    n_side = 1 << ((n_tiles * n_i).bit_length() - 1)
    assert side_rows % n_side == 0 and (side_rows // n_side) % 16 == 0

    def side_block(j, i):
        return jnp.minimum(j * n_i + i, n_side - 1)

    return pl.pallas_call(
        functools.partial(_proj_kernel, tn=tn, head_dim=head_dim),
        grid_spec=pltpu.PrefetchScalarGridSpec(
            num_scalar_prefetch=1,
            grid=grid,
            in_specs=[pl.BlockSpec((tm, d), lambda j, i, k: (i, 0)),
                      pl.BlockSpec((None, d, tn), lambda j, i, k: (idx, 0, first_tile + j)),
                      pl.BlockSpec((tm, LANES), lambda j, i, k: (i, 0)),
                      pl.BlockSpec((tm, LANES), lambda j, i, k: (i, 0)),
                      pl.BlockSpec((None, side_rows // n_side, side_cols),
                                   lambda j, i, k: (idx, side_block(j, i), 0))],
            out_specs=[pl.BlockSpec((tm, tn), lambda j, i, k: (i, j)),
                       pl.BlockSpec((side_rows // n_side, side_cols), lambda j, i, k: (side_block(j, i), 0))],
            scratch_shapes=scratch),
        out_shape=[out_shape, jax.ShapeDtypeStruct((side_rows, side_cols), BF16)],
        compiler_params=_params(("arbitrary", "arbitrary")),
        name=f"proj_rope{head_dim}",
    )(kinds, h, w_stack, cos, sin, side_stack)


def _even_chunk_kinds():
    kinds = [ROPE_NONE] * (EVEN_IN // LANES)
    for start in (E_QA, E_KA, E_QB, E_KB):
        for c in range(start, start + A_HEADS):
            kinds[c] = ROPE_FULL
    return jnp.asarray(kinds, jnp.int32)


def _odd_chunk_kinds():
    kinds = [ROPE_NONE] * (ODD_IN // LANES)
    k_end = C_HEADS * C_HEAD_DIM + C_KV_HEADS * C_HEAD_DIM
    for c in range(len(kinds)):
        if (c + 1) * LANES <= k_end:
            kinds[c] = ROPE_FULL
        elif c * LANES < k_end:
            assert k_end - c * LANES == LANES // 2
            kinds[c] = ROPE_FIRST_HALF
    return jnp.asarray(kinds, jnp.int32)


def _locate(parts, block):
    arr, first = [pt for pt in parts if pt[1] <= block][-1]
    assert block - first < arr.shape[2] // LANES
    return arr, block - first


def _scores_t(k, q):
    return lax.dot_general(k, q, (((1,), (1,)), ((), ())), preferred_element_type=F32)


def _transpose_bf16(x):
    return x.astype(F32).T.astype(BF16)


def _softmax_probs_t(tiles, biases, c):
    m = None
    for s, b in zip(tiles, biases):
        if b is None:
            mt = jnp.max(s, axis=0, keepdims=True)
        elif b.shape[0] == 1:
            mt = jnp.max(s, axis=0, keepdims=True) + b
        else:
            mt = jnp.max(s + b, axis=0, keepdims=True)
        m = mt if m is None else jnp.maximum(m, mt)
    shifts = {}
    l = None
    ps = []
    for s, b in zip(tiles, biases):
        if b is None:
            shift = -m
        else:
            if id(b) not in shifts:
                shifts[id(b)] = b - m
            shift = shifts[id(b)]
        p = jnp.exp2((s + shift) * c)
        lt = jnp.sum(p, axis=0, keepdims=True)
        l = lt if l is None else l + lt
        ps.append(p.astype(BF16))
    p_all = ps[0] if len(ps) == 1 else jnp.concatenate(ps, axis=0)
    return p_all, 1.0 / l


SCORE_LOOKAHEAD = 2
DILATED_LOOKAHEAD = 3
MOBA_LOOKAHEAD = 1


def _interleave(generators):
    live = list(generators)
    while live:
        for gen in list(live):
            try:
                next(gen)
            except StopIteration:
                live.remove(gen)


HEADS_PER_STEP = 2


def _head_group_spec(s, first, heads):
    assert first % heads == 0
    return pl.BlockSpec((None, s, heads * LANES), lambda bi, h: (bi, 0, first // heads + h))


def _tile_delta(blk):
    row = lax.broadcasted_iota(jnp.int32, (blk, blk), 0)
    col = lax.broadcasted_iota(jnp.int32, (blk, blk), 1)
    return col - row


def _dilated_head(q_ref, k_ref, v_ref, g_ref, o_ref, cols, masks, *, blk, nblk, c, ahead):
    vt = _transpose_bf16(v_ref[:, cols])
    nears, far_bias = masks

    def scores(u):
        return _scores_t(k_ref[0:(u + 1) * blk, cols], q_ref[u * blk:(u + 1) * blk, cols])

    sts = [scores(u) for u in range(min(ahead, nblk))]
    yield
    for i in range(nblk):
        rows = slice(i * blk, (i + 1) * blk)
        ext = (i + 1) * blk
        if i + ahead < nblk:
            sts.append(scores(i + ahead))
        st = sts[i]
        tiles = [st[n * blk:(n + 1) * blk] for n in range(i + 1)]
        biases = [nears[i - n] if i - n < len(nears) else far_bias for n in range(i + 1)]
        p_all, inv_l = _softmax_probs_t(tiles, biases, c)
        ot = jnp.dot(vt[:, 0:ext], p_all, preferred_element_type=F32) * inv_l
        o_ref[rows, cols] = (ot.T * _silu(g_ref[rows, cols].astype(F32))).astype(o_ref.dtype)
        yield


def _dilated_kernel(q_ref, k_ref, v_ref, g_ref, o_ref, *, blk, nblk, c, ahead, heads):
    cr = _tile_delta(blk)

    def near(delta):
        dist = cr + delta * blk
        cnt = (jnp.where(dist <= 128, 1, 0)
               + jnp.where(jnp.logical_and((dist & 3) == 0, dist <= 512), 1, 0)
               + jnp.where((dist & 15) == 0, 1, 0))
        cnt = jnp.where(dist >= 0, cnt, 0)
        return jnp.where(cnt == 1, 0.0, jnp.where(cnt == 2, 1.0 / c, jnp.where(cnt == 3, math.log2(3.0) / c, NEG)))

    n_near = -(-512 // blk) + 1
    nears = [near(d) for d in range(min(n_near, nblk))]
    far_bias = jnp.where((cr & 15) == 0, 0.0, NEG)
    masks = (nears, far_bias)
    _interleave([_dilated_head(q_ref, k_ref, v_ref, g_ref, o_ref, slice(hh * LANES, (hh + 1) * LANES), masks,
                               blk=blk, nblk=nblk, c=c, ahead=ahead) for hh in range(heads)])


def _dilated_attention(p3):
    b, s, _ = p3.shape
    blk = ATTN_BLK
    heads = HEADS_PER_STEP
    assert s % blk == 0 and A_HEADS % heads == 0
    return pl.pallas_call(
        functools.partial(_dilated_kernel, blk=blk, nblk=s // blk, c=HEAD_DIM ** -0.5 * LOG2E,
                          ahead=DILATED_LOOKAHEAD, heads=heads),
        grid=(b, A_HEADS // heads),
        in_specs=[_head_group_spec(s, first, heads) for first in (E_QA, E_KA, E_VA, E_GATE)],
        out_specs=_head_group_spec(s, 0, heads),
        out_shape=jax.ShapeDtypeStruct((b, s, A_HEADS * HEAD_DIM), BF16),
        compiler_params=_params(("arbitrary", "arbitrary")),
        name="dilated_attention",
    )(p3, p3, p3, p3)


def _moba_head(q_ref, k_ref, v_ref, g_ref, o_ref, cols, *, blk, nblk, c):
    s_len = nblk * blk
    q = q_ref[:, cols]

    def scores(u):
        return _scores_t(k_ref[0:(u + 1) * blk, cols], q[u * blk:(u + 1) * blk, :])

    sts = [scores(u) for u in range(min(MOBA_LOOKAHEAD, nblk))]
    vt = _transpose_bf16(v_ref[:, cols])

    km = jnp.concatenate(
        [jnp.sum(k_ref[n * blk:(n + 1) * blk, cols].astype(F32), axis=0, keepdims=True) * (1.0 / blk)
         for n in range(nblk)], axis=0)
    hi = km.astype(BF16).astype(F32)
    ksplit = jnp.concatenate([hi, km - hi], axis=0).astype(BF16)
    gt2 = _scores_t(ksplit, q)
    gate = gt2[:nblk] + gt2[nblk:]

    nid = lax.broadcasted_iota(jnp.int32, (nblk, s_len), 0)
    own = jnp.right_shift(lax.broadcasted_iota(jnp.int32, (nblk, s_len), 1), blk.bit_length() - 1)
    rank = jnp.zeros((nblk, s_len), jnp.int32)
    for mm in range(nblk):
        gm = gate[mm:mm + 1, :]
        beats = jnp.logical_or(gm > gate, jnp.logical_and(gm == gate, mm < nid))
        rank = rank + jnp.where(jnp.logical_and(beats, mm < own), 1, 0)
    sel_bias = jnp.where(jnp.logical_and(rank < MOBA_TOPK, nid < own), 0.0, NEG)
    causal_bias = jnp.where(_tile_delta(blk) >= 0, 0.0, NEG)
    yield

    for i in range(nblk):
        rows = slice(i * blk, (i + 1) * blk)
        ext = (i + 1) * blk
        if i + MOBA_LOOKAHEAD < nblk:
            sts.append(scores(i + MOBA_LOOKAHEAD))
        st = sts[i]
        tiles = [st[n * blk:(n + 1) * blk] for n in range(i + 1)]
        biases = [sel_bias[n:n + 1, rows] for n in range(i)] + [causal_bias]
        p_all, inv_l = _softmax_probs_t(tiles, biases, c)
        ot = jnp.dot(vt[:, 0:ext], p_all, preferred_element_type=F32) * inv_l
        o_ref[rows, cols] = (ot.T * _silu(g_ref[rows, cols].astype(F32))).astype(o_ref.dtype)
        yield


def _moba_kernel(q_ref, k_ref, v_ref, g_ref, o_ref, *, blk, nblk, c, heads):
    _interleave([_moba_head(q_ref, k_ref, v_ref, g_ref, o_ref, slice(hh * LANES, (hh + 1) * LANES),
                            blk=blk, nblk=nblk, c=c) for hh in range(heads)])


def _moba_attention(p3):
    b, s, _ = p3.shape
    blk = MOBA_BLOCK
    nblk = s // blk
    heads = HEADS_PER_STEP
    assert nblk >= MOBA_TOPK and blk & (blk - 1) == 0 and B_HEADS % heads == 0
    return pl.pallas_call(
        functools.partial(_moba_kernel, blk=blk, nblk=nblk, c=HEAD_DIM ** -0.5 * LOG2E, heads=heads),
        grid=(b, B_HEADS // heads),
        in_specs=[_head_group_spec(s, first, heads) for first in (E_QB, E_KB, E_VB, E_GATE + A_HEADS)],
        out_specs=_head_group_spec(s, 0, heads),
        out_shape=jax.ShapeDtypeStruct((b, s, B_HEADS * HEAD_DIM), BF16),
        compiler_params=_params(("arbitrary", "arbitrary")),
        name="moba_attention",
    )(p3, p3, p3, p3)


def _mem_kernel(*refs, tq, c):
    q_refs = refs[:MEM_HEADS]
    g_refs = refs[MEM_HEADS:2 * MEM_HEADS]
    memkv_ref, o_ref = refs[2 * MEM_HEADS:]
    kvw = MEM_HEADS * HEAD_DIM
    units = [(h, j) for h in range(MEM_HEADS) for j in range(o_ref.shape[0] // tq)]

    def scores(unit):
        h, j = unit
        return _scores_t(memkv_ref[:, h * HEAD_DIM:(h + 1) * HEAD_DIM], q_refs[h][j * tq:(j + 1) * tq, :])

    sts = [scores(u) for u in units[:SCORE_LOOKAHEAD]]
    mvts = [_transpose_bf16(memkv_ref[:, kvw + h * HEAD_DIM:kvw + (h + 1) * HEAD_DIM]) for h in range(MEM_HEADS)]
    for ui, (h, j) in enumerate(units):
        rows = slice(j * tq, (j + 1) * tq)
        if ui + SCORE_LOOKAHEAD < len(units):
            sts.append(scores(units[ui + SCORE_LOOKAHEAD]))
        p_all, inv_l = _softmax_probs_t([sts[ui]], [None], c)
        ot = jnp.dot(mvts[h], p_all, preferred_element_type=F32) * inv_l
        gate = g_refs[h][rows, :].astype(F32)
        o_ref[rows, h * HEAD_DIM:(h + 1) * HEAD_DIM] = (ot.T * _silu(gate)).astype(o_ref.dtype)


def _memkv_kernel(h_ref, w_ref, o_ref):
    o_ref[...] = jnp.dot(h_ref[...], w_ref[...].astype(BF16), preferred_element_type=F32).astype(o_ref.dtype)


def _memkv_all(mem_n, w_stack, batch):
    t, d = mem_n.shape
    nl, _, n = w_stack.shape
    tn = min(n, MEMKV_TN)
    out = pl.pallas_call(
        _memkv_kernel,
        grid=(nl, n // tn),
        in_specs=[pl.BlockSpec((t, d), lambda l, j: (0, 0)),
                  pl.BlockSpec((None, d, tn), lambda l, j: (l, 0, j))],
        out_specs=pl.BlockSpec((None, t, tn), lambda l, j: (l, 0, j)),
        out_shape=jax.ShapeDtypeStruct((nl, t, n), BF16),
        compiler_params=_params(("arbitrary", "arbitrary")),
        name="memkv_proj",
    )(mem_n, w_stack)
    return out.reshape(nl, batch, MEM_LEN, n)


def _mem_attention(parts, memkv_stack, idx, q_block, gate_block):
    b, s, _ = parts[0][0].shape
    arrays, specs = [], []
    for first in (q_block, gate_block):
        for h in range(MEM_HEADS):
            arr, local = _locate(parts, first + h)
            arrays.append(arr)
            specs.append(pl.BlockSpec((None, s, LANES), functools.partial(lambda bi, c: (bi, 0, c), c=local)))
    width = MEM_HEADS * HEAD_DIM
    return pl.pallas_call(
        functools.partial(_mem_kernel, tq=min(s, MEM_TQ), c=HEAD_DIM ** -0.5 * LOG2E),
        grid=(b,),
        in_specs=specs + [pl.BlockSpec((None, None, MEM_LEN, 2 * width), lambda bi: (idx, bi, 0, 0))],
        out_specs=pl.BlockSpec((None, s, width), lambda bi: (bi, 0, 0)),
        out_shape=jax.ShapeDtypeStruct((b, s, width), BF16),
        compiler_params=_params(("arbitrary",)),
        name="memory_attention",
    )(*arrays, memkv_stack)


SWA_QBLK = C_WINDOW
SWA_GATE_BLOCKS = (C_HEADS * C_HEAD_DIM) // LANES
SWA_VT_WIDTH = 2 * LANES


def _swa_kernel(sinks_ref, q_ref, kv_ref, *rest, tq, c, scale):
    gate_refs = rest[:SWA_GATE_BLOCKS]
    o_ref, kvt_ref = rest[SWA_GATE_BLOCKS:]
    i = pl.program_id(1)
    heads_per_group = C_HEADS // C_KV_HEADS
    kw = C_KV_HEADS * C_HEAD_DIM

    @pl.when(i == 0)
    def _():
        for ch in range(kv_ref.shape[0] // LANES):
            t = kv_ref[ch * LANES:(ch + 1) * LANES, LANES:LANES + SWA_VT_WIDTH].astype(F32)
            kvt_ref[ch] = t.T.astype(BF16)

    krow = lax.broadcasted_iota(jnp.int32, (C_WINDOW, SWA_QBLK), 0)
    qcol = lax.broadcasted_iota(jnp.int32, (C_WINDOW, SWA_QBLK), 1)
    from_prev = krow > qcol
    zero_b = jnp.zeros((C_WINDOW, SWA_QBLK), BF16)
    shift = LANES.bit_length() - 1
    units = [(sub, g) for sub in range(tq // SWA_QBLK) for g in range(C_KV_HEADS)]

    def chunks(sub):
        r0 = i * tq + sub * SWA_QBLK
        prev = pl.multiple_of(jnp.maximum(r0 - C_WINDOW, 0), LANES)
        own = pl.multiple_of(r0, LANES)
        return r0, prev, own

    def scores(unit):
        sub, g = unit
        r0, prev, own = chunks(sub)
        rows = slice(sub * SWA_QBLK, (sub + 1) * SWA_QBLK)
        kcols = slice(g * C_HEAD_DIM, (g + 1) * C_HEAD_DIM)
        qg = jnp.concatenate([q_ref[rows, hd * C_HEAD_DIM:(hd + 1) * C_HEAD_DIM]
                              for hd in range(g * heads_per_group, (g + 1) * heads_per_group)], axis=0)
        s_prev = _scores_t(kv_ref[pl.ds(prev, C_WINDOW), kcols], qg)
        if sub == 0:
            s_prev = s_prev + jnp.where(r0 > 0, 0.0, NEG)
        return s_prev, _scores_t(kv_ref[pl.ds(own, C_WINDOW), kcols], qg)

    sts = [scores(u) for u in units[:SCORE_LOOKAHEAD]]
    for ui, (sub, g) in enumerate(units):
        if ui + SCORE_LOOKAHEAD < len(units):
            sts.append(scores(units[ui + SCORE_LOOKAHEAD]))
        s_prev, s_own = sts[ui]
        _, prev, own = chunks(sub)
        rows = slice(sub * SWA_QBLK, (sub + 1) * SWA_QBLK)
        v0 = (kw + g * C_HEAD_DIM) - LANES
        vt_prev = kvt_ref[jnp.right_shift(prev, shift)][v0:v0 + C_HEAD_DIM, :]
        vt_own = kvt_ref[jnp.right_shift(own, shift)][v0:v0 + C_HEAD_DIM, :]
        p_prev, p_own, inv_l = [], [], []
        for u in range(heads_per_group):
            cols = slice(u * SWA_QBLK, (u + 1) * SWA_QBLK)
            s_h = jnp.where(from_prev, s_prev[:, cols], s_own[:, cols])
            sink = sinks_ref[g * heads_per_group + u] * (1.0 / scale)
            m = jnp.maximum(jnp.max(s_h, axis=0, keepdims=True), sink)
            p = jnp.exp2((s_h - m) * c)
            inv_l.append(1.0 / (jnp.sum(p, axis=0, keepdims=True) + jnp.exp2((sink - m) * c)))
            pb = p.astype(BF16)
            p_prev.append(jnp.where(from_prev, pb, zero_b))
            p_own.append(jnp.where(from_prev, zero_b, pb))
        ot = (jnp.dot(vt_prev, jnp.concatenate(p_prev, axis=1), preferred_element_type=F32)
              + jnp.dot(vt_own, jnp.concatenate(p_own, axis=1), preferred_element_type=F32))
        ot = ot * jnp.concatenate(inv_l, axis=1)
        for pair in range(heads_per_group // 2):
            both = jnp.concatenate([ot[:, (2 * pair + u) * SWA_QBLK:(2 * pair + u + 1) * SWA_QBLK]
                                    for u in range(2)], axis=0)
            blk_i = (g * heads_per_group) // 2 + pair
            gate = gate_refs[blk_i][rows, :].astype(F32)
            o_ref[rows, blk_i * LANES:(blk_i + 1) * LANES] = (both.T * _silu(gate)).astype(o_ref.dtype)


def _swa_attention(parts, sinks):
    p_qkv = parts[0][0]
    b, s, _ = p_qkv.shape
    tq = min(s, SWA_TQ)
    qw = C_HEADS * C_HEAD_DIM
    kvw = 2 * C_KV_HEADS * C_HEAD_DIM
    assert (O_KV * LANES) % kvw == 0 and kvw == LANES + SWA_VT_WIDTH and p_qkv.shape[2] >= qw + kvw
    gate_arrays, gate_specs = [], []
    for u in range(SWA_GATE_BLOCKS):
        arr, local = _locate(parts, O_GATE + u)
        gate_arrays.append(arr)
        gate_specs.append(pl.BlockSpec((None, tq, LANES),
                                       functools.partial(lambda bi, i, sk, c: (bi, i, c), c=local)))
    scale = C_HEAD_DIM ** -0.5
    return pl.pallas_call(
        functools.partial(_swa_kernel, tq=tq, c=scale * LOG2E, scale=scale),
        grid_spec=pltpu.PrefetchScalarGridSpec(
            num_scalar_prefetch=1,
            grid=(b, s // tq),
            in_specs=[pl.BlockSpec((None, tq, qw), lambda bi, i, sk: (bi, i, 0)),
                      pl.BlockSpec((None, s, kvw), lambda bi, i, sk: (bi, 0, (O_KV * LANES) // kvw))]
                     + gate_specs,
            out_specs=pl.BlockSpec((None, tq, qw), lambda bi, i, sk: (bi, i, 0)),
            scratch_shapes=[pltpu.VMEM((s // LANES, SWA_VT_WIDTH, LANES), BF16)]),
        out_shape=jax.ShapeDtypeStruct((b, s, qw), BF16),
        compiler_params=_params(("arbitrary", "arbitrary")),
        name="swa_sink_attention",
    )(sinks, p_qkv, p_qkv, *gate_arrays)


def _outproj_kernel(*refs, widths, tn, emit_x):
    ny = len(widths)
    y_refs = refs[:ny]
    w_ref, x_ref, g_ref = refs[ny:ny + 3]
    ho_ref = refs[-1]
    hold_ref = refs[ny + 3] if emit_x else ho_ref
    tm, d = x_ref.shape
    ss = jnp.zeros((tm, 1), F32)
    for j in range(d // tn):
        cols = slice(j * tn, (j + 1) * tn)
        acc = x_ref[:, cols]
        off = 0
        for y_ref, wd in zip(y_refs, widths):
            acc = acc + jnp.dot(y_ref[...], w_ref[off:off + wd, cols], preferred_element_type=F32)
            off += wd
        hold_ref[:, cols] = acc
        ss = ss + jnp.sum(acc * acc, axis=1, keepdims=True)
    r = lax.rsqrt(ss * (1.0 / d) + EPS)
    for j in range(d // tn):
        cols = slice(j * tn, (j + 1) * tn)
        ho_ref[:, cols] = ((hold_ref[:, cols] * r) * g_ref[:, cols]).astype(ho_ref.dtype)


def _outproj(ys, w_bf16, x2d, g_next, final):
    t, d = x2d.shape
    widths = tuple(y.shape[1] for y in ys)
    kdim = w_bf16.shape[0]
    assert sum(widths) == kdim and w_bf16.shape[1] == d
    tm = min(t, OUT_TM)
    tn = min(d, OUT_TN)
    row_spec = pl.BlockSpec((tm, d), lambda i: (i, 0))
    in_specs = [pl.BlockSpec((tm, wd), lambda i: (i, 0)) for wd in widths]
    in_specs += [pl.BlockSpec((kdim, d), lambda i: (0, 0)), row_spec,
                 pl.BlockSpec((1, d), lambda i: (0, 0))]
    if final:
        out_specs = [row_spec]
        out_shape = [jax.ShapeDtypeStruct((t, d), F32)]
    else:
        out_specs = [row_spec, row_spec]
        out_shape = [jax.ShapeDtypeStruct((t, d), F32), jax.ShapeDtypeStruct((t, d), BF16)]
    res = pl.pallas_call(
        functools.partial(_outproj_kernel, widths=widths, tn=tn, emit_x=not final),
        grid=(t // tm,),
        in_specs=in_specs,
        out_specs=out_specs,
        out_shape=out_shape,
        compiler_params=_params(("arbitrary",)),
        name="outproj_final" if final else "outproj",
    )(*ys, w_bf16, x2d, g_next.reshape(1, d))
    return res[0] if final else (res[0], res[1])


def kernel(x, mem, positions, even_norm, even_w_in, even_w_mem_kv, even_w_out, odd_norm, odd_w_in,
           odd_w_mem_kv, odd_w_out, odd_sinks, mem_norm, final_norm):
    b, s, d = x.shape
    t = b * s
    depth = even_norm.shape[0] + odd_norm.shape[0]
    pos_col = positions.reshape(t, 1)
    kinds_e = _even_chunk_kinds()
    kinds_o = _odd_chunk_kinds()
    mem_n = _rmsnorm(mem.reshape(b * MEM_LEN, d), mem_norm)
    memkv_even = _memkv_all(mem_n, even_w_mem_kv, b)
    memkv_odd = _memkv_all(mem_n, odd_w_mem_kv, b)

    x2d = x.reshape(t, d)
    h, cos_e, sin_e, cos_o, sin_o = _rmsnorm_and_rope_tables(x2d, even_norm[0], pos_col)
    out = None
    for layer in range(depth):
        idx = layer // 2
        last = layer == depth - 1
        if last:
            g_next = final_norm
        elif layer % 2 == 0:
            g_next = odd_norm[idx]
        else:
            g_next = even_norm[idx + 1]
        if layer % 2 == 0:
            p3, w_out = _proj(h, even_w_in, idx, EVEN_TN, (kinds_e, cos_e, sin_e, HEAD_DIM), side_stack=even_w_out)
            p3 = p3.reshape(b, s, EVEN_IN)
            ys = [_dilated_attention(p3), _moba_attention(p3),
                  _mem_attention([(p3, 0)], memkv_even, idx, E_QM, E_GATE + 2 * A_HEADS)]
        else:
            wide = ODD_SPLIT_TILES * ODD_TN
            pa, w_out = _proj(h, odd_w_in, idx, wide // 2, (kinds_o[:wide // LANES], cos_o, sin_o, C_HEAD_DIM),
                              n_tiles=2, tm=ODD_WIDE_TM, side_stack=odd_w_out)
            pa = pa.reshape(b, s, wide)
            pb = _proj(h, odd_w_in, idx, ODD_TN, first_tile=ODD_SPLIT_TILES).reshape(b, s, ODD_IN - wide)
            parts = [(pa, 0), (pb, wide // LANES)]
            ys = [_swa_attention(parts, odd_sinks[idx]),
                  _mem_attention(parts, memkv_odd, idx, O_QM, O_GATE + SWA_GATE_BLOCKS)]
        ys = [y.reshape(t, y.shape[-1]) for y in ys]
        if last:
            out = _outproj(ys, w_out, x2d, g_next, final=True)
        else:
            x2d, h = _outproj(ys, w_out, x2d, g_next, final=False)
    return out.reshape(b, s, d)
```

```python
import functools
import math

import jax
import jax.numpy as jnp
from jax import lax
from jax.experimental import pallas as pl
from jax.experimental.pallas import tpu as pltpu

F32 = jnp.float32
BF16 = jnp.bfloat16

LANES = 128
HEAD_DIM = 128
MEM_LEN = 256
MEM_HEADS = 4
A_HEADS = 6
B_HEADS = 6
MOBA_BLOCK = 256
MOBA_TOPK = 3
C_HEADS = 24
C_KV_HEADS = 3
C_HEAD_DIM = 64
C_WINDOW = 128
ROPE_THETA = 10000.0
EPS = 1e-6
NEG = -1e30
LOG2E = 1.4426950408889634

EVEN_WIDTH = (A_HEADS + B_HEADS + MEM_HEADS) * HEAD_DIM
ODD_WIDTH = C_HEADS * C_HEAD_DIM + MEM_HEADS * HEAD_DIM
EVEN_IN = 3 * A_HEADS * HEAD_DIM + 3 * B_HEADS * HEAD_DIM + MEM_HEADS * HEAD_DIM + EVEN_WIDTH
ODD_IN = C_HEADS * C_HEAD_DIM + 2 * C_KV_HEADS * C_HEAD_DIM + MEM_HEADS * HEAD_DIM + ODD_WIDTH

E_QA, E_KA, E_VA = 0, A_HEADS, 2 * A_HEADS
E_QB, E_KB, E_VB = 3 * A_HEADS, 3 * A_HEADS + B_HEADS, 3 * A_HEADS + 2 * B_HEADS
E_QM = 3 * A_HEADS + 3 * B_HEADS
E_GATE = E_QM + MEM_HEADS
O_Q = 0
O_KV = (C_HEADS * C_HEAD_DIM) // LANES
O_QM = O_KV + (2 * C_KV_HEADS * C_HEAD_DIM) // LANES
O_GATE = O_QM + MEM_HEADS

PROJ_TM = 1024
EVEN_TN = 1024
MEMKV_TN = 512
ODD_TN = 896
ODD_SPLIT_TILES = 4
ODD_WIDE_TM = 512
OUT_TM, OUT_TN = 512, 512
NORM_TM = 512
ATTN_BLK = 256
MEM_TQ = 512
SWA_TQ = 1024

VMEM_LIMIT = 56 * 1024 * 1024


def _params(sem):
    return pltpu.CompilerParams(dimension_semantics=sem, vmem_limit_bytes=VMEM_LIMIT)


def _silu(g):
    half = 0.5 * g
    return half + half * jnp.tanh(half)


def _rmsnorm_rope_kernel(x_ref, g_ref, pos_ref, invf_ref, o_ref, c_full_ref, s_full_ref, c_half_ref, s_half_ref):
    _rmsnorm_kernel(x_ref, g_ref, o_ref)
    ang = pos_ref[...].astype(F32) * invf_ref[...]
    cos = jnp.cos(ang)
    sin = jnp.sin(ang)
    lane = lax.broadcasted_iota(jnp.int32, cos.shape, 1)
    half, quarter = LANES // 2, LANES // 4

    def spread(t, sign):
        swapped = pltpu.roll(t, half, 1)
        full = jnp.where(lane < half, sign * t, swapped)
        narrow = jnp.where(lane < quarter, sign * swapped,
                           jnp.where(lane < half, pltpu.roll(t, LANES - quarter, 1),
                                     jnp.where(lane < half + quarter, sign * t, pltpu.roll(t, quarter, 1))))
        return full, narrow

    c_full_ref[...], c_half_ref[...] = spread(cos, 1.0)
    s_full_ref[...], s_half_ref[...] = spread(sin, -1.0)


def _rmsnorm_and_rope_tables(x2d, g, pos_col):
    t, d = x2d.shape

    def inv_freq(head_dim):
        return jnp.exp(jnp.arange(head_dim // 2, dtype=F32) * (-2.0 * math.log(ROPE_THETA) / head_dim))

    invf = jnp.concatenate([inv_freq(HEAD_DIM), inv_freq(C_HEAD_DIM),
                            jnp.zeros((LANES - HEAD_DIM // 2 - C_HEAD_DIM // 2,), F32)])[None, :]
    tm = min(t, NORM_TM)
    tab = pl.BlockSpec((tm, LANES), lambda i: (i, 0))
    return pl.pallas_call(
        _rmsnorm_rope_kernel,
        grid=(t // tm,),
        in_specs=[pl.BlockSpec((tm, d), lambda i: (i, 0)),
                  pl.BlockSpec((1, d), lambda i: (0, 0)),
                  pl.BlockSpec((tm, 1), lambda i: (i, 0)),
                  pl.BlockSpec((1, LANES), lambda i: (0, 0))],
        out_specs=[pl.BlockSpec((tm, d), lambda i: (i, 0))] + [tab] * 4,
        out_shape=[jax.ShapeDtypeStruct((t, d), BF16)] + [jax.ShapeDtypeStruct((t, LANES), F32)] * 4,
        compiler_params=_params(("arbitrary",)),
        name="rmsnorm_rope_tables",
    )(x2d, g.reshape(1, d), pos_col, invf)


def _rmsnorm_kernel(x_ref, g_ref, o_ref):
    x = x_ref[...]
    ms = jnp.mean(x * x, axis=-1, keepdims=True)
    o_ref[...] = ((x * lax.rsqrt(ms + EPS)) * g_ref[...]).astype(o_ref.dtype)


def _rmsnorm(x2d, g):
    t, d = x2d.shape
    tm = min(t, NORM_TM)
    return pl.pallas_call(
        _rmsnorm_kernel,
        grid=(t // tm,),
        in_specs=[pl.BlockSpec((tm, d), lambda i: (i, 0)),
                  pl.BlockSpec((1, d), lambda i: (0, 0))],
        out_specs=pl.BlockSpec((tm, d), lambda i: (i, 0)),
        out_shape=jax.ShapeDtypeStruct((t, d), BF16),
        compiler_params=_params(("arbitrary",)),
        name="rmsnorm",
    )(x2d, g.reshape(1, d))


ROPE_NONE, ROPE_FULL, ROPE_FIRST_HALF = 0, 1, 2


def _rope_partner(x, head_dim):
    if head_dim == LANES:
        return pltpu.roll(x, LANES // 2, 1)
    lane = lax.broadcasted_iota(jnp.int32, x.shape, 1)
    half = head_dim // 2
    return jnp.where((lane % head_dim) < half, pltpu.roll(x, LANES - half, 1), pltpu.roll(x, half, 1))


def _proj_kernel(types_ref, h_ref, w_ref, cos_ref, sin_ref, side_ref, o_ref, side_o_ref, wbf_ref, *, tn, head_dim):
    j = pl.program_id(0)
    i = pl.program_id(1)
    side_o_ref[...] = side_ref[...].astype(side_o_ref.dtype)

    @pl.when(i == 0)
    def _():
        wbf_ref[...] = w_ref[...].astype(BF16)

    nchunk = tn // LANES
    any_rope = types_ref[pl.num_programs(0) * nchunk + j]

    @pl.when(any_rope == 0)
    def _():
        o_ref[...] = jnp.dot(h_ref[...], wbf_ref[...], preferred_element_type=F32).astype(o_ref.dtype)

    @pl.when(any_rope != 0)
    def _():
        acc = jnp.dot(h_ref[...], wbf_ref[...], preferred_element_type=F32)
        cos = cos_ref[...]
        sin = sin_ref[...]
        upper = lax.broadcasted_iota(jnp.int32, cos.shape, 1) >= LANES // 2
        for c in range(nchunk):
            kind = types_ref[j * nchunk + c]
            x = acc[:, c * LANES:(c + 1) * LANES]
            roped = x * cos + _rope_partner(x, head_dim) * sin
            plain = jnp.logical_or(kind == ROPE_NONE, jnp.logical_and(kind == ROPE_FIRST_HALF, upper))
            o_ref[:, c * LANES:(c + 1) * LANES] = jnp.where(plain, x, roped).astype(o_ref.dtype)


def _proj_plain_kernel(h_ref, w_ref, o_ref, wbf_ref):
    @pl.when(pl.program_id(1) == 0)
    def _():
        wbf_ref[...] = w_ref[...].astype(BF16)

    o_ref[...] = jnp.dot(h_ref[...], wbf_ref[...], preferred_element_type=F32).astype(o_ref.dtype)


def _proj(h, w_stack, idx, tn, rope=None, first_tile=0, n_tiles=None, tm=PROJ_TM, side_stack=None):
    t, d = h.shape
    if n_tiles is None:
        n_tiles = w_stack.shape[2] // tn - first_tile
    assert tn % LANES == 0 and (first_tile + n_tiles) * tn <= w_stack.shape[2]
    n = n_tiles * tn
    tm = min(t, tm)
    grid = (n_tiles, t // tm)
    out_shape = jax.ShapeDtypeStruct((t, n), BF16)
    scratch = [pltpu.VMEM((d, tn), BF16)]
    if rope is None:
        return pl.pallas_call(
            _proj_plain_kernel,
            grid=grid,
            in_specs=[pl.BlockSpec((tm, d), lambda j, i: (i, 0)),
                      pl.BlockSpec((None, d, tn), lambda j, i: (idx, 0, first_tile + j))],
            out_specs=pl.BlockSpec((tm, tn), lambda j, i: (i, j)),
            out_shape=out_shape,
            scratch_shapes=scratch,
            compiler_params=_params(("arbitrary", "arbitrary")),
            name="proj_plain",
        )(h, w_stack)
    kinds, cos, sin, head_dim = rope
    per_tile = kinds.reshape(n_tiles, tn // LANES)
    kinds = jnp.concatenate([kinds, (per_tile != ROPE_NONE).any(axis=1).astype(jnp.int32)])
    side_rows, side_cols = side_stack.shape[1:]
    n_i = t // tm
    n_side = 1 << ((n_tiles * n_i).bit_length() - 1)
    assert side_rows % n_side == 0 and (side_rows // n_side) % 16 == 0

    def side_block(j, i):
        return jnp.minimum(j * n_i + i, n_side - 1)

    return pl.pallas_call(
        functools.partial(_proj_kernel, tn=tn, head_dim=head_dim),
        grid_spec=pltpu.PrefetchScalarGridSpec(
            num_scalar_prefetch=1,
            grid=grid,
            in_specs=[pl.BlockSpec((tm, d), lambda j, i, k: (i, 0)),
                      pl.BlockSpec((None, d, tn), lambda j, i, k: (idx, 0, first_tile + j)),
                      pl.BlockSpec((tm, LANES), lambda j, i, k: (i, 0)),
                      pl.BlockSpec((tm, LANES), lambda j, i, k: (i, 0)),
                      pl.BlockSpec((None, side_rows // n_side, side_cols),
                                   lambda j, i, k: (idx, side_block(j, i), 0))],
            out_specs=[pl.BlockSpec((tm, tn), lambda j, i, k: (i, j)),
                       pl.BlockSpec((side_rows // n_side, side_cols), lambda j, i, k: (side_block(j, i), 0))],
            scratch_shapes=scratch),
        out_shape=[out_shape, jax.ShapeDtypeStruct((side_rows, side_cols), BF16)],
        compiler_params=_params(("arbitrary", "arbitrary")),
        name=f"proj_rope{head_dim}",
    )(kinds, h, w_stack, cos, sin, side_stack)


def _even_chunk_kinds():
    kinds = [ROPE_NONE] * (EVEN_IN // LANES)
    for start in (E_QA, E_KA, E_QB, E_KB):
        for c in range(start, start + A_HEADS):
            kinds[c] = ROPE_FULL
    return jnp.asarray(kinds, jnp.int32)


def _odd_chunk_kinds():
    kinds = [ROPE_NONE] * (ODD_IN // LANES)
    k_end = C_HEADS * C_HEAD_DIM + C_KV_HEADS * C_HEAD_DIM
    for c in range(len(kinds)):
        if (c + 1) * LANES <= k_end:
            kinds[c] = ROPE_FULL
        elif c * LANES < k_end:
            assert k_end - c * LANES == LANES // 2
            kinds[c] = ROPE_FIRST_HALF
    return jnp.asarray(kinds, jnp.int32)


def _locate(parts, block):
    arr, first = [pt for pt in parts if pt[1] <= block][-1]
    assert block - first < arr.shape[2] // LANES
    return arr, block - first


def _scores_t(k, q):
    return lax.dot_general(k, q, (((1,), (1,)), ((), ())), preferred_element_type=F32)


def _transpose_bf16(x):
    return x.astype(F32).T.astype(BF16)


def _softmax_probs_t(tiles, biases, c):
    m = None
    for s, b in zip(tiles, biases):
        if b is None:
            mt = jnp.max(s, axis=0, keepdims=True)
        elif b.shape[0] == 1:
            mt = jnp.max(s, axis=0, keepdims=True) + b
        else:
            mt = jnp.max(s + b, axis=0, keepdims=True)
        m = mt if m is None else jnp.maximum(m, mt)
    shifts = {}
    l = None
    ps = []
    for s, b in zip(tiles, biases):
        if b is None:
            shift = -m
        else:
            if id(b) not in shifts:
                shifts[id(b)] = b - m
            shift = shifts[id(b)]
        p = jnp.exp2((s + shift) * c)
        lt = jnp.sum(p, axis=0, keepdims=True)
        l = lt if l is None else l + lt
        ps.append(p.astype(BF16))
    p_all = ps[0] if len(ps) == 1 else jnp.concatenate(ps, axis=0)
    return p_all, 1.0 / l


SCORE_LOOKAHEAD = 2
DILATED_LOOKAHEAD = 4
MOBA_LOOKAHEAD = 1


def _interleave(generators):
    live = list(generators)
    while live:
        for gen in list(live):
            try:
                next(gen)
            except StopIteration:
                live.remove(gen)


HEADS_PER_STEP = 2


def _head_group_spec(s, first, heads):
    assert first % heads == 0
    return pl.BlockSpec((None, s, heads * LANES), lambda bi, h: (bi, 0, first // heads + h))


def _tile_delta(blk):
    row = lax.broadcasted_iota(jnp.int32, (blk, blk), 0)
    col = lax.broadcasted_iota(jnp.int32, (blk, blk), 1)
    return col - row


def _dilated_head(q_ref, k_ref, v_ref, g_ref, o_ref, cols, masks, *, blk, nblk, c, ahead):
    vt = _transpose_bf16(v_ref[:, cols])
    nears, far_bias = masks

    def scores(u):
        return _scores_t(k_ref[0:(u + 1) * blk, cols], q_ref[u * blk:(u + 1) * blk, cols])

    sts = [scores(u) for u in range(min(ahead, nblk))]
    yield
    for i in range(nblk):
        rows = slice(i * blk, (i + 1) * blk)
        ext = (i + 1) * blk
        if i + ahead < nblk:
            sts.append(scores(i + ahead))
        st = sts[i]
        tiles = [st[n * blk:(n + 1) * blk] for n in range(i + 1)]
        biases = [nears[i - n] if i - n < len(nears) else far_bias for n in range(i + 1)]
        p_all, inv_l = _softmax_probs_t(tiles, biases, c)
        ot = jnp.dot(vt[:, 0:ext], p_all, preferred_element_type=F32) * inv_l
        o_ref[rows, cols] = (ot.T * _silu(g_ref[rows, cols].astype(F32))).astype(o_ref.dtype)
        yield


def _dilated_kernel(q_ref, k_ref, v_ref, g_ref, o_ref, *, blk, nblk, c, ahead, heads):
    cr = _tile_delta(blk)

    def near(delta):
        dist = cr + delta * blk
        cnt = (jnp.where(dist <= 128, 1, 0)
               + jnp.where(jnp.logical_and((dist & 3) == 0, dist <= 512), 1, 0)
               + jnp.where((dist & 15) == 0, 1, 0))
        cnt = jnp.where(dist >= 0, cnt, 0)
        return jnp.where(cnt == 1, 0.0, jnp.where(cnt == 2, 1.0 / c, jnp.where(cnt == 3, math.log2(3.0) / c, NEG)))

    n_near = -(-512 // blk) + 1
    nears = [near(d) for d in range(min(n_near, nblk))]
    far_bias = jnp.where((cr & 15) == 0, 0.0, NEG)
    masks = (nears, far_bias)
    _interleave([_dilated_head(q_ref, k_ref, v_ref, g_ref, o_ref, slice(hh * LANES, (hh + 1) * LANES), masks,
                               blk=blk, nblk=nblk, c=c, ahead=ahead) for hh in range(heads)])


def _dilated_attention(p3):
    b, s, _ = p3.shape
    blk = ATTN_BLK
    heads = HEADS_PER_STEP
    assert s % blk == 0 and A_HEADS % heads == 0
    return pl.pallas_call(
        functools.partial(_dilated_kernel, blk=blk, nblk=s // blk, c=HEAD_DIM ** -0.5 * LOG2E,
                          ahead=DILATED_LOOKAHEAD, heads=heads),
        grid=(b, A_HEADS // heads),
        in_specs=[_head_group_spec(s, first, heads) for first in (E_QA, E_KA, E_VA, E_GATE)],
        out_specs=_head_group_spec(s, 0, heads),
        out_shape=jax.ShapeDtypeStruct((b, s, A_HEADS * HEAD_DIM), BF16),
        compiler_params=_params(("arbitrary", "arbitrary")),
        name="dilated_attention",
    )(p3, p3, p3, p3)


def _moba_head(q_ref, k_ref, v_ref, g_ref, o_ref, cols, *, blk, nblk, c):
    s_len = nblk * blk
    q = q_ref[:, cols]

    def scores(u):
        return _scores_t(k_ref[0:(u + 1) * blk, cols], q[u * blk:(u + 1) * blk, :])

    sts = [scores(u) for u in range(min(MOBA_LOOKAHEAD, nblk))]
    vt = _transpose_bf16(v_ref[:, cols])

    km = jnp.concatenate(
        [jnp.sum(k_ref[n * blk:(n + 1) * blk, cols].astype(F32), axis=0, keepdims=True) * (1.0 / blk)
         for n in range(nblk)], axis=0)
    hi = km.astype(BF16).astype(F32)
    ksplit = jnp.concatenate([hi, km - hi], axis=0).astype(BF16)
    gt2 = _scores_t(ksplit, q)
    gate = gt2[:nblk] + gt2[nblk:]

    nid = lax.broadcasted_iota(jnp.int32, (nblk, s_len), 0)
    own = jnp.right_shift(lax.broadcasted_iota(jnp.int32, (nblk, s_len), 1), blk.bit_length() - 1)
    rank = jnp.zeros((nblk, s_len), jnp.int32)
    for mm in range(nblk):
        gm = gate[mm:mm + 1, :]
        beats = jnp.logical_or(gm > gate, jnp.logical_and(gm == gate, mm < nid))
        rank = rank + jnp.where(jnp.logical_and(beats, mm < own), 1, 0)
    sel_bias = jnp.where(jnp.logical_and(rank < MOBA_TOPK, nid < own), 0.0, NEG)
    causal_bias = jnp.where(_tile_delta(blk) >= 0, 0.0, NEG)
    yield

    for i in range(nblk):
        rows = slice(i * blk, (i + 1) * blk)
        ext = (i + 1) * blk
        if i + MOBA_LOOKAHEAD < nblk:
            sts.append(scores(i + MOBA_LOOKAHEAD))
        st = sts[i]
        tiles = [st[n * blk:(n + 1) * blk] for n in range(i + 1)]
        biases = [sel_bias[n:n + 1, rows] for n in range(i)] + [causal_bias]
        p_all, inv_l = _softmax_probs_t(tiles, biases, c)
        ot = jnp.dot(vt[:, 0:ext], p_all, preferred_element_type=F32) * inv_l
        o_ref[rows, cols] = (ot.T * _silu(g_ref[rows, cols].astype(F32))).astype(o_ref.dtype)
        yield


def _moba_kernel(q_ref, k_ref, v_ref, g_ref, o_ref, *, blk, nblk, c, heads):
    _interleave([_moba_head(q_ref, k_ref, v_ref, g_ref, o_ref, slice(hh * LANES, (hh + 1) * LANES),
                            blk=blk, nblk=nblk, c=c) for hh in range(heads)])


def _moba_attention(p3):
    b, s, _ = p3.shape
    blk = MOBA_BLOCK
    nblk = s // blk
    heads = HEADS_PER_STEP
    assert nblk >= MOBA_TOPK and blk & (blk - 1) == 0 and B_HEADS % heads == 0
    return pl.pallas_call(
        functools.partial(_moba_kernel, blk=blk, nblk=nblk, c=HEAD_DIM ** -0.5 * LOG2E, heads=heads),
        grid=(b, B_HEADS // heads),
        in_specs=[_head_group_spec(s, first, heads) for first in (E_QB, E_KB, E_VB, E_GATE + A_HEADS)],
        out_specs=_head_group_spec(s, 0, heads),
        out_shape=jax.ShapeDtypeStruct((b, s, B_HEADS * HEAD_DIM), BF16),
        compiler_params=_params(("arbitrary", "arbitrary")),
        name="moba_attention",
    )(p3, p3, p3, p3)


def _mem_kernel(*refs, tq, c):
    q_refs = refs[:MEM_HEADS]
    g_refs = refs[MEM_HEADS:2 * MEM_HEADS]
    memkv_ref, o_ref = refs[2 * MEM_HEADS:]
    kvw = MEM_HEADS * HEAD_DIM
    units = [(h, j) for h in range(MEM_HEADS) for j in range(o_ref.shape[0] // tq)]

    def scores(unit):
        h, j = unit
        return _scores_t(memkv_ref[:, h * HEAD_DIM:(h + 1) * HEAD_DIM], q_refs[h][j * tq:(j + 1) * tq, :])

    sts = [scores(u) for u in units[:SCORE_LOOKAHEAD]]
    mvts = [_transpose_bf16(memkv_ref[:, kvw + h * HEAD_DIM:kvw + (h + 1) * HEAD_DIM]) for h in range(MEM_HEADS)]
    for ui, (h, j) in enumerate(units):
        rows = slice(j * tq, (j + 1) * tq)
        if ui + SCORE_LOOKAHEAD < len(units):
            sts.append(scores(units[ui + SCORE_LOOKAHEAD]))
        p_all, inv_l = _softmax_probs_t([sts[ui]], [None], c)
        ot = jnp.dot(mvts[h], p_all, preferred_element_type=F32) * inv_l
        gate = g_refs[h][rows, :].astype(F32)
        o_ref[rows, h * HEAD_DIM:(h + 1) * HEAD_DIM] = (ot.T * _silu(gate)).astype(o_ref.dtype)


def _memkv_kernel(h_ref, w_ref, o_ref):
    o_ref[...] = jnp.dot(h_ref[...], w_ref[...].astype(BF16), preferred_element_type=F32).astype(o_ref.dtype)


def _memkv_all(mem_n, w_stack, batch):
    t, d = mem_n.shape
    nl, _, n = w_stack.shape
    tn = min(n, MEMKV_TN)
    out = pl.pallas_call(
        _memkv_kernel,
        grid=(nl, n // tn),
        in_specs=[pl.BlockSpec((t, d), lambda l, j: (0, 0)),
                  pl.BlockSpec((None, d, tn), lambda l, j: (l, 0, j))],
        out_specs=pl.BlockSpec((None, t, tn), lambda l, j: (l, 0, j)),
        out_shape=jax.ShapeDtypeStruct((nl, t, n), BF16),
        compiler_params=_params(("arbitrary", "arbitrary")),
        name="memkv_proj",
    )(mem_n, w_stack)
    return out.reshape(nl, batch, MEM_LEN, n)


def _mem_attention(parts, memkv_stack, idx, q_block, gate_block):
    b, s, _ = parts[0][0].shape
    arrays, specs = [], []
    for first in (q_block, gate_block):
        for h in range(MEM_HEADS):
            arr, local = _locate(parts, first + h)
            arrays.append(arr)
            specs.append(pl.BlockSpec((None, s, LANES), functools.partial(lambda bi, c: (bi, 0, c), c=local)))
    width = MEM_HEADS * HEAD_DIM
    return pl.pallas_call(
        functools.partial(_mem_kernel, tq=min(s, MEM_TQ), c=HEAD_DIM ** -0.5 * LOG2E),
        grid=(b,),
        in_specs=specs + [pl.BlockSpec((None, None, MEM_LEN, 2 * width), lambda bi: (idx, bi, 0, 0))],
        out_specs=pl.BlockSpec((None, s, width), lambda bi: (bi, 0, 0)),
        out_shape=jax.ShapeDtypeStruct((b, s, width), BF16),
        compiler_params=_params(("arbitrary",)),
        name="memory_attention",
    )(*arrays, memkv_stack)


SWA_QBLK = C_WINDOW
SWA_GATE_BLOCKS = (C_HEADS * C_HEAD_DIM) // LANES
SWA_VT_WIDTH = 2 * LANES


def _swa_kernel(sinks_ref, q_ref, kv_ref, *rest, tq, c, scale):
    gate_refs = rest[:SWA_GATE_BLOCKS]
    o_ref, kvt_ref = rest[SWA_GATE_BLOCKS:]
    i = pl.program_id(1)
    heads_per_group = C_HEADS // C_KV_HEADS
    kw = C_KV_HEADS * C_HEAD_DIM

    @pl.when(i == 0)
    def _():
        for ch in range(kv_ref.shape[0] // LANES):
            t = kv_ref[ch * LANES:(ch + 1) * LANES, LANES:LANES + SWA_VT_WIDTH].astype(F32)
            kvt_ref[ch] = t.T.astype(BF16)

    krow = lax.broadcasted_iota(jnp.int32, (C_WINDOW, SWA_QBLK), 0)
    qcol = lax.broadcasted_iota(jnp.int32, (C_WINDOW, SWA_QBLK), 1)
    from_prev = krow > qcol
    zero_b = jnp.zeros((C_WINDOW, SWA_QBLK), BF16)
    shift = LANES.bit_length() - 1
    units = [(sub, g) for sub in range(tq // SWA_QBLK) for g in range(C_KV_HEADS)]

    def chunks(sub):
        r0 = i * tq + sub * SWA_QBLK
        prev = pl.multiple_of(jnp.maximum(r0 - C_WINDOW, 0), LANES)
        own = pl.multiple_of(r0, LANES)
        return r0, prev, own

    def scores(unit):
        sub, g = unit
        r0, prev, own = chunks(sub)
        rows = slice(sub * SWA_QBLK, (sub + 1) * SWA_QBLK)
        kcols = slice(g * C_HEAD_DIM, (g + 1) * C_HEAD_DIM)
        qg = jnp.concatenate([q_ref[rows, hd * C_HEAD_DIM:(hd + 1) * C_HEAD_DIM]
                              for hd in range(g * heads_per_group, (g + 1) * heads_per_group)], axis=0)
        s_prev = _scores_t(kv_ref[pl.ds(prev, C_WINDOW), kcols], qg)
        if sub == 0:
            s_prev = s_prev + jnp.where(r0 > 0, 0.0, NEG)
        return s_prev, _scores_t(kv_ref[pl.ds(own, C_WINDOW), kcols], qg)

    sts = [scores(u) for u in units[:SCORE_LOOKAHEAD]]
    for ui, (sub, g) in enumerate(units):
        if ui + SCORE_LOOKAHEAD < len(units):
            sts.append(scores(units[ui + SCORE_LOOKAHEAD]))
        s_prev, s_own = sts[ui]
        _, prev, own = chunks(sub)
        rows = slice(sub * SWA_QBLK, (sub + 1) * SWA_QBLK)
        v0 = (kw + g * C_HEAD_DIM) - LANES
        vt_prev = kvt_ref[jnp.right_shift(prev, shift)][v0:v0 + C_HEAD_DIM, :]
        vt_own = kvt_ref[jnp.right_shift(own, shift)][v0:v0 + C_HEAD_DIM, :]
        p_prev, p_own, inv_l = [], [], []
        for u in range(heads_per_group):
            cols = slice(u * SWA_QBLK, (u + 1) * SWA_QBLK)
            s_h = jnp.where(from_prev, s_prev[:, cols], s_own[:, cols])
            sink = sinks_ref[g * heads_per_group + u] * (1.0 / scale)
            m = jnp.maximum(jnp.max(s_h, axis=0, keepdims=True), sink)
            p = jnp.exp2((s_h - m) * c)
            inv_l.append(1.0 / (jnp.sum(p, axis=0, keepdims=True) + jnp.exp2((sink - m) * c)))
            pb = p.astype(BF16)
            p_prev.append(jnp.where(from_prev, pb, zero_b))
            p_own.append(jnp.where(from_prev, zero_b, pb))
        ot = (jnp.dot(vt_prev, jnp.concatenate(p_prev, axis=1), preferred_element_type=F32)
              + jnp.dot(vt_own, jnp.concatenate(p_own, axis=1), preferred_element_type=F32))
        ot = ot * jnp.concatenate(inv_l, axis=1)
        for pair in range(heads_per_group // 2):
            both = jnp.concatenate([ot[:, (2 * pair + u) * SWA_QBLK:(2 * pair + u + 1) * SWA_QBLK]
                                    for u in range(2)], axis=0)
            blk_i = (g * heads_per_group) // 2 + pair
            gate = gate_refs[blk_i][rows, :].astype(F32)
            o_ref[rows, blk_i * LANES:(blk_i + 1) * LANES] = (both.T * _silu(gate)).astype(o_ref.dtype)


def _swa_attention(parts, sinks):
    p_qkv = parts[0][0]
    b, s, _ = p_qkv.shape
    tq = min(s, SWA_TQ)
    qw = C_HEADS * C_HEAD_DIM
    kvw = 2 * C_KV_HEADS * C_HEAD_DIM
    assert (O_KV * LANES) % kvw == 0 and kvw == LANES + SWA_VT_WIDTH and p_qkv.shape[2] >= qw + kvw
    gate_arrays, gate_specs = [], []
    for u in range(SWA_GATE_BLOCKS):
        arr, local = _locate(parts, O_GATE + u)
        gate_arrays.append(arr)
        gate_specs.append(pl.BlockSpec((None, tq, LANES),
                                       functools.partial(lambda bi, i, sk, c: (bi, i, c), c=local)))
    scale = C_HEAD_DIM ** -0.5
    return pl.pallas_call(
        functools.partial(_swa_kernel, tq=tq, c=scale * LOG2E, scale=scale),
        grid_spec=pltpu.PrefetchScalarGridSpec(
            num_scalar_prefetch=1,
            grid=(b, s // tq),
            in_specs=[pl.BlockSpec((None, tq, qw), lambda bi, i, sk: (bi, i, 0)),
                      pl.BlockSpec((None, s, kvw), lambda bi, i, sk: (bi, 0, (O_KV * LANES) // kvw))]
                     + gate_specs,
            out_specs=pl.BlockSpec((None, tq, qw), lambda bi, i, sk: (bi, i, 0)),
            scratch_shapes=[pltpu.VMEM((s // LANES, SWA_VT_WIDTH, LANES), BF16)]),
        out_shape=jax.ShapeDtypeStruct((b, s, qw), BF16),
        compiler_params=_params(("arbitrary", "arbitrary")),
        name="swa_sink_attention",
    )(sinks, p_qkv, p_qkv, *gate_arrays)


def _outproj_kernel(*refs, widths, tn, emit_x):
    ny = len(widths)
    y_refs = refs[:ny]
    w_ref, x_ref, g_ref = refs[ny:ny + 3]
    ho_ref = refs[-1]
    hold_ref = refs[ny + 3] if emit_x else ho_ref
    tm, d = x_ref.shape
    ss = jnp.zeros((tm, 1), F32)
    for j in range(d // tn):
        cols = slice(j * tn, (j + 1) * tn)
        acc = x_ref[:, cols]
        off = 0
        for y_ref, wd in zip(y_refs, widths):
            acc = acc + jnp.dot(y_ref[...], w_ref[off:off + wd, cols], preferred_element_type=F32)
            off += wd
        hold_ref[:, cols] = acc
        ss = ss + jnp.sum(acc * acc, axis=1, keepdims=True)
    r = lax.rsqrt(ss * (1.0 / d) + EPS)
    for j in range(d // tn):
        cols = slice(j * tn, (j + 1) * tn)
        ho_ref[:, cols] = ((hold_ref[:, cols] * r) * g_ref[:, cols]).astype(ho_ref.dtype)


def _outproj(ys, w_bf16, x2d, g_next, final):
    t, d = x2d.shape
    widths = tuple(y.shape[1] for y in ys)
    kdim = w_bf16.shape[0]
    assert sum(widths) == kdim and w_bf16.shape[1] == d
    tm = min(t, OUT_TM)
    tn = min(d, OUT_TN)
    row_spec = pl.BlockSpec((tm, d), lambda i: (i, 0))
    in_specs = [pl.BlockSpec((tm, wd), lambda i: (i, 0)) for wd in widths]
    in_specs += [pl.BlockSpec((kdim, d), lambda i: (0, 0)), row_spec,
                 pl.BlockSpec((1, d), lambda i: (0, 0))]
    if final:
        out_specs = [row_spec]
        out_shape = [jax.ShapeDtypeStruct((t, d), F32)]
    else:
        out_specs = [row_spec, row_spec]
        out_shape = [jax.ShapeDtypeStruct((t, d), F32), jax.ShapeDtypeStruct((t, d), BF16)]
    res = pl.pallas_call(
        functools.partial(_outproj_kernel, widths=widths, tn=tn, emit_x=not final),
        grid=(t // tm,),
        in_specs=in_specs,
        out_specs=out_specs,
        out_shape=out_shape,
        compiler_params=_params(("arbitrary",)),
        name="outproj_final" if final else "outproj",
    )(*ys, w_bf16, x2d, g_next.reshape(1, d))
    return res[0] if final else (res[0], res[1])


def kernel(x, mem, positions, even_norm, even_w_in, even_w_mem_kv, even_w_out, odd_norm, odd_w_in,
           odd_w_mem_kv, odd_w_out, odd_sinks, mem_norm, final_norm):
    b, s, d = x.shape
    t = b * s
    depth = even_norm.shape[0] + odd_norm.shape[0]
    pos_col = positions.reshape(t, 1)
    kinds_e = _even_chunk_kinds()
    kinds_o = _odd_chunk_kinds()
    mem_n = _rmsnorm(mem.reshape(b * MEM_LEN, d), mem_norm)
    memkv_even = _memkv_all(mem_n, even_w_mem_kv, b)
    memkv_odd = _memkv_all(mem_n, odd_w_mem_kv, b)

    x2d = x.reshape(t, d)
    h, cos_e, sin_e, cos_o, sin_o = _rmsnorm_and_rope_tables(x2d, even_norm[0], pos_col)
    out = None
    for layer in range(depth):
        idx = layer // 2
        last = layer == depth - 1
        if last:
            g_next = final_norm
        elif layer % 2 == 0:
            g_next = odd_norm[idx]
        else:
            g_next = even_norm[idx + 1]
        if layer % 2 == 0:
            p3, w_out = _proj(h, even_w_in, idx, EVEN_TN, (kinds_e, cos_e, sin_e, HEAD_DIM), side_stack=even_w_out)
            p3 = p3.reshape(b, s, EVEN_IN)
            ys = [_dilated_attention(p3), _moba_attention(p3),
                  _mem_attention([(p3, 0)], memkv_even, idx, E_QM, E_GATE + 2 * A_HEADS)]
        else:
            wide = ODD_SPLIT_TILES * ODD_TN
            pa, w_out = _proj(h, odd_w_in, idx, wide // 2, (kinds_o[:wide // LANES], cos_o, sin_o, C_HEAD_DIM),
                              n_tiles=2, tm=ODD_WIDE_TM, side_stack=odd_w_out)
            pa = pa.reshape(b, s, wide)
            pb = _proj(h, odd_w_in, idx, ODD_TN, first_tile=ODD_SPLIT_TILES).reshape(b, s, ODD_IN - wide)
            parts = [(pa, 0), (pb, wide // LANES)]
            ys = [_swa_attention(parts, odd_sinks[idx]),
                  _mem_attention(parts, memkv_odd, idx, O_QM, O_GATE + SWA_GATE_BLOCKS)]
        ys = [y.reshape(t, y.shape[-1]) for y in ys]
        if last:
            out = _outproj(ys, w_out, x2d, g_next, final=True)
        else:
            x2d, h = _outproj(ys, w_out, x2d, g_next, final=False)
    return out.reshape(b, s, d)
```

```python
import functools
import math

import jax
import jax.numpy as jnp
from jax import lax
from jax.experimental import pallas as pl
from jax.experimental.pallas import tpu as pltpu

F32 = jnp.float32
BF16 = jnp.bfloat16

LANES = 128
HEAD_DIM = 128
MEM_LEN = 256
MEM_HEADS = 4
A_HEADS = 6
B_HEADS = 6
MOBA_BLOCK = 256
MOBA_TOPK = 3
C_HEADS = 24
C_KV_HEADS = 3
C_HEAD_DIM = 64
C_WINDOW = 128
ROPE_THETA = 10000.0
EPS = 1e-6
NEG = -1e30
LOG2E = 1.4426950408889634

EVEN_WIDTH = (A_HEADS + B_HEADS + MEM_HEADS) * HEAD_DIM
ODD_WIDTH = C_HEADS * C_HEAD_DIM + MEM_HEADS * HEAD_DIM
EVEN_IN = 3 * A_HEADS * HEAD_DIM + 3 * B_HEADS * HEAD_DIM + MEM_HEADS * HEAD_DIM + EVEN_WIDTH
ODD_IN = C_HEADS * C_HEAD_DIM + 2 * C_KV_HEADS * C_HEAD_DIM + MEM_HEADS * HEAD_DIM + ODD_WIDTH

E_QA, E_KA, E_VA = 0, A_HEADS, 2 * A_HEADS
E_QB, E_KB, E_VB = 3 * A_HEADS, 3 * A_HEADS + B_HEADS, 3 * A_HEADS + 2 * B_HEADS
E_QM = 3 * A_HEADS + 3 * B_HEADS
E_GATE = E_QM + MEM_HEADS
O_Q = 0
O_KV = (C_HEADS * C_HEAD_DIM) // LANES
O_QM = O_KV + (2 * C_KV_HEADS * C_HEAD_DIM) // LANES
O_GATE = O_QM + MEM_HEADS

PROJ_TM = 1024
EVEN_TN = 1024
MEMKV_TN = 512
ODD_TN = 896
ODD_SPLIT_TILES = 4
ODD_WIDE_TM = 512
OUT_TM, OUT_TN = 512, 512
NORM_TM = 1024
ATTN_BLK = 256
MEM_TQ = 512
SWA_TQ = 1024

VMEM_LIMIT = 56 * 1024 * 1024


def _params(sem):
    return pltpu.CompilerParams(dimension_semantics=sem, vmem_limit_bytes=VMEM_LIMIT)


def _silu(g):
    half = 0.5 * g
    return half + half * jnp.tanh(half)


def _rmsnorm_rope_kernel(x_ref, g_ref, pos_ref, invf_ref, o_ref, c_full_ref, s_full_ref, c_half_ref, s_half_ref):
    _rmsnorm_kernel(x_ref, g_ref, o_ref)
    ang = pos_ref[...].astype(F32) * invf_ref[...]
    cos = jnp.cos(ang)
    sin = jnp.sin(ang)
    lane = lax.broadcasted_iota(jnp.int32, cos.shape, 1)
    half, quarter = LANES // 2, LANES // 4

    def spread(t, sign):
        swapped = pltpu.roll(t, half, 1)
        full = jnp.where(lane < half, sign * t, swapped)
        narrow = jnp.where(lane < quarter, sign * swapped,
                           jnp.where(lane < half, pltpu.roll(t, LANES - quarter, 1),
                                     jnp.where(lane < half + quarter, sign * t, pltpu.roll(t, quarter, 1))))
        return full, narrow

    c_full_ref[...], c_half_ref[...] = spread(cos, 1.0)
    s_full_ref[...], s_half_ref[...] = spread(sin, -1.0)


def _rmsnorm_and_rope_tables(x2d, g, pos_col):
    t, d = x2d.shape

    def inv_freq(head_dim):
        return jnp.exp(jnp.arange(head_dim // 2, dtype=F32) * (-2.0 * math.log(ROPE_THETA) / head_dim))

    invf = jnp.concatenate([inv_freq(HEAD_DIM), inv_freq(C_HEAD_DIM),
                            jnp.zeros((LANES - HEAD_DIM // 2 - C_HEAD_DIM // 2,), F32)])[None, :]
    tm = min(t, NORM_TM)
    tab = pl.BlockSpec((tm, LANES), lambda i: (i, 0))
    return pl.pallas_call(
        _rmsnorm_rope_kernel,
        grid=(t // tm,),
        in_specs=[pl.BlockSpec((tm, d), lambda i: (i, 0)),
                  pl.BlockSpec((1, d), lambda i: (0, 0)),
                  pl.BlockSpec((tm, 1), lambda i: (i, 0)),
                  pl.BlockSpec((1, LANES), lambda i: (0, 0))],
        out_specs=[pl.BlockSpec((tm, d), lambda i: (i, 0))] + [tab] * 4,
        out_shape=[jax.ShapeDtypeStruct((t, d), BF16)] + [jax.ShapeDtypeStruct((t, LANES), F32)] * 4,
        compiler_params=_params(("arbitrary",)),
        name="rmsnorm_rope_tables",
    )(x2d, g.reshape(1, d), pos_col, invf)


def _rmsnorm_kernel(x_ref, g_ref, o_ref):
    x = x_ref[...]
    ms = jnp.mean(x * x, axis=-1, keepdims=True)
    o_ref[...] = ((x * lax.rsqrt(ms + EPS)) * g_ref[...]).astype(o_ref.dtype)


def _rmsnorm(x2d, g):
    t, d = x2d.shape
    tm = min(t, NORM_TM)
    return pl.pallas_call(
        _rmsnorm_kernel,
        grid=(t // tm,),
        in_specs=[pl.BlockSpec((tm, d), lambda i: (i, 0)),
                  pl.BlockSpec((1, d), lambda i: (0, 0))],
        out_specs=pl.BlockSpec((tm, d), lambda i: (i, 0)),
        out_shape=jax.ShapeDtypeStruct((t, d), BF16),
        compiler_params=_params(("arbitrary",)),
        name="rmsnorm",
    )(x2d, g.reshape(1, d))


ROPE_NONE, ROPE_FULL, ROPE_FIRST_HALF = 0, 1, 2


def _rope_partner(x, head_dim):
    if head_dim == LANES:
        return pltpu.roll(x, LANES // 2, 1)
    lane = lax.broadcasted_iota(jnp.int32, x.shape, 1)
    half = head_dim // 2
    return jnp.where((lane % head_dim) < half, pltpu.roll(x, LANES - half, 1), pltpu.roll(x, half, 1))


def _proj_kernel(types_ref, h_ref, w_ref, cos_ref, sin_ref, side_ref, o_ref, side_o_ref, wbf_ref, *, tn, head_dim):
    j = pl.program_id(0)
    i = pl.program_id(1)
    side_o_ref[...] = side_ref[...].astype(side_o_ref.dtype)

    @pl.when(i == 0)
    def _():
        wbf_ref[...] = w_ref[...].astype(BF16)

    nchunk = tn // LANES
    any_rope = types_ref[pl.num_programs(0) * nchunk + j]

    @pl.when(any_rope == 0)
    def _():
        o_ref[...] = jnp.dot(h_ref[...], wbf_ref[...], preferred_element_type=F32).astype(o_ref.dtype)

    @pl.when(any_rope != 0)
    def _():
        acc = jnp.dot(h_ref[...], wbf_ref[...], preferred_element_type=F32)
        cos = cos_ref[...]
        sin = sin_ref[...]
        upper = lax.broadcasted_iota(jnp.int32, cos.shape, 1) >= LANES // 2
        for c in range(nchunk):
            kind = types_ref[j * nchunk + c]
            x = acc[:, c * LANES:(c + 1) * LANES]
            roped = x * cos + _rope_partner(x, head_dim) * sin
            plain = jnp.logical_or(kind == ROPE_NONE, jnp.logical_and(kind == ROPE_FIRST_HALF, upper))
            o_ref[:, c * LANES:(c + 1) * LANES] = jnp.where(plain, x, roped).astype(o_ref.dtype)


def _proj_plain_kernel(h_ref, w_ref, o_ref, wbf_ref):
    @pl.when(pl.program_id(1) == 0)
    def _():
        wbf_ref[...] = w_ref[...].astype(BF16)

    o_ref[...] = jnp.dot(h_ref[...], wbf_ref[...], preferred_element_type=F32).astype(o_ref.dtype)


def _proj(h, w_stack, idx, tn, rope=None, first_tile=0, n_tiles=None, tm=PROJ_TM, side_stack=None):
    t, d = h.shape
    if n_tiles is None:
        n_tiles = w_stack.shape[2] // tn - first_tile
    assert tn % LANES == 0 and (first_tile + n_tiles) * tn <= w_stack.shape[2]
    n = n_tiles * tn
    tm = min(t, tm)
    grid = (n_tiles, t // tm)
    out_shape = jax.ShapeDtypeStruct((t, n), BF16)
    scratch = [pltpu.VMEM((d, tn), BF16)]
    if rope is None:
        return pl.pallas_call(
            _proj_plain_kernel,
            grid=grid,
            in_specs=[pl.BlockSpec((tm, d), lambda j, i: (i, 0)),
                      pl.BlockSpec((None, d, tn), lambda j, i: (idx, 0, first_tile + j))],
            out_specs=pl.BlockSpec((tm, tn), lambda j, i: (i, j)),
            out_shape=out_shape,
            scratch_shapes=scratch,
            compiler_params=_params(("arbitrary", "arbitrary")),
            name="proj_plain",
        )(h, w_stack)
    kinds, cos, sin, head_dim = rope
    per_tile = kinds.reshape(n_tiles, tn // LANES)
    kinds = jnp.concatenate([kinds, (per_tile != ROPE_NONE).any(axis=1).astype(jnp.int32)])
    side_rows, side_cols = side_stack.shape[1:]
    n_i = t // tm
    n_side = 1 << ((n_tiles * n_i).bit_length() - 1)
    assert side_rows % n_side == 0 and (side_rows // n_side) % 16 == 0

    def side_block(j, i):
        return jnp.minimum(j * n_i + i, n_side - 1)

    return pl.pallas_call(
        functools.partial(_proj_kernel, tn=tn, head_dim=head_dim),
        grid_spec=pltpu.PrefetchScalarGridSpec(
            num_scalar_prefetch=1,
            grid=grid,
            in_specs=[pl.BlockSpec((tm, d), lambda j, i, k: (i, 0)),
                      pl.BlockSpec((None, d, tn), lambda j, i, k: (idx, 0, first_tile + j)),
                      pl.BlockSpec((tm, LANES), lambda j, i, k: (i, 0)),
                      pl.BlockSpec((tm, LANES), lambda j, i, k: (i, 0)),
                      pl.BlockSpec((None, side_rows // n_side, side_cols),
                                   lambda j, i, k: (idx, side_block(j, i), 0))],
            out_specs=[pl.BlockSpec((tm, tn), lambda j, i, k: (i, j)),
                       pl.BlockSpec((side_rows // n_side, side_cols), lambda j, i, k: (side_block(j, i), 0))],
            scratch_shapes=scratch),
        out_shape=[out_shape, jax.ShapeDtypeStruct((side_rows, side_cols), BF16)],
        compiler_params=_params(("arbitrary", "arbitrary")),
        name=f"proj_rope{head_dim}",
    )(kinds, h, w_stack, cos, sin, side_stack)


def _even_chunk_kinds():
    kinds = [ROPE_NONE] * (EVEN_IN // LANES)
    for start in (E_QA, E_KA, E_QB, E_KB):
        for c in range(start, start + A_HEADS):
            kinds[c] = ROPE_FULL
    return jnp.asarray(kinds, jnp.int32)


def _odd_chunk_kinds():
    kinds = [ROPE_NONE] * (ODD_IN // LANES)
    k_end = C_HEADS * C_HEAD_DIM + C_KV_HEADS * C_HEAD_DIM
    for c in range(len(kinds)):
        if (c + 1) * LANES <= k_end:
            kinds[c] = ROPE_FULL
        elif c * LANES < k_end:
            assert k_end - c * LANES == LANES // 2
            kinds[c] = ROPE_FIRST_HALF
    return jnp.asarray(kinds, jnp.int32)


def _locate(parts, block):
    arr, first = [pt for pt in parts if pt[1] <= block][-1]
    assert block - first < arr.shape[2] // LANES
    return arr, block - first


def _scores_t(k, q):
    return lax.dot_general(k, q, (((1,), (1,)), ((), ())), preferred_element_type=F32)


def _transpose_bf16(x):
    return x.astype(F32).T.astype(BF16)


def _softmax_probs_t(tiles, biases, c):
    m = None
    for s, b in zip(tiles, biases):
        if b is None:
            mt = jnp.max(s, axis=0, keepdims=True)
        elif b.shape[0] == 1:
            mt = jnp.max(s, axis=0, keepdims=True) + b
        else:
            mt = jnp.max(s + b, axis=0, keepdims=True)
        m = mt if m is None else jnp.maximum(m, mt)
    shifts = {}
    l = None
    ps = []
    for s, b in zip(tiles, biases):
        if b is None:
            shift = -m
        else:
            if id(b) not in shifts:
                shifts[id(b)] = b - m
            shift = shifts[id(b)]
        p = jnp.exp2((s + shift) * c)
        lt = jnp.sum(p, axis=0, keepdims=True)
        l = lt if l is None else l + lt
        ps.append(p.astype(BF16))
    p_all = ps[0] if len(ps) == 1 else jnp.concatenate(ps, axis=0)
    return p_all, 1.0 / l


SCORE_LOOKAHEAD = 2
DILATED_LOOKAHEAD = 4
MOBA_LOOKAHEAD = 1


def _interleave(generators):
    live = list(generators)
    while live:
        for gen in list(live):
            try:
                next(gen)
            except StopIteration:
                live.remove(gen)


HEADS_PER_STEP = 2


def _head_group_spec(s, first, heads):
    assert first % heads == 0
    return pl.BlockSpec((None, s, heads * LANES), lambda bi, h: (bi, 0, first // heads + h))


def _tile_delta(blk):
    row = lax.broadcasted_iota(jnp.int32, (blk, blk), 0)
    col = lax.broadcasted_iota(jnp.int32, (blk, blk), 1)
    return col - row


def _dilated_head(q_ref, k_ref, v_ref, g_ref, o_ref, cols, masks, *, blk, nblk, c, ahead):
    vt = _transpose_bf16(v_ref[:, cols])
    nears, far_bias = masks

    def scores(u):
        return _scores_t(k_ref[0:(u + 1) * blk, cols], q_ref[u * blk:(u + 1) * blk, cols])

    sts = [scores(u) for u in range(min(ahead, nblk))]
    yield
    for i in range(nblk):
        rows = slice(i * blk, (i + 1) * blk)
        ext = (i + 1) * blk
        if i + ahead < nblk:
            sts.append(scores(i + ahead))
        st = sts[i]
        tiles = [st[n * blk:(n + 1) * blk] for n in range(i + 1)]
        biases = [nears[i - n] if i - n < len(nears) else far_bias for n in range(i + 1)]
        p_all, inv_l = _softmax_probs_t(tiles, biases, c)
        ot = jnp.dot(vt[:, 0:ext], p_all, preferred_element_type=F32) * inv_l
        o_ref[rows, cols] = (ot.T * _silu(g_ref[rows, cols].astype(F32))).astype(o_ref.dtype)
        yield


def _dilated_kernel(q_ref, k_ref, v_ref, g_ref, o_ref, *, blk, nblk, c, ahead, heads):
    cr = _tile_delta(blk)

    def near(delta):
        dist = cr + delta * blk
        cnt = (jnp.where(dist <= 128, 1, 0)
               + jnp.where(jnp.logical_and((dist & 3) == 0, dist <= 512), 1, 0)
               + jnp.where((dist & 15) == 0, 1, 0))
        cnt = jnp.where(dist >= 0, cnt, 0)
        return jnp.where(cnt == 1, 0.0, jnp.where(cnt == 2, 1.0 / c, jnp.where(cnt == 3, math.log2(3.0) / c, NEG)))

    n_near = -(-512 // blk) + 1
    nears = [near(d) for d in range(min(n_near, nblk))]
    far_bias = jnp.where((cr & 15) == 0, 0.0, NEG)
    masks = (nears, far_bias)
    _interleave([_dilated_head(q_ref, k_ref, v_ref, g_ref, o_ref, slice(hh * LANES, (hh + 1) * LANES), masks,
                               blk=blk, nblk=nblk, c=c, ahead=ahead) for hh in range(heads)])


def _dilated_attention(p3):
    b, s, _ = p3.shape
    blk = ATTN_BLK
    heads = HEADS_PER_STEP
    assert s % blk == 0 and A_HEADS % heads == 0
    return pl.pallas_call(
        functools.partial(_dilated_kernel, blk=blk, nblk=s // blk, c=HEAD_DIM ** -0.5 * LOG2E,
                          ahead=DILATED_LOOKAHEAD, heads=heads),
        grid=(b, A_HEADS // heads),
        in_specs=[_head_group_spec(s, first, heads) for first in (E_QA, E_KA, E_VA, E_GATE)],
        out_specs=_head_group_spec(s, 0, heads),
        out_shape=jax.ShapeDtypeStruct((b, s, A_HEADS * HEAD_DIM), BF16),
        compiler_params=_params(("arbitrary", "arbitrary")),
        name="dilated_attention",
    )(p3, p3, p3, p3)


def _moba_head(q_ref, k_ref, v_ref, g_ref, o_ref, cols, *, blk, nblk, c):
    s_len = nblk * blk
    q = q_ref[:, cols]

    def scores(u):
        return _scores_t(k_ref[0:(u + 1) * blk, cols], q[u * blk:(u + 1) * blk, :])

    sts = [scores(u) for u in range(min(MOBA_LOOKAHEAD, nblk))]
    vt = _transpose_bf16(v_ref[:, cols])

    km = jnp.concatenate(
        [jnp.sum(k_ref[n * blk:(n + 1) * blk, cols].astype(F32), axis=0, keepdims=True) * (1.0 / blk)
         for n in range(nblk)], axis=0)
    hi = km.astype(BF16).astype(F32)
    ksplit = jnp.concatenate([hi, km - hi], axis=0).astype(BF16)
    gt2 = _scores_t(ksplit, q)
    gate = gt2[:nblk] + gt2[nblk:]

    nid = lax.broadcasted_iota(jnp.int32, (nblk, s_len), 0)
    own = jnp.right_shift(lax.broadcasted_iota(jnp.int32, (nblk, s_len), 1), blk.bit_length() - 1)
    rank = jnp.zeros((nblk, s_len), jnp.int32)
    for mm in range(nblk):
        gm = gate[mm:mm + 1, :]
        beats = jnp.logical_or(gm > gate, jnp.logical_and(gm == gate, mm < nid))
        rank = rank + jnp.where(jnp.logical_and(beats, mm < own), 1, 0)
    sel_bias = jnp.where(jnp.logical_and(rank < MOBA_TOPK, nid < own), 0.0, NEG)
    causal_bias = jnp.where(_tile_delta(blk) >= 0, 0.0, NEG)
    yield

    for i in range(nblk):
        rows = slice(i * blk, (i + 1) * blk)
        ext = (i + 1) * blk
        if i + MOBA_LOOKAHEAD < nblk:
            sts.append(scores(i + MOBA_LOOKAHEAD))
        st = sts[i]
        tiles = [st[n * blk:(n + 1) * blk] for n in range(i + 1)]
        biases = [sel_bias[n:n + 1, rows] for n in range(i)] + [causal_bias]
        p_all, inv_l = _softmax_probs_t(tiles, biases, c)
        ot = jnp.dot(vt[:, 0:ext], p_all, preferred_element_type=F32) * inv_l
        o_ref[rows, cols] = (ot.T * _silu(g_ref[rows, cols].astype(F32))).astype(o_ref.dtype)
        yield


def _moba_kernel(q_ref, k_ref, v_ref, g_ref, o_ref, *, blk, nblk, c, heads):
    _interleave([_moba_head(q_ref, k_ref, v_ref, g_ref, o_ref, slice(hh * LANES, (hh + 1) * LANES),
                            blk=blk, nblk=nblk, c=c) for hh in range(heads)])


def _moba_attention(p3):
    b, s, _ = p3.shape
    blk = MOBA_BLOCK
    nblk = s // blk
    heads = HEADS_PER_STEP
    assert nblk >= MOBA_TOPK and blk & (blk - 1) == 0 and B_HEADS % heads == 0
    return pl.pallas_call(
        functools.partial(_moba_kernel, blk=blk, nblk=nblk, c=HEAD_DIM ** -0.5 * LOG2E, heads=heads),
        grid=(b, B_HEADS // heads),
        in_specs=[_head_group_spec(s, first, heads) for first in (E_QB, E_KB, E_VB, E_GATE + A_HEADS)],
        out_specs=_head_group_spec(s, 0, heads),
        out_shape=jax.ShapeDtypeStruct((b, s, B_HEADS * HEAD_DIM), BF16),
        compiler_params=_params(("arbitrary", "arbitrary")),
        name="moba_attention",
    )(p3, p3, p3, p3)


def _mem_kernel(*refs, tq, c):
    q_refs = refs[:MEM_HEADS]
    g_refs = refs[MEM_HEADS:2 * MEM_HEADS]
    memkv_ref, o_ref = refs[2 * MEM_HEADS:]
    kvw = MEM_HEADS * HEAD_DIM
    units = [(h, j) for h in range(MEM_HEADS) for j in range(o_ref.shape[0] // tq)]

    def scores(unit):
        h, j = unit
        return _scores_t(memkv_ref[:, h * HEAD_DIM:(h + 1) * HEAD_DIM], q_refs[h][j * tq:(j + 1) * tq, :])

    sts = [scores(u) for u in units[:SCORE_LOOKAHEAD]]
    mvts = [_transpose_bf16(memkv_ref[:, kvw + h * HEAD_DIM:kvw + (h + 1) * HEAD_DIM]) for h in range(MEM_HEADS)]
    for ui, (h, j) in enumerate(units):
        rows = slice(j * tq, (j + 1) * tq)
        if ui + SCORE_LOOKAHEAD < len(units):
            sts.append(scores(units[ui + SCORE_LOOKAHEAD]))
        p_all, inv_l = _softmax_probs_t([sts[ui]], [None], c)
        ot = jnp.dot(mvts[h], p_all, preferred_element_type=F32) * inv_l
        gate = g_refs[h][rows, :].astype(F32)
        o_ref[rows, h * HEAD_DIM:(h + 1) * HEAD_DIM] = (ot.T * _silu(gate)).astype(o_ref.dtype)


def _memkv_kernel(h_ref, w_ref, o_ref):
    o_ref[...] = jnp.dot(h_ref[...], w_ref[...].astype(BF16), preferred_element_type=F32).astype(o_ref.dtype)


def _memkv_all(mem_n, w_stack, batch):
    t, d = mem_n.shape
    nl, _, n = w_stack.shape
    tn = min(n, MEMKV_TN)
    out = pl.pallas_call(
        _memkv_kernel,
        grid=(nl, n // tn),
        in_specs=[pl.BlockSpec((t, d), lambda l, j: (0, 0)),
                  pl.BlockSpec((None, d, tn), lambda l, j: (l, 0, j))],
        out_specs=pl.BlockSpec((None, t, tn), lambda l, j: (l, 0, j)),
        out_shape=jax.ShapeDtypeStruct((nl, t, n), BF16),
        compiler_params=_params(("arbitrary", "arbitrary")),
        name="memkv_proj",
    )(mem_n, w_stack)
    return out.reshape(nl, batch, MEM_LEN, n)


def _mem_attention(parts, memkv_stack, idx, q_block, gate_block):
    b, s, _ = parts[0][0].shape
    arrays, specs = [], []
    for first in (q_block, gate_block):
        for h in range(MEM_HEADS):
            arr, local = _locate(parts, first + h)
            arrays.append(arr)
            specs.append(pl.BlockSpec((None, s, LANES), functools.partial(lambda bi, c: (bi, 0, c), c=local)))
    width = MEM_HEADS * HEAD_DIM
    return pl.pallas_call(
        functools.partial(_mem_kernel, tq=min(s, MEM_TQ), c=HEAD_DIM ** -0.5 * LOG2E),
        grid=(b,),
        in_specs=specs + [pl.BlockSpec((None, None, MEM_LEN, 2 * width), lambda bi: (idx, bi, 0, 0))],
        out_specs=pl.BlockSpec((None, s, width), lambda bi: (bi, 0, 0)),
        out_shape=jax.ShapeDtypeStruct((b, s, width), BF16),
        compiler_params=_params(("arbitrary",)),
        name="memory_attention",
    )(*arrays, memkv_stack)


SWA_QBLK = C_WINDOW
SWA_GATE_BLOCKS = (C_HEADS * C_HEAD_DIM) // LANES
SWA_VT_WIDTH = 2 * LANES


def _swa_kernel(sinks_ref, q_ref, kv_ref, *rest, tq, c, scale):
    gate_refs = rest[:SWA_GATE_BLOCKS]
    o_ref, kvt_ref = rest[SWA_GATE_BLOCKS:]
    i = pl.program_id(1)
    heads_per_group = C_HEADS // C_KV_HEADS
    kw = C_KV_HEADS * C_HEAD_DIM

    @pl.when(i == 0)
    def _():
        for ch in range(kv_ref.shape[0] // LANES):
            t = kv_ref[ch * LANES:(ch + 1) * LANES, LANES:LANES + SWA_VT_WIDTH].astype(F32)
            kvt_ref[ch] = t.T.astype(BF16)

    krow = lax.broadcasted_iota(jnp.int32, (C_WINDOW, SWA_QBLK), 0)
    qcol = lax.broadcasted_iota(jnp.int32, (C_WINDOW, SWA_QBLK), 1)
    from_prev = krow > qcol
    zero_b = jnp.zeros((C_WINDOW, SWA_QBLK), BF16)
    shift = LANES.bit_length() - 1
    units = [(sub, g) for sub in range(tq // SWA_QBLK) for g in range(C_KV_HEADS)]

    def chunks(sub):
        r0 = i * tq + sub * SWA_QBLK
        prev = pl.multiple_of(jnp.maximum(r0 - C_WINDOW, 0), LANES)
        own = pl.multiple_of(r0, LANES)
        return r0, prev, own

    def scores(unit):
        sub, g = unit
        r0, prev, own = chunks(sub)
        rows = slice(sub * SWA_QBLK, (sub + 1) * SWA_QBLK)
        kcols = slice(g * C_HEAD_DIM, (g + 1) * C_HEAD_DIM)
        qg = jnp.concatenate([q_ref[rows, hd * C_HEAD_DIM:(hd + 1) * C_HEAD_DIM]
                              for hd in range(g * heads_per_group, (g + 1) * heads_per_group)], axis=0)
        s_prev = _scores_t(kv_ref[pl.ds(prev, C_WINDOW), kcols], qg)
        if sub == 0:
            s_prev = s_prev + jnp.where(r0 > 0, 0.0, NEG)
        return s_prev, _scores_t(kv_ref[pl.ds(own, C_WINDOW), kcols], qg)

    sts = [scores(u) for u in units[:SCORE_LOOKAHEAD]]
    for ui, (sub, g) in enumerate(units):
        if ui + SCORE_LOOKAHEAD < len(units):
            sts.append(scores(units[ui + SCORE_LOOKAHEAD]))
        s_prev, s_own = sts[ui]
        _, prev, own = chunks(sub)
        rows = slice(sub * SWA_QBLK, (sub + 1) * SWA_QBLK)
        v0 = (kw + g * C_HEAD_DIM) - LANES
        vt_prev = kvt_ref[jnp.right_shift(prev, shift)][v0:v0 + C_HEAD_DIM, :]
        vt_own = kvt_ref[jnp.right_shift(own, shift)][v0:v0 + C_HEAD_DIM, :]
        p_prev, p_own, inv_l = [], [], []
        for u in range(heads_per_group):
            cols = slice(u * SWA_QBLK, (u + 1) * SWA_QBLK)
            s_h = jnp.where(from_prev, s_prev[:, cols], s_own[:, cols])
            sink = sinks_ref[g * heads_per_group + u] * (1.0 / scale)
            m = jnp.maximum(jnp.max(s_h, axis=0, keepdims=True), sink)
            p = jnp.exp2((s_h - m) * c)
            inv_l.append(1.0 / (jnp.sum(p, axis=0, keepdims=True) + jnp.exp2((sink - m) * c)))
            pb = p.astype(BF16)
            p_prev.append(jnp.where(from_prev, pb, zero_b))
            p_own.append(jnp.where(from_prev, zero_b, pb))
        ot = (jnp.dot(vt_prev, jnp.concatenate(p_prev, axis=1), preferred_element_type=F32)
              + jnp.dot(vt_own, jnp.concatenate(p_own, axis=1), preferred_element_type=F32))
        ot = ot * jnp.concatenate(inv_l, axis=1)
        for pair in range(heads_per_group // 2):
            both = jnp.concatenate([ot[:, (2 * pair + u) * SWA_QBLK:(2 * pair + u + 1) * SWA_QBLK]
                                    for u in range(2)], axis=0)
            blk_i = (g * heads_per_group) // 2 + pair
            gate = gate_refs[blk_i][rows, :].astype(F32)
            o_ref[rows, blk_i * LANES:(blk_i + 1) * LANES] = (both.T * _silu(gate)).astype(o_ref.dtype)


def _swa_attention(parts, sinks):
    p_qkv = parts[0][0]
    b, s, _ = p_qkv.shape
    tq = min(s, SWA_TQ)
    qw = C_HEADS * C_HEAD_DIM
    kvw = 2 * C_KV_HEADS * C_HEAD_DIM
    assert (O_KV * LANES) % kvw == 0 and kvw == LANES + SWA_VT_WIDTH and p_qkv.shape[2] >= qw + kvw
    gate_arrays, gate_specs = [], []
    for u in range(SWA_GATE_BLOCKS):
        arr, local = _locate(parts, O_GATE + u)
        gate_arrays.append(arr)
        gate_specs.append(pl.BlockSpec((None, tq, LANES),
                                       functools.partial(lambda bi, i, sk, c: (bi, i, c), c=local)))
    scale = C_HEAD_DIM ** -0.5
    return pl.pallas_call(
        functools.partial(_swa_kernel, tq=tq, c=scale * LOG2E, scale=scale),
        grid_spec=pltpu.PrefetchScalarGridSpec(
            num_scalar_prefetch=1,
            grid=(b, s // tq),
            in_specs=[pl.BlockSpec((None, tq, qw), lambda bi, i, sk: (bi, i, 0)),
                      pl.BlockSpec((None, s, kvw), lambda bi, i, sk: (bi, 0, (O_KV * LANES) // kvw))]
                     + gate_specs,
            out_specs=pl.BlockSpec((None, tq, qw), lambda bi, i, sk: (bi, i, 0)),
            scratch_shapes=[pltpu.VMEM((s // LANES, SWA_VT_WIDTH, LANES), BF16)]),
        out_shape=jax.ShapeDtypeStruct((b, s, qw), BF16),
        compiler_params=_params(("arbitrary", "arbitrary")),
        name="swa_sink_attention",
    )(sinks, p_qkv, p_qkv, *gate_arrays)


def _outproj_kernel(*refs, widths, tn, emit_x):
    ny = len(widths)
    y_refs = refs[:ny]
    w_ref, x_ref, g_ref = refs[ny:ny + 3]
    ho_ref = refs[-1]
    hold_ref = refs[ny + 3] if emit_x else ho_ref
    tm, d = x_ref.shape
    ss = jnp.zeros((tm, 1), F32)
    for j in range(d // tn):
        cols = slice(j * tn, (j + 1) * tn)
        acc = x_ref[:, cols]
        off = 0
        for y_ref, wd in zip(y_refs, widths):
            acc = acc + jnp.dot(y_ref[...], w_ref[off:off + wd, cols], preferred_element_type=F32)
            off += wd
        hold_ref[:, cols] = acc
        ss = ss + jnp.sum(acc * acc, axis=1, keepdims=True)
    r = lax.rsqrt(ss * (1.0 / d) + EPS)
    for j in range(d // tn):
        cols = slice(j * tn, (j + 1) * tn)
        ho_ref[:, cols] = ((hold_ref[:, cols] * r) * g_ref[:, cols]).astype(ho_ref.dtype)


def _outproj(ys, w_bf16, x2d, g_next, final):
    t, d = x2d.shape
    widths = tuple(y.shape[1] for y in ys)
    kdim = w_bf16.shape[0]
    assert sum(widths) == kdim and w_bf16.shape[1] == d
    tm = min(t, OUT_TM)
    tn = min(d, OUT_TN)
    row_spec = pl.BlockSpec((tm, d), lambda i: (i, 0))
    in_specs = [pl.BlockSpec((tm, wd), lambda i: (i, 0)) for wd in widths]
    in_specs += [pl.BlockSpec((kdim, d), lambda i: (0, 0)), row_spec,
                 pl.BlockSpec((1, d), lambda i: (0, 0))]
    if final:
        out_specs = [row_spec]
        out_shape = [jax.ShapeDtypeStruct((t, d), F32)]
    else:
        out_specs = [row_spec, row_spec]
        out_shape = [jax.ShapeDtypeStruct((t, d), F32), jax.ShapeDtypeStruct((t, d), BF16)]
    res = pl.pallas_call(
        functools.partial(_outproj_kernel, widths=widths, tn=tn, emit_x=not final),
        grid=(t // tm,),
        in_specs=in_specs,
        out_specs=out_specs,
        out_shape=out_shape,
        compiler_params=_params(("arbitrary",)),
        name="outproj_final" if final else "outproj",
    )(*ys, w_bf16, x2d, g_next.reshape(1, d))
    return res[0] if final else (res[0], res[1])


def kernel(x, mem, positions, even_norm, even_w_in, even_w_mem_kv, even_w_out, odd_norm, odd_w_in,
           odd_w_mem_kv, odd_w_out, odd_sinks, mem_norm, final_norm):
    b, s, d = x.shape
    t = b * s
    depth = even_norm.shape[0] + odd_norm.shape[0]
    pos_col = positions.reshape(t, 1)
    kinds_e = _even_chunk_kinds()
    kinds_o = _odd_chunk_kinds()
    mem_n = _rmsnorm(mem.reshape(b * MEM_LEN, d), mem_norm)
    memkv_even = _memkv_all(mem_n, even_w_mem_kv, b)
    memkv_odd = _memkv_all(mem_n, odd_w_mem_kv, b)

    x2d = x.reshape(t, d)
    h, cos_e, sin_e, cos_o, sin_o = _rmsnorm_and_rope_tables(x2d, even_norm[0], pos_col)
    out = None
    for layer in range(depth):
        idx = layer // 2
        last = layer == depth - 1
        if last:
            g_next = final_norm
        elif layer % 2 == 0:
            g_next = odd_norm[idx]
        else:
            g_next = even_norm[idx + 1]
        if layer % 2 == 0:
            p3, w_out = _proj(h, even_w_in, idx, EVEN_TN, (kinds_e, cos_e, sin_e, HEAD_DIM), side_stack=even_w_out)
            p3 = p3.reshape(b, s, EVEN_IN)
            ys = [_dilated_attention(p3), _moba_attention(p3),
                  _mem_attention([(p3, 0)], memkv_even, idx, E_QM, E_GATE + 2 * A_HEADS)]
        else:
            wide = ODD_SPLIT_TILES * ODD_TN
            pa, w_out = _proj(h, odd_w_in, idx, wide // 2, (kinds_o[:wide // LANES], cos_o, sin_o, C_HEAD_DIM),
                              n_tiles=2, tm=ODD_WIDE_TM, side_stack=odd_w_out)
            pa = pa.reshape(b, s, wide)
            pb = _proj(h, odd_w_in, idx, ODD_TN, first_tile=ODD_SPLIT_TILES).reshape(b, s, ODD_IN - wide)
            parts = [(pa, 0), (pb, wide // LANES)]
            ys = [_swa_attention(parts, odd_sinks[idx]),
                  _mem_attention(parts, memkv_odd, idx, O_QM, O_GATE + SWA_GATE_BLOCKS)]
        ys = [y.reshape(t, y.shape[-1]) for y in ys]
        if last:
            out = _outproj(ys, w_out, x2d, g_next, final=True)
        else:
            x2d, h = _outproj(ys, w_out, x2d, g_next, final=False)
    return out.reshape(b, s, d)
```

```python
import functools
import math

import jax
import jax.numpy as jnp
from jax import lax
from jax.experimental import pallas as pl
from jax.experimental.pallas import tpu as pltpu

F32 = jnp.float32
BF16 = jnp.bfloat16

LANES = 128
HEAD_DIM = 128
MEM_LEN = 256
MEM_HEADS = 4
A_HEADS = 6
B_HEADS = 6
MOBA_BLOCK = 256
MOBA_TOPK = 3
C_HEADS = 24
C_KV_HEADS = 3
C_HEAD_DIM = 64
C_WINDOW = 128
ROPE_THETA = 10000.0
EPS = 1e-6
NEG = -1e30
LOG2E = 1.4426950408889634

EVEN_WIDTH = (A_HEADS + B_HEADS + MEM_HEADS) * HEAD_DIM
ODD_WIDTH = C_HEADS * C_HEAD_DIM + MEM_HEADS * HEAD_DIM
EVEN_IN = 3 * A_HEADS * HEAD_DIM + 3 * B_HEADS * HEAD_DIM + MEM_HEADS * HEAD_DIM + EVEN_WIDTH
ODD_IN = C_HEADS * C_HEAD_DIM + 2 * C_KV_HEADS * C_HEAD_DIM + MEM_HEADS * HEAD_DIM + ODD_WIDTH

E_QA, E_KA, E_VA = 0, A_HEADS, 2 * A_HEADS
E_QB, E_KB, E_VB = 3 * A_HEADS, 3 * A_HEADS + B_HEADS, 3 * A_HEADS + 2 * B_HEADS
E_QM = 3 * A_HEADS + 3 * B_HEADS
E_GATE = E_QM + MEM_HEADS
O_Q = 0
O_KV = (C_HEADS * C_HEAD_DIM) // LANES
O_QM = O_KV + (2 * C_KV_HEADS * C_HEAD_DIM) // LANES
O_GATE = O_QM + MEM_HEADS

PROJ_TM = 1024
EVEN_TN = 1024
MEMKV_TN = 512
ODD_TN = 896
ODD_SPLIT_TILES = 4
ODD_WIDE_TM = 512
OUT_TM, OUT_TN = 512, 512
NORM_TM = 1024
ATTN_BLK = 256
MEM_TQ = 512
SWA_TQ = 1024

VMEM_LIMIT = 56 * 1024 * 1024


def _params(sem):
    return pltpu.CompilerParams(dimension_semantics=sem, vmem_limit_bytes=VMEM_LIMIT)


def _silu(g):
    half = 0.5 * g
    return half + half * jnp.tanh(half)


def _rmsnorm_rope_kernel(x_ref, g_ref, pos_ref, invf_ref, o_ref, c_full_ref, s_full_ref, c_half_ref, s_half_ref):
    _rmsnorm_kernel(x_ref, g_ref, o_ref)
    ang = pos_ref[...].astype(F32) * invf_ref[...]
    cos = jnp.cos(ang)
    sin = jnp.sin(ang)
    lane = lax.broadcasted_iota(jnp.int32, cos.shape, 1)
    half, quarter = LANES // 2, LANES // 4

    def spread(t, sign):
        swapped = pltpu.roll(t, half, 1)
        full = jnp.where(lane < half, sign * t, swapped)
        narrow = jnp.where(lane < quarter, sign * swapped,
                           jnp.where(lane < half, pltpu.roll(t, LANES - quarter, 1),
                                     jnp.where(lane < half + quarter, sign * t, pltpu.roll(t, quarter, 1))))
        return full, narrow

    c_full_ref[...], c_half_ref[...] = spread(cos, 1.0)
    s_full_ref[...], s_half_ref[...] = spread(sin, -1.0)


def _rmsnorm_and_rope_tables(x2d, g, pos_col):
    t, d = x2d.shape

    def inv_freq(head_dim):
        return jnp.exp(jnp.arange(head_dim // 2, dtype=F32) * (-2.0 * math.log(ROPE_THETA) / head_dim))

    invf = jnp.concatenate([inv_freq(HEAD_DIM), inv_freq(C_HEAD_DIM),
                            jnp.zeros((LANES - HEAD_DIM // 2 - C_HEAD_DIM // 2,), F32)])[None, :]
    tm = min(t, NORM_TM)
    tab = pl.BlockSpec((tm, LANES), lambda i: (i, 0))
    return pl.pallas_call(
        _rmsnorm_rope_kernel,
        grid=(t // tm,),
        in_specs=[pl.BlockSpec((tm, d), lambda i: (i, 0)),
                  pl.BlockSpec((1, d), lambda i: (0, 0)),
                  pl.BlockSpec((tm, 1), lambda i: (i, 0)),
                  pl.BlockSpec((1, LANES), lambda i: (0, 0))],
        out_specs=[pl.BlockSpec((tm, d), lambda i: (i, 0))] + [tab] * 4,
        out_shape=[jax.ShapeDtypeStruct((t, d), BF16)] + [jax.ShapeDtypeStruct((t, LANES), F32)] * 4,
        compiler_params=_params(("arbitrary",)),
        name="rmsnorm_rope_tables",
    )(x2d, g.reshape(1, d), pos_col, invf)


def _rmsnorm_kernel(x_ref, g_ref, o_ref):
    x = x_ref[...]
    ms = jnp.mean(x * x, axis=-1, keepdims=True)
    o_ref[...] = ((x * lax.rsqrt(ms + EPS)) * g_ref[...]).astype(o_ref.dtype)


def _rmsnorm(x2d, g):
    t, d = x2d.shape
    tm = min(t, NORM_TM)
    return pl.pallas_call(
        _rmsnorm_kernel,
        grid=(t // tm,),
        in_specs=[pl.BlockSpec((tm, d), lambda i: (i, 0)),
                  pl.BlockSpec((1, d), lambda i: (0, 0))],
        out_specs=pl.BlockSpec((tm, d), lambda i: (i, 0)),
        out_shape=jax.ShapeDtypeStruct((t, d), BF16),
        compiler_params=_params(("arbitrary",)),
        name="rmsnorm",
    )(x2d, g.reshape(1, d))


ROPE_NONE, ROPE_FULL, ROPE_FIRST_HALF = 0, 1, 2


def _rope_partner(x, head_dim):
    if head_dim == LANES:
        return pltpu.roll(x, LANES // 2, 1)
    lane = lax.broadcasted_iota(jnp.int32, x.shape, 1)
    half = head_dim // 2
    return jnp.where((lane % head_dim) < half, pltpu.roll(x, LANES - half, 1), pltpu.roll(x, half, 1))


def _proj_kernel(types_ref, h_ref, w_ref, cos_ref, sin_ref, side_ref, o_ref, side_o_ref, wbf_ref, *, tn, head_dim):
    j = pl.program_id(0)
    i = pl.program_id(1)
    side_o_ref[...] = side_ref[...].astype(side_o_ref.dtype)

    @pl.when(i == 0)
    def _():
        wbf_ref[...] = w_ref[...].astype(BF16)

    nchunk = tn // LANES
    any_rope = types_ref[pl.num_programs(0) * nchunk + j]

    @pl.when(any_rope == 0)
    def _():
        o_ref[...] = jnp.dot(h_ref[...], wbf_ref[...], preferred_element_type=F32).astype(o_ref.dtype)

    @pl.when(any_rope != 0)
    def _():
        acc = jnp.dot(h_ref[...], wbf_ref[...], preferred_element_type=F32)
        cos = cos_ref[...]
        sin = sin_ref[...]
        upper = lax.broadcasted_iota(jnp.int32, cos.shape, 1) >= LANES // 2
        for c in range(nchunk):
            kind = types_ref[j * nchunk + c]
            x = acc[:, c * LANES:(c + 1) * LANES]
            roped = x * cos + _rope_partner(x, head_dim) * sin
            plain = jnp.logical_or(kind == ROPE_NONE, jnp.logical_and(kind == ROPE_FIRST_HALF, upper))
            o_ref[:, c * LANES:(c + 1) * LANES] = jnp.where(plain, x, roped).astype(o_ref.dtype)


def _proj_plain_kernel(h_ref, w_ref, o_ref, wbf_ref):
    @pl.when(pl.program_id(1) == 0)
    def _():
        wbf_ref[...] = w_ref[...].astype(BF16)

    o_ref[...] = jnp.dot(h_ref[...], wbf_ref[...], preferred_element_type=F32).astype(o_ref.dtype)


def _proj(h, w_stack, idx, tn, rope=None, first_tile=0, n_tiles=None, tm=PROJ_TM, side_stack=None):
    t, d = h.shape
    if n_tiles is None:
        n_tiles = w_stack.shape[2] // tn - first_tile
    assert tn % LANES == 0 and (first_tile + n_tiles) * tn <= w_stack.shape[2]
    n = n_tiles * tn
    tm = min(t, tm)
    grid = (n_tiles, t // tm)
    out_shape = jax.ShapeDtypeStruct((t, n), BF16)
    scratch = [pltpu.VMEM((d, tn), BF16)]
    if rope is None:
        return pl.pallas_call(
            _proj_plain_kernel,
            grid=grid,
            in_specs=[pl.BlockSpec((tm, d), lambda j, i: (i, 0)),
                      pl.BlockSpec((None, d, tn), lambda j, i: (idx, 0, first_tile + j))],
            out_specs=pl.BlockSpec((tm, tn), lambda j, i: (i, j)),
            out_shape=out_shape,
            scratch_shapes=scratch,
            compiler_params=_params(("arbitrary", "arbitrary")),
            name="proj_plain",
        )(h, w_stack)
    kinds, cos, sin, head_dim = rope
    per_tile = kinds.reshape(n_tiles, tn // LANES)
    kinds = jnp.concatenate([kinds, (per_tile != ROPE_NONE).any(axis=1).astype(jnp.int32)])
    side_rows, side_cols = side_stack.shape[1:]
    n_i = t // tm
    n_side = 1 << ((n_tiles * n_i).bit_length() - 1)
    assert side_rows % n_side == 0 and (side_rows // n_side) % 16 == 0

    def side_block(j, i):
        return jnp.minimum(j * n_i + i, n_side - 1)

    return pl.pallas_call(
        functools.partial(_proj_kernel, tn=tn, head_dim=head_dim),
        grid_spec=pltpu.PrefetchScalarGridSpec(
            num_scalar_prefetch=1,
            grid=grid,
            in_specs=[pl.BlockSpec((tm, d), lambda j, i, k: (i, 0)),
                      pl.BlockSpec((None, d, tn), lambda j, i, k: (idx, 0, first_tile + j)),
                      pl.BlockSpec((tm, LANES), lambda j, i, k: (i, 0)),
                      pl.BlockSpec((tm, LANES), lambda j, i, k: (i, 0)),
                      pl.BlockSpec((None, side_rows // n_side, side_cols),
                                   lambda j, i, k: (idx, side_block(j, i), 0))],
            out_specs=[pl.BlockSpec((tm, tn), lambda j, i, k: (i, j)),
                       pl.BlockSpec((side_rows // n_side, side_cols), lambda j, i, k: (side_block(j, i), 0))],
            scratch_shapes=scratch),
        out_shape=[out_shape, jax.ShapeDtypeStruct((side_rows, side_cols), BF16)],
        compiler_params=_params(("arbitrary", "arbitrary")),
        name=f"proj_rope{head_dim}",
    )(kinds, h, w_stack, cos, sin, side_stack)


def _even_chunk_kinds():
    kinds = [ROPE_NONE] * (EVEN_IN // LANES)
    for start in (E_QA, E_KA, E_QB, E_KB):
        for c in range(start, start + A_HEADS):
            kinds[c] = ROPE_FULL
    return jnp.asarray(kinds, jnp.int32)


def _odd_chunk_kinds():
    kinds = [ROPE_NONE] * (ODD_IN // LANES)
    k_end = C_HEADS * C_HEAD_DIM + C_KV_HEADS * C_HEAD_DIM
    for c in range(len(kinds)):
        if (c + 1) * LANES <= k_end:
            kinds[c] = ROPE_FULL
        elif c * LANES < k_end:
            assert k_end - c * LANES == LANES // 2
            kinds[c] = ROPE_FIRST_HALF
    return jnp.asarray(kinds, jnp.int32)


def _locate(parts, block):
    arr, first = [pt for pt in parts if pt[1] <= block][-1]
    assert block - first < arr.shape[2] // LANES
    return arr, block - first


def _scores_t(k, q):
    return lax.dot_general(k, q, (((1,), (1,)), ((), ())), preferred_element_type=F32)


def _transpose_bf16(x):
    return x.astype(F32).T.astype(BF16)


def _softmax_probs_t(tiles, biases, c):
    nq = tiles[0].shape[1]
    if nq > LANES:
        parts = []
        for q0 in range(0, nq, LANES):
            cols = slice(q0, q0 + LANES)
            sliced = {id(b): b[:, cols] for b in biases if b is not None}
            parts.append(_softmax_probs_t([s[:, cols] for s in tiles],
                                          [b if b is None else sliced[id(b)] for b in biases], c))
        return jnp.concatenate([p for p, _ in parts], axis=1), jnp.concatenate([r for _, r in parts], axis=1)
    m = None
    for s, b in zip(tiles, biases):
        if b is None:
            mt = jnp.max(s, axis=0, keepdims=True)
        elif b.shape[0] == 1:
            mt = jnp.max(s, axis=0, keepdims=True) + b
        else:
            mt = jnp.max(s + b, axis=0, keepdims=True)
        m = mt if m is None else jnp.maximum(m, mt)
    shifts = {}
    l = None
    ps = []
    for s, b in zip(tiles, biases):
        if b is None:
            shift = -m
        else:
            if id(b) not in shifts:
                shifts[id(b)] = b - m
            shift = shifts[id(b)]
        p = jnp.exp2((s + shift) * c)
        lt = jnp.sum(p, axis=0, keepdims=True)
        l = lt if l is None else l + lt
        ps.append(p.astype(BF16))
    p_all = ps[0] if len(ps) == 1 else jnp.concatenate(ps, axis=0)
    return p_all, 1.0 / l


SCORE_LOOKAHEAD = 2
DILATED_LOOKAHEAD = 4
MOBA_LOOKAHEAD = 1


def _interleave(generators):
    live = list(generators)
    while live:
        for gen in list(live):
            try:
                next(gen)
            except StopIteration:
                live.remove(gen)


HEADS_PER_STEP = 2


def _head_group_spec(s, first, heads):
    assert first % heads == 0
    return pl.BlockSpec((None, s, heads * LANES), lambda bi, h: (bi, 0, first // heads + h))


def _tile_delta(blk):
    row = lax.broadcasted_iota(jnp.int32, (blk, blk), 0)
    col = lax.broadcasted_iota(jnp.int32, (blk, blk), 1)
    return col - row


def _dilated_head(q_ref, k_ref, v_ref, g_ref, o_ref, cols, masks, *, blk, nblk, c, ahead):
    vt = _transpose_bf16(v_ref[:, cols])
    nears, far_bias = masks

    def scores(u):
        return _scores_t(k_ref[0:(u + 1) * blk, cols], q_ref[u * blk:(u + 1) * blk, cols])

    sts = [scores(u) for u in range(min(ahead, nblk))]
    yield
    for i in range(nblk):
        rows = slice(i * blk, (i + 1) * blk)
        ext = (i + 1) * blk
        if i + ahead < nblk:
            sts.append(scores(i + ahead))
        st = sts[i]
        tiles = [st[n * blk:(n + 1) * blk] for n in range(i + 1)]
        biases = [nears[i - n] if i - n < len(nears) else far_bias for n in range(i + 1)]
        p_all, inv_l = _softmax_probs_t(tiles, biases, c)
        ot = jnp.dot(vt[:, 0:ext], p_all, preferred_element_type=F32) * inv_l
        o_ref[rows, cols] = (ot.T * _silu(g_ref[rows, cols].astype(F32))).astype(o_ref.dtype)
        yield


def _dilated_kernel(q_ref, k_ref, v_ref, g_ref, o_ref, *, blk, nblk, c, ahead, heads):
    cr = _tile_delta(blk)

    def near(delta):
        dist = cr + delta * blk
        cnt = (jnp.where(dist <= 128, 1, 0)
               + jnp.where(jnp.logical_and((dist & 3) == 0, dist <= 512), 1, 0)
               + jnp.where((dist & 15) == 0, 1, 0))
        cnt = jnp.where(dist >= 0, cnt, 0)
        return jnp.where(cnt == 1, 0.0, jnp.where(cnt == 2, 1.0 / c, jnp.where(cnt == 3, math.log2(3.0) / c, NEG)))

    n_near = -(-512 // blk) + 1
    nears = [near(d) for d in range(min(n_near, nblk))]
    far_bias = jnp.where((cr & 15) == 0, 0.0, NEG)
    masks = (nears, far_bias)
    _interleave([_dilated_head(q_ref, k_ref, v_ref, g_ref, o_ref, slice(hh * LANES, (hh + 1) * LANES), masks,
                               blk=blk, nblk=nblk, c=c, ahead=ahead) for hh in range(heads)])


def _dilated_attention(p3):
    b, s, _ = p3.shape
    blk = ATTN_BLK
    heads = HEADS_PER_STEP
    assert s % blk == 0 and A_HEADS % heads == 0
    return pl.pallas_call(
        functools.partial(_dilated_kernel, blk=blk, nblk=s // blk, c=HEAD_DIM ** -0.5 * LOG2E,
                          ahead=DILATED_LOOKAHEAD, heads=heads),
        grid=(b, A_HEADS // heads),
        in_specs=[_head_group_spec(s, first, heads) for first in (E_QA, E_KA, E_VA, E_GATE)],
        out_specs=_head_group_spec(s, 0, heads),
        out_shape=jax.ShapeDtypeStruct((b, s, A_HEADS * HEAD_DIM), BF16),
        compiler_params=_params(("arbitrary", "arbitrary")),
        name="dilated_attention",
    )(p3, p3, p3, p3)


def _moba_head(q_ref, k_ref, v_ref, g_ref, o_ref, cols, *, blk, nblk, c):
    s_len = nblk * blk
    q = q_ref[:, cols]

    def scores(u):
        return _scores_t(k_ref[0:(u + 1) * blk, cols], q[u * blk:(u + 1) * blk, :])

    sts = [scores(u) for u in range(min(MOBA_LOOKAHEAD, nblk))]
    vt = _transpose_bf16(v_ref[:, cols])

    km = jnp.concatenate(
        [jnp.sum(k_ref[n * blk:(n + 1) * blk, cols].astype(F32), axis=0, keepdims=True) * (1.0 / blk)
         for n in range(nblk)], axis=0)
    hi = km.astype(BF16).astype(F32)
    ksplit = jnp.concatenate([hi, km - hi], axis=0).astype(BF16)
    gt2 = _scores_t(ksplit, q)
    gate = gt2[:nblk] + gt2[nblk:]

    nid = lax.broadcasted_iota(jnp.int32, (nblk, s_len), 0)
    own = jnp.right_shift(lax.broadcasted_iota(jnp.int32, (nblk, s_len), 1), blk.bit_length() - 1)
    rank = jnp.zeros((nblk, s_len), jnp.int32)
    for mm in range(nblk):
        gm = gate[mm:mm + 1, :]
        beats = jnp.logical_or(gm > gate, jnp.logical_and(gm == gate, mm < nid))
        rank = rank + jnp.where(jnp.logical_and(beats, mm < own), 1, 0)
    sel_bias = jnp.where(jnp.logical_and(rank < MOBA_TOPK, nid < own), 0.0, NEG)
    causal_bias = jnp.where(_tile_delta(blk) >= 0, 0.0, NEG)
    yield

    for i in range(nblk):
        rows = slice(i * blk, (i + 1) * blk)
        ext = (i + 1) * blk
        if i + MOBA_LOOKAHEAD < nblk:
            sts.append(scores(i + MOBA_LOOKAHEAD))
        st = sts[i]
        tiles = [st[n * blk:(n + 1) * blk] for n in range(i + 1)]
        biases = [sel_bias[n:n + 1, rows] for n in range(i)] + [causal_bias]
        p_all, inv_l = _softmax_probs_t(tiles, biases, c)
        ot = jnp.dot(vt[:, 0:ext], p_all, preferred_element_type=F32) * inv_l
        o_ref[rows, cols] = (ot.T * _silu(g_ref[rows, cols].astype(F32))).astype(o_ref.dtype)
        yield


def _moba_kernel(q_ref, k_ref, v_ref, g_ref, o_ref, *, blk, nblk, c, heads):
    _interleave([_moba_head(q_ref, k_ref, v_ref, g_ref, o_ref, slice(hh * LANES, (hh + 1) * LANES),
                            blk=blk, nblk=nblk, c=c) for hh in range(heads)])


def _moba_attention(p3):
    b, s, _ = p3.shape
    blk = MOBA_BLOCK
    nblk = s // blk
    heads = HEADS_PER_STEP
    assert nblk >= MOBA_TOPK and blk & (blk - 1) == 0 and B_HEADS % heads == 0
    return pl.pallas_call(
        functools.partial(_moba_kernel, blk=blk, nblk=nblk, c=HEAD_DIM ** -0.5 * LOG2E, heads=heads),
        grid=(b, B_HEADS // heads),
        in_specs=[_head_group_spec(s, first, heads) for first in (E_QB, E_KB, E_VB, E_GATE + A_HEADS)],
        out_specs=_head_group_spec(s, 0, heads),
        out_shape=jax.ShapeDtypeStruct((b, s, B_HEADS * HEAD_DIM), BF16),
        compiler_params=_params(("arbitrary", "arbitrary")),
        name="moba_attention",
    )(p3, p3, p3, p3)


def _mem_kernel(*refs, tq, c):
    q_refs = refs[:MEM_HEADS]
    g_refs = refs[MEM_HEADS:2 * MEM_HEADS]
    memkv_ref, o_ref = refs[2 * MEM_HEADS:]
    kvw = MEM_HEADS * HEAD_DIM
    units = [(h, j) for h in range(MEM_HEADS) for j in range(o_ref.shape[0] // tq)]

    def scores(unit):
        h, j = unit
        return _scores_t(memkv_ref[:, h * HEAD_DIM:(h + 1) * HEAD_DIM], q_refs[h][j * tq:(j + 1) * tq, :])

    sts = [scores(u) for u in units[:SCORE_LOOKAHEAD]]
    mvts = [_transpose_bf16(memkv_ref[:, kvw + h * HEAD_DIM:kvw + (h + 1) * HEAD_DIM]) for h in range(MEM_HEADS)]
    for ui, (h, j) in enumerate(units):
        rows = slice(j * tq, (j + 1) * tq)
        if ui + SCORE_LOOKAHEAD < len(units):
            sts.append(scores(units[ui + SCORE_LOOKAHEAD]))
        p_all, inv_l = _softmax_probs_t([sts[ui]], [None], c)
        ot = jnp.dot(mvts[h], p_all, preferred_element_type=F32) * inv_l
        gate = g_refs[h][rows, :].astype(F32)
        o_ref[rows, h * HEAD_DIM:(h + 1) * HEAD_DIM] = (ot.T * _silu(gate)).astype(o_ref.dtype)


def _memkv_kernel(h_ref, w_ref, o_ref):
    o_ref[...] = jnp.dot(h_ref[...], w_ref[...].astype(BF16), preferred_element_type=F32).astype(o_ref.dtype)


def _memkv_all(mem_n, w_stack, batch):
    t, d = mem_n.shape
    nl, _, n = w_stack.shape
    tn = min(n, MEMKV_TN)
    out = pl.pallas_call(
        _memkv_kernel,
        grid=(nl, n // tn),
        in_specs=[pl.BlockSpec((t, d), lambda l, j: (0, 0)),
                  pl.BlockSpec((None, d, tn), lambda l, j: (l, 0, j))],
        out_specs=pl.BlockSpec((None, t, tn), lambda l, j: (l, 0, j)),
        out_shape=jax.ShapeDtypeStruct((nl, t, n), BF16),
        compiler_params=_params(("arbitrary", "arbitrary")),
        name="memkv_proj",
    )(mem_n, w_stack)
    return out.reshape(nl, batch, MEM_LEN, n)


def _mem_attention(parts, memkv_stack, idx, q_block, gate_block):
    b, s, _ = parts[0][0].shape
    arrays, specs = [], []
    for first in (q_block, gate_block):
        for h in range(MEM_HEADS):
            arr, local = _locate(parts, first + h)
            arrays.append(arr)
            specs.append(pl.BlockSpec((None, s, LANES), functools.partial(lambda bi, c: (bi, 0, c), c=local)))
    width = MEM_HEADS * HEAD_DIM
    return pl.pallas_call(
        functools.partial(_mem_kernel, tq=min(s, MEM_TQ), c=HEAD_DIM ** -0.5 * LOG2E),
        grid=(b,),
        in_specs=specs + [pl.BlockSpec((None, None, MEM_LEN, 2 * width), lambda bi: (idx, bi, 0, 0))],
        out_specs=pl.BlockSpec((None, s, width), lambda bi: (bi, 0, 0)),
        out_shape=jax.ShapeDtypeStruct((b, s, width), BF16),
        compiler_params=_params(("arbitrary",)),
        name="memory_attention",
    )(*arrays, memkv_stack)


SWA_QBLK = C_WINDOW
SWA_GATE_BLOCKS = (C_HEADS * C_HEAD_DIM) // LANES
SWA_VT_WIDTH = 2 * LANES


def _swa_kernel(sinks_ref, q_ref, kv_ref, *rest, tq, c, scale):
    gate_refs = rest[:SWA_GATE_BLOCKS]
    o_ref, kvt_ref = rest[SWA_GATE_BLOCKS:]
    i = pl.program_id(1)
    heads_per_group = C_HEADS // C_KV_HEADS
    kw = C_KV_HEADS * C_HEAD_DIM

    @pl.when(i == 0)
    def _():
        for ch in range(kv_ref.shape[0] // LANES):
            t = kv_ref[ch * LANES:(ch + 1) * LANES, LANES:LANES + SWA_VT_WIDTH].astype(F32)
            kvt_ref[ch] = t.T.astype(BF16)

    krow = lax.broadcasted_iota(jnp.int32, (C_WINDOW, SWA_QBLK), 0)
    qcol = lax.broadcasted_iota(jnp.int32, (C_WINDOW, SWA_QBLK), 1)
    from_prev = krow > qcol
    zero_b = jnp.zeros((C_WINDOW, SWA_QBLK), BF16)
    shift = LANES.bit_length() - 1
    units = [(sub, g) for sub in range(tq // SWA_QBLK) for g in range(C_KV_HEADS)]

    def chunks(sub):
        r0 = i * tq + sub * SWA_QBLK
        prev = pl.multiple_of(jnp.maximum(r0 - C_WINDOW, 0), LANES)
        own = pl.multiple_of(r0, LANES)
        return r0, prev, own

    def scores(unit):
        sub, g = unit
        r0, prev, own = chunks(sub)
        rows = slice(sub * SWA_QBLK, (sub + 1) * SWA_QBLK)
        kcols = slice(g * C_HEAD_DIM, (g + 1) * C_HEAD_DIM)
        qg = jnp.concatenate([q_ref[rows, hd * C_HEAD_DIM:(hd + 1) * C_HEAD_DIM]
                              for hd in range(g * heads_per_group, (g + 1) * heads_per_group)], axis=0)
        s_prev = _scores_t(kv_ref[pl.ds(prev, C_WINDOW), kcols], qg)
        if sub == 0:
            s_prev = s_prev + jnp.where(r0 > 0, 0.0, NEG)
        return s_prev, _scores_t(kv_ref[pl.ds(own, C_WINDOW), kcols], qg)

    sts = [scores(u) for u in units[:SCORE_LOOKAHEAD]]
    for ui, (sub, g) in enumerate(units):
        if ui + SCORE_LOOKAHEAD < len(units):
            sts.append(scores(units[ui + SCORE_LOOKAHEAD]))
        s_prev, s_own = sts[ui]
        _, prev, own = chunks(sub)
        rows = slice(sub * SWA_QBLK, (sub + 1) * SWA_QBLK)
        v0 = (kw + g * C_HEAD_DIM) - LANES
        vt_prev = kvt_ref[jnp.right_shift(prev, shift)][v0:v0 + C_HEAD_DIM, :]
        vt_own = kvt_ref[jnp.right_shift(own, shift)][v0:v0 + C_HEAD_DIM, :]
        p_prev, p_own, inv_l = [], [], []
        for u in range(heads_per_group):
            cols = slice(u * SWA_QBLK, (u + 1) * SWA_QBLK)
            s_h = jnp.where(from_prev, s_prev[:, cols], s_own[:, cols])
            sink = sinks_ref[g * heads_per_group + u] * (1.0 / scale)
            m = jnp.maximum(jnp.max(s_h, axis=0, keepdims=True), sink)
            p = jnp.exp2((s_h - m) * c)
            inv_l.append(1.0 / (jnp.sum(p, axis=0, keepdims=True) + jnp.exp2((sink - m) * c)))
            pb = p.astype(BF16)
            p_prev.append(jnp.where(from_prev, pb, zero_b))
            p_own.append(jnp.where(from_prev, zero_b, pb))
        ot = (jnp.dot(vt_prev, jnp.concatenate(p_prev, axis=1), preferred_element_type=F32)
              + jnp.dot(vt_own, jnp.concatenate(p_own, axis=1), preferred_element_type=F32))
        ot = ot * jnp.concatenate(inv_l, axis=1)
        for pair in range(heads_per_group // 2):
            both = jnp.concatenate([ot[:, (2 * pair + u) * SWA_QBLK:(2 * pair + u + 1) * SWA_QBLK]
                                    for u in range(2)], axis=0)
            blk_i = (g * heads_per_group) // 2 + pair
            gate = gate_refs[blk_i][rows, :].astype(F32)
            o_ref[rows, blk_i * LANES:(blk_i + 1) * LANES] = (both.T * _silu(gate)).astype(o_ref.dtype)


def _swa_attention(parts, sinks):
    p_qkv = parts[0][0]
    b, s, _ = p_qkv.shape
    tq = min(s, SWA_TQ)
    qw = C_HEADS * C_HEAD_DIM
    kvw = 2 * C_KV_HEADS * C_HEAD_DIM
    assert (O_KV * LANES) % kvw == 0 and kvw == LANES + SWA_VT_WIDTH and p_qkv.shape[2] >= qw + kvw
    gate_arrays, gate_specs = [], []
    for u in range(SWA_GATE_BLOCKS):
        arr, local = _locate(parts, O_GATE + u)
        gate_arrays.append(arr)
        gate_specs.append(pl.BlockSpec((None, tq, LANES),
                                       functools.partial(lambda bi, i, sk, c: (bi, i, c), c=local)))
    scale = C_HEAD_DIM ** -0.5
    return pl.pallas_call(
        functools.partial(_swa_kernel, tq=tq, c=scale * LOG2E, scale=scale),
        grid_spec=pltpu.PrefetchScalarGridSpec(
            num_scalar_prefetch=1,
            grid=(b, s // tq),
            in_specs=[pl.BlockSpec((None, tq, qw), lambda bi, i, sk: (bi, i, 0)),
                      pl.BlockSpec((None, s, kvw), lambda bi, i, sk: (bi, 0, (O_KV * LANES) // kvw))]
                     + gate_specs,
            out_specs=pl.BlockSpec((None, tq, qw), lambda bi, i, sk: (bi, i, 0)),
            scratch_shapes=[pltpu.VMEM((s // LANES, SWA_VT_WIDTH, LANES), BF16)]),
        out_shape=jax.ShapeDtypeStruct((b, s, qw), BF16),
        compiler_params=_params(("arbitrary", "arbitrary")),
        name="swa_sink_attention",
    )(sinks, p_qkv, p_qkv, *gate_arrays)


def _outproj_kernel(*refs, widths, tn, emit_x):
    ny = len(widths)
    y_refs = refs[:ny]
    w_ref, x_ref, g_ref = refs[ny:ny + 3]
    ho_ref = refs[-1]
    hold_ref = refs[ny + 3] if emit_x else ho_ref
    tm, d = x_ref.shape
    ss = jnp.zeros((tm, 1), F32)
    for j in range(d // tn):
        cols = slice(j * tn, (j + 1) * tn)
        acc = x_ref[:, cols]
        off = 0
        for y_ref, wd in zip(y_refs, widths):
            acc = acc + jnp.dot(y_ref[...], w_ref[off:off + wd, cols], preferred_element_type=F32)
            off += wd
        hold_ref[:, cols] = acc
        ss = ss + jnp.sum(acc * acc, axis=1, keepdims=True)
    r = lax.rsqrt(ss * (1.0 / d) + EPS)
    for j in range(d // tn):
        cols = slice(j * tn, (j + 1) * tn)
        ho_ref[:, cols] = ((hold_ref[:, cols] * r) * g_ref[:, cols]).astype(ho_ref.dtype)


def _outproj(ys, w_bf16, x2d, g_next, final):
    t, d = x2d.shape
    widths = tuple(y.shape[1] for y in ys)
    kdim = w_bf16.shape[0]
    assert sum(widths) == kdim and w_bf16.shape[1] == d
    tm = min(t, OUT_TM)
    tn = min(d, OUT_TN)
    row_spec = pl.BlockSpec((tm, d), lambda i: (i, 0))
    in_specs = [pl.BlockSpec((tm, wd), lambda i: (i, 0)) for wd in widths]
    in_specs += [pl.BlockSpec((kdim, d), lambda i: (0, 0)), row_spec,
                 pl.BlockSpec((1, d), lambda i: (0, 0))]
    if final:
        out_specs = [row_spec]
        out_shape = [jax.ShapeDtypeStruct((t, d), F32)]
    else:
        out_specs = [row_spec, row_spec]
        out_shape = [jax.ShapeDtypeStruct((t, d), F32), jax.ShapeDtypeStruct((t, d), BF16)]
    res = pl.pallas_call(
        functools.partial(_outproj_kernel, widths=widths, tn=tn, emit_x=not final),
        grid=(t // tm,),
        in_specs=in_specs,
        out_specs=out_specs,
        out_shape=out_shape,
        compiler_params=_params(("arbitrary",)),
        name="outproj_final" if final else "outproj",
    )(*ys, w_bf16, x2d, g_next.reshape(1, d))
    return res[0] if final else (res[0], res[1])


def kernel(x, mem, positions, even_norm, even_w_in, even_w_mem_kv, even_w_out, odd_norm, odd_w_in,
           odd_w_mem_kv, odd_w_out, odd_sinks, mem_norm, final_norm):
    b, s, d = x.shape
    t = b * s
    depth = even_norm.shape[0] + odd_norm.shape[0]
    pos_col = positions.reshape(t, 1)
    kinds_e = _even_chunk_kinds()
    kinds_o = _odd_chunk_kinds()
    mem_n = _rmsnorm(mem.reshape(b * MEM_LEN, d), mem_norm)
    memkv_even = _memkv_all(mem_n, even_w_mem_kv, b)
    memkv_odd = _memkv_all(mem_n, odd_w_mem_kv, b)

    x2d = x.reshape(t, d)
    h, cos_e, sin_e, cos_o, sin_o = _rmsnorm_and_rope_tables(x2d, even_norm[0], pos_col)
    out = None
    for layer in range(depth):
        idx = layer // 2
        last = layer == depth - 1
        if last:
            g_next = final_norm
        elif layer % 2 == 0:
            g_next = odd_norm[idx]
        else:
            g_next = even_norm[idx + 1]
        if layer % 2 == 0:
            p3, w_out = _proj(h, even_w_in, idx, EVEN_TN, (kinds_e, cos_e, sin_e, HEAD_DIM), side_stack=even_w_out)
            p3 = p3.reshape(b, s, EVEN_IN)
            ys = [_dilated_attention(p3), _moba_attention(p3),
                  _mem_attention([(p3, 0)], memkv_even, idx, E_QM, E_GATE + 2 * A_HEADS)]
        else:
            wide = ODD_SPLIT_TILES * ODD_TN
            pa, w_out = _proj(h, odd_w_in, idx, wide // 2, (kinds_o[:wide // LANES], cos_o, sin_o, C_HEAD_DIM),
                              n_tiles=2, tm=ODD_WIDE_TM, side_stack=odd_w_out)
            pa = pa.reshape(b, s, wide)
            pb = _proj(h, odd_w_in, idx, ODD_TN, first_tile=ODD_SPLIT_TILES).reshape(b, s, ODD_IN - wide)
            parts = [(pa, 0), (pb, wide // LANES)]
            ys = [_swa_attention(parts, odd_sinks[idx]),
                  _mem_attention(parts, memkv_odd, idx, O_QM, O_GATE + SWA_GATE_BLOCKS)]
        ys = [y.reshape(t, y.shape[-1]) for y in ys]
        if last:
            out = _outproj(ys, w_out, x2d, g_next, final=True)
        else:
            x2d, h = _outproj(ys, w_out, x2d, g_next, final=False)
    return out.reshape(b, s, d)
```
